```python
import math
import jax
import jax.numpy as jnp
from jax import lax
import numpy as np

D_MODEL = 2048
BATCH = 8
SEQ = 2048
DEPTH = 2

EPS = 1e-6
D_FF = 5632
HG_HEADS = 4
HG_DK = 128
HG_DV = 128
HG_WIDTH = HG_HEADS * HG_DK
HG_CHUNK = 16
SSM_GROUP = 16
SSM_WIDTH = 512
SSM_GROUPS = SSM_WIDTH // SSM_GROUP
SSM_STATE = 64
NSA_HEADS = 16
NSA_KV_HEADS = 4
NSA_HPG = NSA_HEADS // NSA_KV_HEADS
NSA_DH = 64
NSA_WIDTH = NSA_HEADS * NSA_DH
NSA_KV_WIDTH = NSA_KV_HEADS * NSA_DH
CMP_LEN = 32
CMP_STRIDE = 16
SEL_LEN = 64
SEL_TOPN = 16
SEL_QCHUNK = 16
WIN = 512
WIN_QBLOCK = 128
FORCE_SCORE = 1e4
NEG = -1e30
REL_BUCKETS = 32
REL_MAX_DIST = 128
IN_SPLITS = (HG_WIDTH, HG_WIDTH, HG_WIDTH, HG_WIDTH, SSM_WIDTH, NSA_WIDTH,
             NSA_KV_WIDTH, NSA_KV_WIDTH, NSA_KV_WIDTH, NSA_KV_WIDTH, NSA_KV_WIDTH, NSA_KV_WIDTH,
             3 * NSA_HEADS, D_MODEL, D_MODEL, D_MODEL)
IN_WIDTH = sum(IN_SPLITS)

kernel_name = 'hybrid_gated_hgrn2_s5_nsa_block'


def rms_norm(x, gain):
    xf = x.astype(jnp.float32)
    y = xf * lax.rsqrt(jnp.mean(xf * xf, axis=-1, keepdims=True) + EPS)
    return (y * gain.astype(jnp.float32)).astype(x.dtype)


def head_rms(x, gain):
    return x * lax.rsqrt(jnp.mean(x * x, axis=-1, keepdims=True) + EPS) * gain.astype(jnp.float32)


def modulate(xn, shift, scale):
    return xn * (1.0 + scale) + shift


def swiglu(u, wi, wo):
    a = u @ wi
    return (jax.nn.silu(a[..., :D_FF]) * a[..., D_FF:]) @ wo


def split_cols(z, sizes):
    out, off = [], 0
    for s in sizes:
        out.append(z[..., off:off + s])
        off += s
    return out


def masked_softmax(logits, mask):
    p = jax.nn.softmax(jnp.where(mask, logits, NEG), axis=-1)
    return p * mask


def t5_bucket(dist):
    n = jnp.maximum(dist, 0)
    max_exact = REL_BUCKETS // 2
    nf = jnp.maximum(n, 1).astype(jnp.float32)
    large = max_exact + (jnp.log(nf / max_exact) / math.log(REL_MAX_DIST / max_exact)
                         * (REL_BUCKETS - max_exact)).astype(jnp.int32)
    large = jnp.minimum(large, REL_BUCKETS - 1)
    return jnp.where(n < max_exact, n, large)


def hgrn2(q, f_logit, i, g, lb, onorm):
    f32 = jnp.float32
    B_, S_, _ = q.shape
    H, DK, DV, C = HG_HEADS, HG_DK, HG_DV, HG_CHUNK
    nC = S_ // C
    q = jax.nn.silu(q.astype(f32))
    lb = lb.astype(f32)
    log_f = jnp.logaddexp(jnp.log(lb), jnp.log1p(-lb) + jax.nn.log_sigmoid(f_logit.astype(f32)))
    k = -jnp.expm1(log_f)

    def chunks(z, d):
        return z.reshape(B_, nC, C, H, d).transpose(1, 0, 3, 2, 4)

    qc, kc, ic = chunks(q, DK), chunks(k, DK), chunks(i.astype(f32), DV)
    bc = jnp.cumsum(chunks(log_f, DK), axis=-2)
    causal = jnp.tril(jnp.ones((C, C), dtype=bool))[:, :, None]

    def step(state, inp):
        qx, kx, ix, bx = inp
        dec = jnp.exp(jnp.where(causal, bx[..., :, None, :] - bx[..., None, :, :], -jnp.inf))
        scores = jnp.einsum('bhtk,bhsk,bhtsk->bhts', qx, kx, dec)
        o = scores @ ix + jnp.einsum('bhtk,bhkv->bhtv', qx * jnp.exp(bx), state)
        b_end = bx[..., -1, :]
        state = state * jnp.exp(b_end)[..., None] + jnp.einsum(
            'bhsk,bhsv->bhkv', kx * jnp.exp(b_end[..., None, :] - bx), ix)
        return state, o

    s0 = jnp.zeros((B_, H, DK, DV), f32)
    _, o = lax.scan(step, s0, (qc, kc, ic, bc))
    o = o.transpose(1, 0, 3, 2, 4).reshape(B_, S_, H, DV)
    o = head_rms(o, onorm) * jax.nn.silu(g.astype(f32).reshape(B_, S_, H, DV))
    return o.reshape(B_, S_, H * DV)


def s5_ssm(u, a_re, a_im, log_dt, b_re, b_im, c_re, c_im, d_skip):
    f32 = jnp.float32
    B_, S_, _ = u.shape
    G, P, N = SSM_GROUPS, SSM_GROUP, SSM_STATE
    uf = u.astype(f32).reshape(B_, S_, G, P)
    a_re = jnp.minimum(a_re.astype(f32), -1e-4)
    a_im = a_im.astype(f32)
    dt = jnp.exp(log_dt.astype(f32))[:, None]
    mag = jnp.exp(dt * a_re)
    ab_re, ab_im = mag * jnp.cos(dt * a_im), mag * jnp.sin(dt * a_im)
    den = a_re * a_re + a_im * a_im
    nr = ab_re - 1.0
    z_re = (nr * a_re + ab_im * a_im) / den
    z_im = (ab_im * a_re - nr * a_im) / den
    b_re, b_im = b_re.astype(f32), b_im.astype(f32)
    bb_re = z_re[..., None] * b_re - z_im[..., None] * b_im
    bb_im = z_re[..., None] * b_im + z_im[..., None] * b_re
    bu_re = jnp.einsum('bsgp,gnp->bsgn', uf, bb_re)
    bu_im = jnp.einsum('bsgp,gnp->bsgn', uf, bb_im)
    ar_s = jnp.broadcast_to(ab_re, (1, S_, G, N))
    ai_s = jnp.broadcast_to(ab_im, (1, S_, G, N))

    def combine(e1, e2):
        a1r, a1i, b1r, b1i = e1
        a2r, a2i, b2r, b2i = e2
        return (a2r * a1r - a2i * a1i, a2r * a1i + a2i * a1r,
                a2r * b1r - a2i * b1i + b2r, a2r * b1i + a2i * b1r + b2i)

    _, _, x_re, x_im = lax.associative_scan(combine, (ar_s, ai_s, bu_re, bu_im), axis=1)
    y = (jnp.einsum('bsgn,gpn->bsgp', x_re, c_re.astype(f32))
         - jnp.einsum('bsgn,gpn->bsgp', x_im, c_im.astype(f32))
         + d_skip.astype(f32).reshape(G, P) * uf)
    return y.reshape(B_, S_, G * P)


def nsa_attention(q, k_cmp, v_cmp, k_slc, v_slc, k_win, v_win, gate_logits,
                  q_gain, k_gain, pe_k, pe_v, phi_k, phi_v, rel_table):
    f32 = jnp.float32
    B_, S_, _ = q.shape
    G, HPG, DH = NSA_KV_HEADS, NSA_HPG, NSA_DH
    q = head_rms(q.astype(f32).reshape(B_, S_, G, HPG, DH), q_gain) * (DH ** -0.5)

    def kvr(z):
        return z.astype(f32).reshape(B_, S_, G, DH)

    k_slc, k_win = head_rms(kvr(k_slc), k_gain), head_rms(kvr(k_win), k_gain)
    v_cmp, v_slc, v_win, k_cmp = kvr(v_cmp), kvr(v_slc), kvr(v_win), kvr(k_cmp)
    rel_table = rel_table.astype(f32)
    pos = jnp.arange(S_)

    n_cmp = (S_ - CMP_LEN) // CMP_STRIDE + 1
    starts = jnp.arange(n_cmp) * CMP_STRIDE
    tok = starts[:, None] + jnp.arange(CMP_LEN)[None, :]
    kc = jnp.einsum('bnlgd,lde->bnge', k_cmp[:, tok] + pe_k.astype(f32)[:, None, :], phi_k.astype(f32))
    vc = jnp.einsum('bnlgd,lde->bnge', v_cmp[:, tok] + pe_v.astype(f32)[:, None, :], phi_v.astype(f32))
    kc = head_rms(kc, k_gain)
    dist_c = pos[:, None] - (starts + CMP_LEN - 1)[None, :]
    mask_c = dist_c >= 0
    bias_c = rel_table[t5_bucket(dist_c)].reshape(S_, n_cmp, G, HPG).transpose(0, 2, 3, 1)
    logit_c = jnp.einsum('btghd,bngd->btghn', q, kc) + bias_c
    p_cmp = masked_softmax(logit_c, mask_c[:, None, None, :])
    o_cmp = jnp.einsum('btghn,bngd->btghd', p_cmp, vc)

    n_sel = S_ // SEL_LEN
    n_top = min(SEL_TOPN, n_sel)
    sel_start = jnp.arange(n_sel) * SEL_LEN
    overlap = ((starts[:, None] < sel_start[None, :] + SEL_LEN)
               & (starts[:, None] + CMP_LEN > sel_start[None, :])).astype(f32)
    imp = jnp.einsum('btghn,nj->btgj', p_cmp, overlap)
    blk_t = pos // SEL_LEN
    jb = jnp.arange(n_sel)
    forced = (jb[None] == 0) | (jb[None] == blk_t[:, None]) | (jb[None] == blk_t[:, None] - 1)
    valid = jb[None] <= blk_t[:, None]
    imp = jnp.where(forced[None, :, None, :], FORCE_SCORE, imp)
    imp = jnp.where(valid[None, :, None, :], imp, NEG)
    _, sel_idx = lax.top_k(imp, n_top)

    ksb = k_slc.reshape(B_, n_sel, SEL_LEN, G, DH).transpose(0, 3, 1, 2, 4)
    vsb = v_slc.reshape(B_, n_sel, SEL_LEN, G, DH).transpose(0, 3, 1, 2, 4)
    rel_g = rel_table.reshape(REL_BUCKETS, G, HPG).transpose(1, 0, 2)
    nq = S_ // SEL_QCHUNK
    qs = q.reshape(B_, nq, SEL_QCHUNK, G, HPG, DH).transpose(1, 0, 2, 3, 4, 5)
    idx_s = sel_idx.reshape(B_, nq, SEL_QCHUNK, G, n_top).transpose(1, 0, 2, 3, 4)
    tpos_s = pos.reshape(nq, SEL_QCHUNK)
    bi = jnp.arange(B_)[:, None, None, None]
    gi = jnp.arange(G)[None, None, :, None]

    def sel_chunk(args):
        qx, ix, tp = args
        kx = ksb[bi, gi, ix]
        vx = vsb[bi, gi, ix]
        spos = ix[..., None] * SEL_LEN + jnp.arange(SEL_LEN)
        dist = tp[None, :, None, None, None] - spos
        bias = rel_g[gi[..., None], t5_bucket(dist)].transpose(0, 1, 2, 5, 3, 4)
        logits = jnp.einsum('bqghd,bqgnld->bqghnl', qx, kx) + bias
        m = (dist >= 0)[:, :, :, None]
        sh = logits.shape
        p = masked_softmax(logits.reshape(sh[:4] + (n_top * SEL_LEN,)),
                           jnp.broadcast_to(m, sh).reshape(sh[:4] + (n_top * SEL_LEN,)))
        return jnp.einsum('bqghm,bqgmd->bqghd', p, vx.reshape(sh[0], sh[1], G, n_top * SEL_LEN, DH))

    o_slc = lax.map(sel_chunk, (qs, idx_s, tpos_s))
    o_slc = o_slc.transpose(1, 0, 2, 3, 4, 5).reshape(B_, S_, G, HPG, DH)

    QB = WIN_QBLOCK
    KW = QB + WIN
    n_wb = S_ // QB
    kwp = jnp.pad(k_win, ((0, 0), (WIN, 0), (0, 0), (0, 0)))
    vwp = jnp.pad(v_win, ((0, 0), (WIN, 0), (0, 0), (0, 0)))
    rel = WIN + jnp.arange(QB)[:, None] - jnp.arange(KW)[None, :]
    band = (rel >= 0) & (rel < WIN)
    bias_w = rel_table[t5_bucket(rel)].reshape(QB, KW, G, HPG).transpose(0, 2, 3, 1)
    qw = q.reshape(B_, n_wb, QB, G, HPG, DH).transpose(1, 0, 2, 3, 4, 5)

    def win_block(args):
        ib, qx = args
        start = ib * QB
        kx = lax.dynamic_slice_in_dim(kwp, start, KW, axis=1)
        vx = lax.dynamic_slice_in_dim(vwp, start, KW, axis=1)
        spos = start - WIN + jnp.arange(KW)
        m = band & (spos >= 0)[None, :]
        logits = jnp.einsum('bqghd,bkgd->bqghk', qx, kx) + bias_w
        p = masked_softmax(logits, m[:, None, None, :])
        return jnp.einsum('bqghk,bkgd->bqghd', p, vx)

    o_win = lax.map(win_block, (jnp.arange(n_wb), qw))
    o_win = o_win.transpose(1, 0, 2, 3, 4, 5).reshape(B_, S_, G, HPG, DH)

    gt = jax.nn.sigmoid(gate_logits.astype(f32)).reshape(B_, S_, 3, G, HPG, 1)
    o = gt[:, :, 0] * o_cmp + gt[:, :, 1] * o_slc + gt[:, :, 2] * o_win
    return o.reshape(B_, S_, NSA_WIDTH)


def setup_inputs(seed: int = 0) -> dict:
    key = jax.random.key(seed)
    ks = jax.random.split(key, 32)
    f32 = jnp.float32
    L, D = DEPTH, D_MODEL
    G, P, N = SSM_GROUPS, SSM_GROUP, SSM_STATE

    def nrm(k, shape, s):
        return jax.random.normal(k, shape, f32) * s

    n_idx = jnp.arange(N, dtype=f32)
    return {
        'x': nrm(ks[0], (BATCH, SEQ, D), 1.0),
        'c': nrm(ks[1], (BATCH, D), 1.0),
        'ada_w': nrm(ks[2], (L, D, 9 * D), 0.5 * D ** -0.5),
        'ada_b': nrm(ks[3], (L, 9 * D), 0.01),
        'norm_g': 1.0 + nrm(ks[4], (L, 3, D), 0.01),
        'ffn1_wi': nrm(ks[5], (L, D, 2 * D_FF), D ** -0.5),
        'ffn1_wo': nrm(ks[6], (L, D_FF, D), D_FF ** -0.5),
        'ffn2_wi': nrm(ks[7], (L, D, 2 * D_FF), D ** -0.5),
        'ffn2_wo': nrm(ks[8], (L, D_FF, D), D_FF ** -0.5),
        'w_in': nrm(ks[9], (L, D, IN_WIDTH), D ** -0.5),
        'hg_lb_logits': nrm(ks[10], (L, HG_WIDTH), 1.0),
        'hg_onorm': 1.0 + nrm(ks[11], (L, HG_DV), 0.01),
        'hg_proj': nrm(ks[12], (L, HG_WIDTH, D), HG_WIDTH ** -0.5),
        'ssm_a_re': -0.5 + nrm(ks[13], (L, G, N), 0.01),
        'ssm_a_im': math.pi * n_idx + nrm(ks[14], (L, G, N), 0.01),
        'ssm_log_dt': jax.random.uniform(ks[15], (L, G), f32, math.log(1e-3), math.log(1e-1)),
        'ssm_b_re': nrm(ks[16], (L, G, N, P), (2 * P) ** -0.5),
        'ssm_b_im': nrm(ks[17], (L, G, N, P), (2 * P) ** -0.5),
        'ssm_c_re': nrm(ks[18], (L, G, P, N), N ** -0.5),
        'ssm_c_im': nrm(ks[19], (L, G, P, N), N ** -0.5),
        'ssm_d': nrm(ks[20], (L, SSM_WIDTH), 1.0),
        'ssm_glu_w': nrm(ks[21], (L, SSM_WIDTH, 2 * D), SSM_WIDTH ** -0.5),
        'nsa_q_gain': 1.0 + nrm(ks[22], (L, NSA_DH), 0.01),
        'nsa_k_gain': 1.0 + nrm(ks[23], (L, NSA_DH), 0.01),
        'nsa_pe_k': nrm(ks[24], (L, CMP_LEN, NSA_DH), 0.1),
        'nsa_pe_v': nrm(ks[25], (L, CMP_LEN, NSA_DH), 0.1),
        'nsa_phi_k': nrm(ks[26], (L, CMP_LEN, NSA_DH, NSA_DH), (CMP_LEN * NSA_DH) ** -0.5),
        'nsa_phi_v': nrm(ks[27], (L, CMP_LEN, NSA_DH, NSA_DH), (CMP_LEN * NSA_DH) ** -0.5),
        'nsa_proj': nrm(ks[28], (L, NSA_WIDTH, D), NSA_WIDTH ** -0.5),
        'rel_table': nrm(ks[29], (REL_BUCKETS, NSA_HEADS), 0.5),
        'w_out': nrm(ks[30], (L, D, D), D ** -0.5),
    }


def reference(x, c, ada_w, ada_b, norm_g, ffn1_wi, ffn1_wo, ffn2_wi, ffn2_wo, w_in,
              hg_lb_logits, hg_onorm, hg_proj, ssm_a_re, ssm_a_im, ssm_log_dt, ssm_b_re, ssm_b_im,
              ssm_c_re, ssm_c_im, ssm_d, ssm_glu_w, nsa_q_gain, nsa_k_gain, nsa_pe_k, nsa_pe_v,
              nsa_phi_k, nsa_phi_v, nsa_proj, rel_table, w_out):
    lb_cum = jnp.cumsum(jax.nn.softmax(hg_lb_logits.astype(jnp.float32), axis=0), axis=0)
    lower_bounds = lb_cum - lb_cum[0:1]
    c_act = jax.nn.silu(c)
    h = x
    for l in range(DEPTH):
        mod = c_act @ ada_w[l] + ada_b[l]
        sh1, sc1, g1, sh2, sc2, g2, sh3, sc3, g3 = [m[:, None, :] for m in jnp.split(mod, 9, axis=-1)]
        u = modulate(rms_norm(h, norm_g[l, 0]), sh1, sc1)
        h = h + 0.5 * g1 * swiglu(u, ffn1_wi[l], ffn1_wo[l])
        u = modulate(rms_norm(h, norm_g[l, 1]), sh2, sc2)
        (hq, hf, hi, hg, su, nq_, kc_, vc_, ks_, vs_, kw_, vw_, ngate,
         za, zb, zc) = split_cols(u @ w_in[l], IN_SPLITS)
        y_a = hgrn2(hq, hf, hi, hg, lower_bounds[l], hg_onorm[l]).astype(u.dtype) @ hg_proj[l]
        y_s = s5_ssm(su, ssm_a_re[l], ssm_a_im[l], ssm_log_dt[l], ssm_b_re[l], ssm_b_im[l],
                     ssm_c_re[l], ssm_c_im[l], ssm_d[l])
        zz = jax.nn.gelu(y_s).astype(u.dtype) @ ssm_glu_w[l]
        y_b = zz[..., :D_MODEL] * jax.nn.sigmoid(zz[..., D_MODEL:])
        y_c = nsa_attention(nq_, kc_, vc_, ks_, vs_, kw_, vw_, ngate, nsa_q_gain[l], nsa_k_gain[l],
                            nsa_pe_k[l], nsa_pe_v[l], nsa_phi_k[l], nsa_phi_v[l],
                            rel_table).astype(u.dtype) @ nsa_proj[l]
        merged = jax.nn.sigmoid(za) * y_a + jax.nn.sigmoid(zb) * y_b + jax.nn.sigmoid(zc) * y_c
        h = h + g2 * (merged @ w_out[l])
        u = modulate(rms_norm(h, norm_g[l, 2]), sh3, sc3)
        h = h + 0.5 * g3 * swiglu(u, ffn2_wi[l], ffn2_wo[l])
    return h
```

```python
import functools
import math

import jax
import jax.numpy as jnp
from jax import lax
from jax.experimental import pallas as pl
from jax.experimental.pallas import tpu as pltpu

F32 = jnp.float32
BF16 = jnp.bfloat16

EPS = 1e-6
NEG = -1e30
LOG2E = 1.4426950408889634
FORCE_SCORE = 1e4

HG_HEADS, HG_D, HG_CHUNK = 4, 128, 16
SSM_GROUPS, SSM_P, SSM_N, SSM_CHUNK = 32, 16, 64, 8
NSA_G, NSA_HPG, NSA_DH = 4, 4, 64
CMP_LEN, CMP_STRIDE, SEL_LEN, SEL_TOPN, WIN = 32, 16, 64, 16, 512
REL_BUCKETS, REL_MAX_DIST = 32, 128
QT = 128

VMEM_LIMIT = 56 * 1024 * 1024

OFF_ZA, OFF_ZB, OFF_ZC = 0, 2048, 4096
OFF_HQ, OFF_HF, OFF_HI, OFF_HG = 6144, 6656, 7168, 7680
OFF_SU, OFF_NQ = 8192, 8704
OFF_KC, OFF_VC, OFF_KS, OFF_VS, OFF_KW, OFF_VW = 9728, 9984, 10240, 10496, 10752, 11008
Z_WIDTH = 11264


def _cparams(sem):
    return pltpu.CompilerParams(dimension_semantics=sem, vmem_limit_bytes=VMEM_LIMIT)


def _dot(a, b):
    return jnp.dot(a, b, preferred_element_type=F32)


def _dot_nt(a, b):
    return lax.dot_general(a, b, (((1,), (1,)), ((), ())), preferred_element_type=F32)


def _dot_tn(a, b):
    return lax.dot_general(a, b, (((0,), (0,)), ((), ())), preferred_element_type=F32)


def _sigmoid(x):
    return 1.0 / (1.0 + jnp.exp(-x))


def _split3(x):
    hi = x.astype(BF16)
    r = x - hi.astype(F32)
    mid = r.astype(BF16)
    lo = (r - mid.astype(F32)).astype(BF16)
    return hi, mid, lo


def _norm_mod(x, gain, shift, scale):
    ms = jnp.mean(x * x, axis=-1, keepdims=True)
    y = x * lax.rsqrt(ms + EPS) * gain
    return y * (1.0 + scale) + shift


def _mod_kernel(c_ref, w_ref, b_ref, o_ref):
    c = c_ref[...]
    ca = (c * _sigmoid(c)).astype(BF16)
    o_ref[0] = _dot(ca, w_ref[0].astype(BF16)) + b_ref[0]


def _mods(c, ada_w, ada_b):
    L, D, W = ada_w.shape
    B = c.shape[0]
    tn = 1024
    return pl.pallas_call(
        _mod_kernel,
        grid=(L, W // tn),
        in_specs=[pl.BlockSpec((B, D), lambda l, j: (0, 0)),
                  pl.BlockSpec((1, D, tn), lambda l, j: (l, 0, j)),
                  pl.BlockSpec((1, 1, tn), lambda l, j: (l, 0, j))],
        out_specs=pl.BlockSpec((1, B, tn), lambda l, j: (l, 0, j)),
        out_shape=jax.ShapeDtypeStruct((L, B, W), F32),
        compiler_params=_cparams(("parallel", "parallel")),
        name="adaln_mod",
    )(c, ada_w, ada_b.reshape(L, 1, W))


def _ffn_kernel(h_ref, mod_ref, g_ref, wi1_ref, wi2_ref, wo_ref, o_ref, u_s, *, k0, nf):
    f = pl.program_id(1)

    @pl.when(f == 0)
    def _():
        u = _norm_mod(h_ref[...], g_ref[...], mod_ref[0, k0:k0 + 1, :], mod_ref[0, k0 + 1:k0 + 2, :])
        u_s[...] = u.astype(BF16)
        o_ref[...] = jnp.zeros_like(o_ref)

    u = u_s[...]
    a1 = _dot(u, wi1_ref[...])
    a2 = _dot(u, wi2_ref[...])
    act = (a1 * _sigmoid(a1) * a2).astype(BF16)
    o_ref[...] += _dot(act, wo_ref[...])

    @pl.when(f == nf - 1)
    def _():
        o_ref[...] = h_ref[...] + (0.5 * mod_ref[0, k0 + 2:k0 + 3, :]) * o_ref[...]


def _ffn(h, mod, gain, wi, wo, k0, seq):
    N, D = h.shape
    dff = wo.shape[0]
    tm, tf = 512, 512
    nf = dff // tf
    tpb = seq // tm
    return pl.pallas_call(
        functools.partial(_ffn_kernel, k0=k0, nf=nf),
        grid=(N // tm, nf),
        in_specs=[pl.BlockSpec((tm, D), lambda i, f: (i, 0)),
                  pl.BlockSpec((1, 9, D), lambda i, f: (i // tpb, 0, 0)),
                  pl.BlockSpec((1, D), lambda i, f: (0, 0)),
                  pl.BlockSpec((D, tf), lambda i, f: (0, f)),
                  pl.BlockSpec((D, tf), lambda i, f: (0, f + nf)),
                  pl.BlockSpec((tf, D), lambda i, f: (f, 0))],
        out_specs=pl.BlockSpec((tm, D), lambda i, f: (i, 0)),
        out_shape=jax.ShapeDtypeStruct((N, D), F32),
        scratch_shapes=[pltpu.VMEM((tm, D), BF16)],
        compiler_params=_cparams(("parallel", "arbitrary")),
        name="ffn",
    )(h, mod, gain, wi, wi, wo)


def _win_kernel(h_ref, mod_ref, g_ref, w_ref, wg_ref, z_ref, zg_ref, u_s):
    j = pl.program_id(1)

    @pl.when(j == 0)
    def _():
        u = _norm_mod(h_ref[...], g_ref[...], mod_ref[0, 3:4, :], mod_ref[0, 4:5, :])
        ub = u.astype(BF16)
        u_s[...] = ub
        zg_ref[...] = _dot(ub, wg_ref[...])

    z_ref[...] = _dot(u_s[...], w_ref[...])


def _win(h, mod, gain, w_main, w_gate, seq):
    N, D = h.shape
    tm, tn = 512, 1024
    tpb = seq // tm
    GW = w_gate.shape[1]
    return pl.pallas_call(
        _win_kernel,
        grid=(N // tm, Z_WIDTH // tn),
        in_specs=[pl.BlockSpec((tm, D), lambda i, j: (i, 0)),
                  pl.BlockSpec((1, 9, D), lambda i, j: (i // tpb, 0, 0)),
                  pl.BlockSpec((1, D), lambda i, j: (0, 0)),
                  pl.BlockSpec((D, tn), lambda i, j: (0, j)),
                  pl.BlockSpec((D, GW), lambda i, j: (0, 0))],
        out_specs=[pl.BlockSpec((tm, tn), lambda i, j: (i, j)),
                   pl.BlockSpec((tm, GW), lambda i, j: (i, 0))],
        out_shape=[jax.ShapeDtypeStruct((N, Z_WIDTH), F32),
                   jax.ShapeDtypeStruct((N, GW), F32)],
        scratch_shapes=[pltpu.VMEM((tm, D), BF16)],
        compiler_params=_cparams(("parallel", "arbitrary")),
        name="in_proj",
    )(h, mod, gain, w_main, w_gate)


def _hgrn_kernel(q_ref, f_ref, i_ref, g_ref, lb_ref, on_ref, o_ref, st_ref, *, tb):
    C = HG_CHUNK
    nc = tb // C

    @pl.when(pl.program_id(1) == 0)
    def _():
        st_ref[...] = jnp.zeros_like(st_ref)

    q = q_ref[...]
    qs = q * _sigmoid(q)
    x = f_ref[...]
    iv = i_ref[...]
    lb = lb_ref[...]
    sp = jnp.log1p(jnp.exp(-jnp.abs(x)))
    lsig = jnp.minimum(x, 0.0) - sp
    a = jnp.log(jnp.maximum(lb, 1e-38))
    bterm = jnp.log1p(-lb) + lsig
    lae = jnp.maximum(a, bterm) + jnp.log1p(jnp.exp(-jnp.abs(a - bterm)))
    log_f = jnp.where(lb > 0.0, lae, bterm)
    k = (1.0 - lb) * jnp.exp(jnp.minimum(-x, 0.0) - sp)

    r = lax.broadcasted_iota(jnp.int32, (tb, tb), 0)
    cidx = lax.broadcasted_iota(jnp.int32, (tb, tb), 1)
    lmat = jnp.where(((r // C) == (cidx // C)) & (cidx <= r), 1.0, 0.0).astype(BF16)
    hi, mid, lo = _split3(log_f)
    b = _dot(lmat, hi) + _dot(lmat, mid) + _dot(lmat, lo)

    tmod = lax.broadcasted_iota(jnp.int32, (tb, HG_D), 0) % C
    ones = jnp.ones((HG_D, HG_D), BF16)
    o = jnp.zeros((tb, HG_D), F32)
    for d in range(C):
        if d == 0:
            p = qs * k
            isd = iv
        else:
            kd = pltpu.roll(k, d, 0)
            bd = pltpu.roll(b, d, 0)
            isd = pltpu.roll(iv, d, 0)
            p = jnp.where(tmod >= d, qs * kd * jnp.exp(b - bd), 0.0)
        phi = p.astype(BF16)
        plo = (p - phi.astype(F32)).astype(BF16)
        rsum = _dot(phi, ones) + _dot(plo, ones)
        o = o + rsum * isd

    eb = jnp.exp(b)
    qe = (qs * eb).astype(BF16)
    b3 = b.reshape(nc, C, HG_D)
    bend = b3[:, C - 1:C, :]
    kdec = (k.reshape(nc, C, HG_D) * jnp.exp(bend - b3)).reshape(tb, HG_D).astype(BF16)
    ebend = jnp.exp(bend)
    ib = iv.astype(BF16)
    st = st_ref[...]
    outs = []
    for c in range(nc):
        sl = slice(c * C, (c + 1) * C)
        outs.append(_dot_nt(qe[sl], st.astype(BF16)))
        ut = _dot_tn(ib[sl], kdec[sl])
        st = st * ebend[c] + ut
    st_ref[...] = st
    o = o + jnp.concatenate(outs, axis=0)

    on = o * lax.rsqrt(jnp.mean(o * o, axis=-1, keepdims=True) + EPS) * on_ref[...]
    g = g_ref[...]
    o_ref[...] = (on * (g * _sigmoid(g))).astype(o_ref.dtype)


def _hgrn(z, lb, onorm, batch, seq):
    N = z.shape[0]
    tb = 256
    nb = seq // tb
    H = HG_HEADS

    def zspec(off):
        return pl.BlockSpec((tb, HG_D), lambda p, j: ((p // H) * nb + j, off // HG_D + p % H))

    return pl.pallas_call(
        functools.partial(_hgrn_kernel, tb=tb),
        grid=(batch * H, nb),
        in_specs=[zspec(OFF_HQ), zspec(OFF_HF), zspec(OFF_HI), zspec(OFF_HG),
                  pl.BlockSpec((1, HG_D), lambda p, j: (0, p % H)),
                  pl.BlockSpec((1, HG_D), lambda p, j: (0, 0))],
        out_specs=pl.BlockSpec((tb, HG_D), lambda p, j: ((p // H) * nb + j, p % H)),
        out_shape=jax.ShapeDtypeStruct((N, H * HG_D), BF16),
        scratch_shapes=[pltpu.VMEM((HG_D, HG_D), F32)],
        compiler_params=_cparams(("parallel", "arbitrary")),
        name="hgrn2",
    )(z, z, z, z, lb, onorm)


def _ssm_kernel(u_ref, bm_ref, cr_ref, ci_ref, kt_ref, p1r_ref, p1i_ref, p2r_ref, p2i_ref,
                alr_ref, ali_ref, d_ref, o_ref, xr_ref, xi_ref, xpr_ref, xpi_ref, *, tt):
    LC = SSM_CHUNK
    nc = tt // LC
    W = SSM_GROUPS * SSM_N

    @pl.when(pl.program_id(1) == 0)
    def _():
        xr_ref[...] = jnp.zeros_like(xr_ref)
        xi_ref[...] = jnp.zeros_like(xi_ref)

    u = u_ref[...]
    ub = u.astype(BF16)
    bu = _dot(ub, bm_ref[...])
    bur = bu[:, :W].reshape(nc, LC, W)
    bui = bu[:, W:].reshape(nc, LC, W)
    p1r, p1i = p1r_ref[...], p1i_ref[...]
    vr = jnp.sum(bur * p1r - bui * p1i, axis=1)
    vi = jnp.sum(bur * p1i + bui * p1r, axis=1)

    alr, ali = alr_ref[...], ali_ref[...]
    xr, xi = xr_ref[...], xi_ref[...]
    for c in range(nc):
        xpr_ref[c:c + 1, :] = xr
        xpi_ref[c:c + 1, :] = xi
        nr = alr * xr - ali * xi + vr[c:c + 1, :]
        ni = alr * xi + ali * xr + vi[c:c + 1, :]
        xr, xi = nr, ni
    xr_ref[...] = xr
    xi_ref[...] = xi

    xpr = xpr_ref[...][:, None, :]
    xpi = xpi_ref[...][:, None, :]
    p2r, p2i = p2r_ref[...], p2i_ref[...]
    zr = (p2r * xpr - p2i * xpi).reshape(tt, W).astype(BF16)
    zi = (p2r * xpi + p2i * xpr).reshape(tt, W).astype(BF16)
    y = _dot(zr, cr_ref[...]) + _dot(zi, ci_ref[...])

    tmod = lax.broadcasted_iota(jnp.int32, u.shape, 0) % LC
    for tau in range(LC):
        if tau == 0:
            us = ub
        else:
            us = jnp.where(tmod >= tau, pltpu.roll(u, tau, 0), 0.0).astype(BF16)
        y = y + _dot(us, kt_ref[tau])
    y = y + d_ref[...] * u
    g = 0.5 * y * (1.0 + jnp.tanh(0.7978845608028654 * (y + 0.044715 * (y * y * y))))
    o_ref[...] = g.astype(o_ref.dtype)


def _ssm_params(a_re, a_im, log_dt, b_re, b_im, c_re, c_im):
    G, P, N, LC = SSM_GROUPS, SSM_P, SSM_N, SSM_CHUNK
    hp = lax.Precision.HIGHEST
    a_re = jnp.minimum(a_re.astype(F32), -1e-4)
    a_im = a_im.astype(F32)
    dt = jnp.exp(log_dt.astype(F32))[:, None]
    mag = jnp.exp(dt * a_re)
    ab_re, ab_im = mag * jnp.cos(dt * a_im), mag * jnp.sin(dt * a_im)
    den = a_re * a_re + a_im * a_im
    nr = ab_re - 1.0
    z_re = (nr * a_re + ab_im * a_im) / den
    z_im = (ab_im * a_re - nr * a_im) / den
    b_re, b_im = b_re.astype(F32), b_im.astype(F32)
    bb_re = z_re[..., None] * b_re - z_im[..., None] * b_im
    bb_im = z_re[..., None] * b_im + z_im[..., None] * b_re
    kk = jnp.arange(LC + 1, dtype=F32)[:, None, None]
    pm = jnp.exp(kk * dt * a_re)
    pw_re, pw_im = pm * jnp.cos(kk * dt * a_im), pm * jnp.sin(kk * dt * a_im)
    eye = jnp.eye(G, dtype=F32)
    bm_re = jnp.einsum('gnp,gh->gphn', bb_re, eye).reshape(G * P, G * N)
    bm_im = jnp.einsum('gnp,gh->gphn', bb_im, eye).reshape(G * P, G * N)
    bmat = jnp.concatenate([bm_re, bm_im], axis=1)
    c_re, c_im = c_re.astype(F32), c_im.astype(F32)
    cr = jnp.einsum('gpn,gh->gnhp', c_re, eye).reshape(G * N, G * P)
    ci = -jnp.einsum('gpn,gh->gnhp', c_im, eye).reshape(G * N, G * P)
    t_re = pw_re[:LC, :, :, None] * bb_re[None] - pw_im[:LC, :, :, None] * bb_im[None]
    t_im = pw_re[:LC, :, :, None] * bb_im[None] + pw_im[:LC, :, :, None] * bb_re[None]
    kt = (jnp.einsum('gqn,tgnp->tgpq', c_re, t_re, precision=hp)
          - jnp.einsum('gqn,tgnp->tgpq', c_im, t_im, precision=hp))
    ktm = jnp.einsum('tgpq,gh->tgphq', kt, eye).reshape(LC, G * P, G * P)
    flat = lambda x: x.reshape(x.shape[0], G * N)
    p1r, p1i = flat(pw_re[LC - 1::-1][:LC]), flat(pw_im[LC - 1::-1][:LC])
    p2r, p2i = flat(pw_re[1:LC + 1]), flat(pw_im[1:LC + 1])
    alr, ali = flat(pw_re[LC:LC + 1]), flat(pw_im[LC:LC + 1])
    return (bmat.astype(BF16), cr.astype(BF16), ci.astype(BF16), ktm.astype(BF16),
            p1r, p1i, p2r, p2i, alr, ali)


def _ssm(z, params, d_skip, batch, seq):
    N = z.shape[0]
    tt = 256
    nb = seq // tt
    CW = SSM_GROUPS * SSM_P
    W = SSM_GROUPS * SSM_N
    LC = SSM_CHUNK
    bmat, cr, ci, ktm, p1r, p1i, p2r, p2i, alr, ali = params
    full = lambda a: pl.BlockSpec(a.shape, lambda b, j, _n=a.ndim: (0,) * _n)
    return pl.pallas_call(
        functools.partial(_ssm_kernel, tt=tt),
        grid=(batch, nb),
        in_specs=[pl.BlockSpec((tt, CW), lambda b, j: (b * nb + j, OFF_SU // CW)),
                  full(bmat), full(cr), full(ci), full(ktm), full(p1r), full(p1i),
                  full(p2r), full(p2i), full(alr), full(ali),
                  pl.BlockSpec((1, CW), lambda b, j: (0, 0))],
        out_specs=pl.BlockSpec((tt, CW), lambda b, j: (b * nb + j, 0)),
        out_shape=jax.ShapeDtypeStruct((N, CW), BF16),
        scratch_shapes=[pltpu.VMEM((1, W), F32), pltpu.VMEM((1, W), F32),
                        pltpu.VMEM((tt // LC, W), F32), pltpu.VMEM((tt // LC, W), F32)],
        compiler_params=_cparams(("parallel", "arbitrary")),
        name="s5_ssm",
    )(z, bmat, cr, ci, ktm, p1r, p1i, p2r, p2i, alr, ali, d_skip)


def _t5_bucket(dist):
    n = jnp.maximum(dist, 0)
    max_exact = REL_BUCKETS // 2
    nf = jnp.maximum(n, 1).astype(F32)
    large = max_exact + (jnp.log(nf / max_exact) / math.log(REL_MAX_DIST / max_exact)
                         * (REL_BUCKETS - max_exact)).astype(jnp.int32)
    large = jnp.minimum(large, REL_BUCKETS - 1)
    return jnp.where(n < max_exact, n, large)


def _bias_kernel(tab_ref, bk_ref, o_ref):
    h = pl.program_id(0)
    bk = bk_ref[...]
    base = tab_ref[REL_BUCKETS - 1, h]
    acc = jnp.full(bk.shape, NEG, F32)
    for k in range(REL_BUCKETS):
        acc = jnp.where(bk == k, (tab_ref[k, h] - base) * LOG2E, acc)
    o_ref[0] = acc


def _bias_tables(rel_table, seq):
    i = jnp.arange(QT)[:, None]
    j = jnp.arange(QT)[None, :]
    d0 = i - j
    b0 = jnp.where(d0 >= 0, _t5_bucket(d0), REL_BUCKETS)
    b1 = _t5_bucket(QT + i - j)
    t = jnp.arange(seq)[:, None]
    n = jnp.arange(QT)[None, :]
    n_cmp = (seq - CMP_LEN) // CMP_STRIDE + 1
    dc = t - (n * CMP_STRIDE + CMP_LEN - 1)
    bc = jnp.where((dc >= 0) & (n < n_cmp), _t5_bucket(dc), REL_BUCKETS)
    bk = jnp.concatenate([b0, b1, bc], axis=0).astype(jnp.int32)
    R = bk.shape[0]
    nh = rel_table.shape[1]
    return pl.pallas_call(
        _bias_kernel,
        grid=(nh,),
        in_specs=[pl.BlockSpec(memory_space=pltpu.SMEM),
                  pl.BlockSpec((R, QT), lambda h: (0, 0))],
        out_specs=pl.BlockSpec((1, R, QT), lambda h: (h, 0, 0)),
        out_shape=jax.ShapeDtypeStruct((nh, R, QT), F32),
        compiler_params=_cparams(("arbitrary",)),
        name="nsa_bias_tables",
    )(rel_table.astype(F32), bk)


def _head_rms(x, gain):
    return x * lax.rsqrt(jnp.mean(x * x, axis=-1, keepdims=True) + EPS) * gain


def _nsa_prep_kernel(kc_ref, vc_ref, ks_ref, vs_ref, kw_ref, vw_ref, kg_ref, pek_ref, pev_ref,
                     phik_ref, phiv_ref, oks_ref, ovs_ref, okw_ref, ovw_ref, okc_ref, ovc_ref,
                     xpad_ref, *, seq):
    DH = NSA_DH
    kg = kg_ref[...]
    srow = lax.broadcasted_iota(jnp.int32, (seq, DH), 0)
    lane = lax.broadcasted_iota(jnp.int32, (seq, DH), 1)
    onehot = jnp.where(lane == srow // SEL_LEN, 1.0, 0.0).astype(BF16)
    ones = jnp.ones((seq, DH), BF16)
    ones_c = jnp.ones((QT, DH), BF16)
    xpad_ref[seq:seq + CMP_LEN, :] = jnp.zeros((CMP_LEN, DH), F32)
    for g in range(NSA_G):
        ls = slice(g * DH, (g + 1) * DH)
        ksn = _head_rms(ks_ref[:, ls], kg).astype(BF16)
        oks_ref[0, g] = jnp.concatenate([ksn, onehot], axis=1)
        okw_ref[0, g] = _head_rms(kw_ref[:, ls], kg).astype(BF16)
        ovs_ref[0, g] = jnp.concatenate([vs_ref[:, ls].astype(BF16), ones], axis=1)
        ovw_ref[0, g] = jnp.concatenate([vw_ref[:, ls].astype(BF16), ones], axis=1)
        for src_ref, pe_ref, phi_ref, is_k in ((kc_ref, pek_ref, phik_ref, True),
                                               (vc_ref, pev_ref, phiv_ref, False)):
            xpad_ref[0:seq, :] = src_ref[:, ls]
            acc = jnp.zeros((QT, DH), F32)
            for l in range(CMP_LEN):
                xl = xpad_ref[pl.ds(l, QT, stride=CMP_STRIDE), :] + pe_ref[l:l + 1, :]
                acc = acc + _dot(xl.astype(BF16), phi_ref[l])
            if is_k:
                okc_ref[0, g] = _head_rms(acc, kg).astype(BF16)
            else:
                ovc_ref[0, g] = jnp.concatenate([acc.astype(BF16), ones_c], axis=1)


def _nsa_prep(z, k_gain, pe_k, pe_v, phi_k, phi_v, batch, seq):
    G, DH = NSA_G, NSA_DH
    KW = G * DH

    def zspec(off):
        return pl.BlockSpec((seq, KW), lambda b: (b, off // KW))

    full = lambda a: pl.BlockSpec(a.shape, lambda b, _n=a.ndim: (0,) * _n)

    def ospec(rows, w):
        return pl.BlockSpec((1, G, rows, w), lambda b: (b, 0, 0, 0))

    def oshape(rows, w):
        return jax.ShapeDtypeStruct((batch, G, rows, w), BF16)

    return pl.pallas_call(
        functools.partial(_nsa_prep_kernel, seq=seq),
        grid=(batch,),
        in_specs=[zspec(OFF_KC), zspec(OFF_VC), zspec(OFF_KS), zspec(OFF_VS), zspec(OFF_KW),
                  zspec(OFF_VW), full(k_gain), full(pe_k), full(pe_v), full(phi_k), full(phi_v)],
        out_specs=[ospec(seq, 2 * DH), ospec(seq, 2 * DH), ospec(seq, DH), ospec(seq, 2 * DH),
                   ospec(QT, DH), ospec(QT, 2 * DH)],
        out_shape=[oshape(seq, 2 * DH), oshape(seq, 2 * DH), oshape(seq, DH), oshape(seq, 2 * DH),
                   oshape(QT, DH), oshape(QT, 2 * DH)],
        scratch_shapes=[pltpu.VMEM((seq + CMP_LEN, DH), F32)],
        compiler_params=_cparams(("parallel",)),
        name="nsa_kv_prep",
    )(z, z, z, z, z, z, k_gain, pe_k, pe_v, phi_k, phi_v)


def _nsa_kernel(q_ref, gt_ref, ks_ref, vs_ref, kw_ref, vw_ref, kc_ref, vc_ref, bc_ref, nb_ref,
                qg_ref, ov_ref, o_ref, slog_ref, wlog_ref, mrun_ref, acc_ref):
    DH, HPG = NSA_DH, NSA_HPG
    R = HPG * QT
    qi = pl.program_id(1)
    t0 = qi * QT

    qblk = q_ref[...]
    qg = qg_ref[...] * (DH ** -0.5 * LOG2E)
    q4 = jnp.concatenate([_head_rms(qblk[:, h * DH:(h + 1) * DH], qg) for h in range(HPG)], axis=0)
    q4b = q4.astype(BF16)

    def finish(acc):
        l = acc[:, DH:DH + 1]
        return acc[:, :DH] / jnp.where(l > 0.0, l, 1.0)

    sc = _dot_nt(q4b, kc_ref[0, 0]) + bc_ref[:, 0].reshape(R, QT)
    mc = jnp.max(sc, axis=-1, keepdims=True)
    ec = jnp.where(sc > 0.5 * NEG, jnp.exp2(sc - mc), 0.0)
    lc = jnp.sum(ec, axis=-1, keepdims=True)
    pc = ec / jnp.where(lc > 0.0, lc, 1.0)
    o_cmp = _dot(pc.astype(BF16), vc_ref[0, 0])[:, :DH]

    psum = pc[0:QT] + pc[QT:2 * QT] + pc[2 * QT:3 * QT] + pc[3 * QT:4 * QT]
    ovm = ov_ref[...]
    hi, mid, lo = _split3(psum)
    imp_t = (_dot_nt(ovm, hi) + _dot_nt(ovm, mid) + _dot_nt(ovm, lo))[0:32, :]
    jrow = lax.broadcasted_iota(jnp.int32, (32, QT), 0)
    blk = (t0 + lax.broadcasted_iota(jnp.int32, (32, QT), 1)) // SEL_LEN
    forced = (jrow == 0) | (jrow == blk) | (jrow == blk - 1)
    imp_t = jnp.where(forced, FORCE_SCORE, imp_t)
    imp_t = jnp.where(jrow <= blk, imp_t, NEG)
    cnt = jnp.zeros((32, QT), F32)
    for jp in range(32):
        rowv = imp_t[jp:jp + 1, :]
        beats = (rowv > imp_t) | ((rowv == imp_t) & (jrow > jp))
        cnt = cnt + jnp.where(beats, 1.0, 0.0)
    selb_t = jnp.where(cnt < float(SEL_TOPN), 0.0, NEG)
    selb = jnp.concatenate([selb_t, jnp.zeros((QT - 32, QT), F32)], axis=0).T
    selb4 = jnp.concatenate([selb[:, :DH]] * HPG, axis=0).astype(BF16)
    qaug = jnp.concatenate([q4b, selb4], axis=1)

    nb0 = nb_ref[:, 0].reshape(R, QT)
    nb1 = nb_ref[:, 1].reshape(R, QT)
    mrun_ref[...] = jnp.full((R, QT), NEG, F32)

    def ktile(ref, kt):
        return ref[0, 0, pl.ds(pl.multiple_of(kt * QT, QT), QT), :]

    def sel_far(kt, carry):
        s = _dot_nt(qaug, ktile(ks_ref, kt))
        slog_ref[kt] = s
        mrun_ref[...] = jnp.maximum(mrun_ref[...], s)
        return carry

    lax.fori_loop(0, jnp.maximum(qi - 1, 0), sel_far, 0)

    @pl.when(qi >= 1)
    def _():
        s = _dot_nt(qaug, ktile(ks_ref, qi - 1)) + nb1
        slog_ref[qi - 1] = s
        mrun_ref[...] = jnp.maximum(mrun_ref[...], s)

    s = _dot_nt(qaug, ktile(ks_ref, qi)) + nb0
    slog_ref[qi] = s
    msel = jnp.max(jnp.maximum(mrun_ref[...], s), axis=-1, keepdims=True)

    acc_ref[...] = jnp.zeros_like(acc_ref)

    def sel_pv(kt, carry):
        p = jnp.exp2(slog_ref[kt] - msel).astype(BF16)
        acc_ref[...] += _dot(p, ktile(vs_ref, kt))
        return carry

    lax.fori_loop(0, qi + 1, sel_pv, 0)
    o_slc = finish(acc_ref[...])

    ii = lax.broadcasted_iota(jnp.int32, (R, QT), 0) % QT
    jj = lax.broadcasted_iota(jnp.int32, (R, QT), 1)
    nwt = WIN // QT
    mw = jnp.full((R, QT), NEG, F32)
    for dlt in range(nwt + 1):
        kt = jnp.maximum(qi - dlt, 0)
        s = _dot_nt(q4b, ktile(kw_ref, kt))
        if dlt == 0:
            s = s + nb0
        elif dlt == 1:
            s = s + nb1
        elif dlt == nwt:
            s = jnp.where(jj > ii, s, NEG)
        s = jnp.where(qi >= dlt, s, NEG)
        wlog_ref[dlt] = s
        mw = jnp.maximum(mw, s)
    mwin = jnp.max(mw, axis=-1, keepdims=True)
    accw = jnp.zeros((R, 2 * DH), F32)
    for dlt in range(nwt + 1):
        kt = jnp.maximum(qi - dlt, 0)
        p = jnp.exp2(wlog_ref[dlt] - mwin).astype(BF16)
        accw = accw + _dot(p, ktile(vw_ref, kt))
    o_win = finish(accw)

    gt = _sigmoid(gt_ref[...])
    outs = []
    for h in range(HPG):
        rs = slice(h * QT, (h + 1) * QT)
        outs.append(gt[:, h:h + 1] * o_cmp[rs] + gt[:, HPG + h:HPG + h + 1] * o_slc[rs]
                    + gt[:, 2 * HPG + h:2 * HPG + h + 1] * o_win[rs])
    o_ref[...] = jnp.concatenate(outs, axis=1).astype(o_ref.dtype)


def _nsa(z, zg, prep, bias, q_gain, ovm, batch, seq):
    N = z.shape[0]
    G, HPG, DH = NSA_G, NSA_HPG, NSA_DH
    nq = seq // QT
    R = HPG * QT
    oks, ovs, okw, ovw, okc, ovc = prep
    QW = HPG * DH

    def kvspec(rows, w):
        return pl.BlockSpec((1, 1, rows, w), lambda p, i: (p // G, p % G, 0, 0))

    return pl.pallas_call(
        _nsa_kernel,
        grid=(batch * G, nq),
        in_specs=[pl.BlockSpec((QT, QW), lambda p, i: ((p // G) * nq + i, OFF_NQ // QW + p % G)),
                  pl.BlockSpec((QT, 128), lambda p, i: ((p // G) * nq + i, p % G)),
                  kvspec(seq, 2 * DH), kvspec(seq, 2 * DH), kvspec(seq, DH), kvspec(seq, 2 * DH),
                  kvspec(QT, DH), kvspec(QT, 2 * DH),
                  pl.BlockSpec((HPG, 1, QT, QT), lambda p, i: (p % G, 2 + i, 0, 0)),
                  pl.BlockSpec((HPG, 2, QT, QT), lambda p, i: (p % G, 0, 0, 0)),
                  pl.BlockSpec((1, DH), lambda p, i: (0, 0)),
                  pl.BlockSpec((QT, QT), lambda p, i: (0, 0))],
        out_specs=pl.BlockSpec((QT, QW), lambda p, i: ((p // G) * nq + i, p % G)),
        out_shape=jax.ShapeDtypeStruct((N, G * QW), BF16),
        scratch_shapes=[pltpu.VMEM((nq, R, QT), F32), pltpu.VMEM((WIN // QT + 1, R, QT), F32),
                        pltpu.VMEM((R, QT), F32), pltpu.VMEM((R, 2 * DH), F32)],
        compiler_params=_cparams(("parallel", "arbitrary")),
        name="nsa_attention",
    )(z, zg, oks, ovs, okw, ovw, okc, ovc,
      bias.reshape(bias.shape[0], -1, QT, QT), bias[:, :2 * QT].reshape(-1, 2, QT, QT),
      q_gain, ovm)


def _merge_kernel(h_ref, mod_ref, za_ref, zb_ref, zc_ref, oa_ref, ys_ref, oc_ref,
                  wa_ref, wb_ref, wc_ref, wo_ref, o_ref):
    D = h_ref.shape[1]
    ya = _dot(oa_ref[...], wa_ref[...])
    zz = _dot(ys_ref[...], wb_ref[...])
    yb = zz[:, :D] * _sigmoid(zz[:, D:])
    yc = _dot(oc_ref[...], wc_ref[...])
    merged = _sigmoid(za_ref[...]) * ya + _sigmoid(zb_ref[...]) * yb + _sigmoid(zc_ref[...]) * yc
    o_ref[...] = h_ref[...] + mod_ref[0, 5:6, :] * _dot(merged.astype(BF16), wo_ref[...])


def _merge(h, mod, z, oa, ys, oc, wa, wb, wc, wo, seq):
    N, D = h.shape
    tm = 256
    tpb = seq // tm
    row = lambda w: pl.BlockSpec((tm, w), lambda i: (i, 0))
    res = lambda a: pl.BlockSpec(a.shape, lambda i: (0, 0), pipeline_mode=pl.Buffered(1))
    return pl.pallas_call(
        _merge_kernel,
        grid=(N // tm,),
        in_specs=[row(D),
                  pl.BlockSpec((1, 9, D), lambda i: (i // tpb, 0, 0)),
                  pl.BlockSpec((tm, D), lambda i: (i, OFF_ZA // D)),
                  pl.BlockSpec((tm, D), lambda i: (i, OFF_ZB // D)),
                  pl.BlockSpec((tm, D), lambda i: (i, OFF_ZC // D)),
                  row(oa.shape[1]), row(ys.shape[1]), row(oc.shape[1]),
                  res(wa), res(wb), res(wc), res(wo)],
        out_specs=row(D),
        out_shape=jax.ShapeDtypeStruct((N, D), F32),
        compiler_params=_cparams(("parallel",)),
        name="mix_merge",
    )(h, mod, z, z, z, oa, ys, oc, wa, wb, wc, wo)


def _permute_w_in(w):
    gate0 = 4 * 512 + 512 + 1024 + 6 * 256
    ngate = 3 * NSA_G * NSA_HPG
    main = jnp.concatenate([w[:, gate0 + ngate:], w[:, :gate0]], axis=1)
    gw = w[:, gate0:gate0 + ngate].reshape(w.shape[0], 3, NSA_G, NSA_HPG)
    gw = gw.transpose(0, 2, 1, 3).reshape(w.shape[0], NSA_G, 3 * NSA_HPG)
    gw = jnp.pad(gw, ((0, 0), (0, 0), (0, 128 - 3 * NSA_HPG))).reshape(w.shape[0], NSA_G * 128)
    return main.astype(BF16), gw.astype(BF16)


def _overlap_matrix():
    j = jnp.arange(QT)[:, None]
    n = jnp.arange(QT)[None, :]
    st = n * CMP_STRIDE
    ov = (st < j * SEL_LEN + SEL_LEN) & (st + CMP_LEN > j * SEL_LEN)
    return jnp.where(ov, 1.0, 0.0).astype(BF16)


def kernel(x, c, ada_w, ada_b, norm_g, ffn1_wi, ffn1_wo, ffn2_wi, ffn2_wo, w_in, hg_lb_logits,
           hg_onorm, hg_proj, ssm_a_re, ssm_a_im, ssm_log_dt, ssm_b_re, ssm_b_im, ssm_c_re,
           ssm_c_im, ssm_d, ssm_glu_w, nsa_q_gain, nsa_k_gain, nsa_pe_k, nsa_pe_v, nsa_phi_k,
           nsa_phi_v, nsa_proj, rel_table, w_out):
    B, S, D = x.shape
    L = ada_w.shape[0]
    N = B * S
    assert S % 512 == 0 and S // SEL_LEN == 32 and S // QT == 16
    lb_cum = jnp.cumsum(jax.nn.softmax(hg_lb_logits.astype(F32), axis=0), axis=0)
    lower_bounds = lb_cum - lb_cum[0:1]
    mods = _mods(c, ada_w, ada_b).reshape(L, B, 9, D)
    bias = _bias_tables(rel_table, S)
    ovm = _overlap_matrix()
    h = x.reshape(N, D)
    for l in range(L):
        mod = mods[l]
        h = _ffn(h, mod, norm_g[l, 0:1], ffn1_wi[l].astype(BF16), ffn1_wo[l].astype(BF16), 0, S)
        w_main, w_gate = _permute_w_in(w_in[l])
        z, zg = _win(h, mod, norm_g[l, 1:2], w_main, w_gate, S)
        oa = _hgrn(z, lower_bounds[l:l + 1], hg_onorm[l:l + 1], B, S)
        sp = _ssm_params(ssm_a_re[l], ssm_a_im[l], ssm_log_dt[l], ssm_b_re[l], ssm_b_im[l],
                         ssm_c_re[l], ssm_c_im[l])
        ys = _ssm(z, sp, ssm_d[l:l + 1], B, S)
        prep = _nsa_prep(z, nsa_k_gain[l:l + 1], nsa_pe_k[l], nsa_pe_v[l],
                         nsa_phi_k[l].astype(BF16), nsa_phi_v[l].astype(BF16), B, S)
        oc = _nsa(z, zg, prep, bias, nsa_q_gain[l:l + 1], ovm, B, S)
        h = _merge(h, mod, z, oa, ys, oc, hg_proj[l].astype(BF16), ssm_glu_w[l].astype(BF16),
                   nsa_proj[l].astype(BF16), w_out[l].astype(BF16), S)
        h = _ffn(h, mod, norm_g[l, 2:3], ffn2_wi[l].astype(BF16), ffn2_wo[l].astype(BF16), 6, S)
    return h.reshape(B, S, D)
```

```python
import functools
import math

import jax
import jax.numpy as jnp
from jax import lax
from jax.experimental import pallas as pl
from jax.experimental.pallas import tpu as pltpu

F32 = jnp.float32
BF16 = jnp.bfloat16

EPS = 1e-6
NEG = -1e30
LOG2E = 1.4426950408889634
FORCE_SCORE = 1e4

HG_HEADS, HG_D, HG_CHUNK = 4, 128, 16
SSM_GROUPS, SSM_P, SSM_N, SSM_CHUNK = 32, 16, 64, 8
NSA_G, NSA_HPG, NSA_DH = 4, 4, 64
CMP_LEN, CMP_STRIDE, SEL_LEN, SEL_TOPN, WIN = 32, 16, 64, 16, 512
REL_BUCKETS, REL_MAX_DIST = 32, 128
QT = 128
NSA_TQ = 512

VMEM_LIMIT = 56 * 1024 * 1024

OFF_ZA, OFF_ZB, OFF_ZC = 0, 2048, 4096
OFF_HQ, OFF_HF, OFF_HI, OFF_HG = 6144, 6656, 7168, 7680
OFF_SU, OFF_NQ = 8192, 8704
OFF_KC, OFF_VC, OFF_KS, OFF_VS, OFF_KW, OFF_VW = 9728, 9984, 10240, 10496, 10752, 11008
Z_WIDTH = 11264


def _cparams(sem):
    return pltpu.CompilerParams(dimension_semantics=sem, vmem_limit_bytes=VMEM_LIMIT)


def _dot(a, b):
    return jnp.dot(a, b, preferred_element_type=F32)


def _dot_nt(a, b):
    return lax.dot_general(a, b, (((1,), (1,)), ((), ())), preferred_element_type=F32)


def _dot_tn(a, b):
    return lax.dot_general(a, b, (((0,), (0,)), ((), ())), preferred_element_type=F32)


def _sigmoid(x):
    return 1.0 / (1.0 + jnp.exp(-x))


def _split3(x):
    hi = x.astype(BF16)
    r = x - hi.astype(F32)
    mid = r.astype(BF16)
    lo = (r - mid.astype(F32)).astype(BF16)
    return hi, mid, lo


def _norm_mod(x, gain, shift, scale):
    ms = jnp.mean(x * x, axis=-1, keepdims=True)
    y = x * lax.rsqrt(ms + EPS) * gain
    return y * (1.0 + scale) + shift


def _mod_kernel(c_ref, w_ref, b_ref, o_ref):
    c = c_ref[...]
    ca = (c * _sigmoid(c)).astype(BF16)
    o_ref[0] = _dot(ca, w_ref[0].astype(BF16)) + b_ref[0]


def _mods(c, ada_w, ada_b):
    L, D, W = ada_w.shape
    B = c.shape[0]
    tn = 1024
    return pl.pallas_call(
        _mod_kernel,
        grid=(L, W // tn),
        in_specs=[pl.BlockSpec((B, D), lambda l, j: (0, 0)),
                  pl.BlockSpec((1, D, tn), lambda l, j: (l, 0, j)),
                  pl.BlockSpec((1, 1, tn), lambda l, j: (l, 0, j))],
        out_specs=pl.BlockSpec((1, B, tn), lambda l, j: (l, 0, j)),
        out_shape=jax.ShapeDtypeStruct((L, B, W), F32),
        compiler_params=_cparams(("parallel", "parallel")),
        name="adaln_mod",
    )(c, ada_w, ada_b.reshape(L, 1, W))


def _ffn_kernel(h_ref, mod_ref, g_ref, wi1_ref, wi2_ref, wo_ref, o_ref, u_s, *, k0, nf):
    f = pl.program_id(1)

    @pl.when(f == 0)
    def _():
        u = _norm_mod(h_ref[...], g_ref[...], mod_ref[0, k0:k0 + 1, :], mod_ref[0, k0 + 1:k0 + 2, :])
        u_s[...] = u.astype(BF16)
        o_ref[...] = jnp.zeros_like(o_ref)

    u = u_s[...]
    a1 = _dot(u, wi1_ref[...])
    a2 = _dot(u, wi2_ref[...])
    act = (a1 * _sigmoid(a1) * a2).astype(BF16)
    o_ref[...] += _dot(act, wo_ref[...])

    @pl.when(f == nf - 1)
    def _():
        o_ref[...] = h_ref[...] + (0.5 * mod_ref[0, k0 + 2:k0 + 3, :]) * o_ref[...]


def _ffn(h, mod, gain, wi, wo, k0, seq):
    N, D = h.shape
    dff = wo.shape[0]
    tm, tf = 512, 512
    nf = dff // tf
    tpb = seq // tm
    return pl.pallas_call(
        functools.partial(_ffn_kernel, k0=k0, nf=nf),
        grid=(N // tm, nf),
        in_specs=[pl.BlockSpec((tm, D), lambda i, f: (i, 0)),
                  pl.BlockSpec((1, 9, D), lambda i, f: (i // tpb, 0, 0)),
                  pl.BlockSpec((1, D), lambda i, f: (0, 0)),
                  pl.BlockSpec((D, tf), lambda i, f: (0, f)),
                  pl.BlockSpec((D, tf), lambda i, f: (0, f + nf)),
                  pl.BlockSpec((tf, D), lambda i, f: (f, 0))],
        out_specs=pl.BlockSpec((tm, D), lambda i, f: (i, 0)),
        out_shape=jax.ShapeDtypeStruct((N, D), F32),
        scratch_shapes=[pltpu.VMEM((tm, D), BF16)],
        compiler_params=_cparams(("parallel", "arbitrary")),
        name="ffn",
    )(h, mod, gain, wi, wi, wo)


def _win_kernel(h_ref, mod_ref, g_ref, w_ref, wg_ref, z_ref, zg_ref, u_s):
    j = pl.program_id(1)

    @pl.when(j == 0)
    def _():
        u = _norm_mod(h_ref[...], g_ref[...], mod_ref[0, 3:4, :], mod_ref[0, 4:5, :])
        ub = u.astype(BF16)
        u_s[...] = ub
        zg_ref[...] = _dot(ub, wg_ref[...])

    z_ref[...] = _dot(u_s[...], w_ref[...])


def _win(h, mod, gain, w_main, w_gate, seq):
    N, D = h.shape
    tm, tn = 512, 1024
    tpb = seq // tm
    GW = w_gate.shape[1]
    return pl.pallas_call(
        _win_kernel,
        grid=(N // tm, Z_WIDTH // tn),
        in_specs=[pl.BlockSpec((tm, D), lambda i, j: (i, 0)),
                  pl.BlockSpec((1, 9, D), lambda i, j: (i // tpb, 0, 0)),
                  pl.BlockSpec((1, D), lambda i, j: (0, 0)),
                  pl.BlockSpec((D, tn), lambda i, j: (0, j)),
                  pl.BlockSpec((D, GW), lambda i, j: (0, 0))],
        out_specs=[pl.BlockSpec((tm, tn), lambda i, j: (i, j)),
                   pl.BlockSpec((tm, GW), lambda i, j: (i, 0))],
        out_shape=[jax.ShapeDtypeStruct((N, Z_WIDTH), F32),
                   jax.ShapeDtypeStruct((N, GW), F32)],
        scratch_shapes=[pltpu.VMEM((tm, D), BF16)],
        compiler_params=_cparams(("parallel", "arbitrary")),
        name="in_proj",
    )(h, mod, gain, w_main, w_gate)


def _hgrn_kernel(q_ref, f_ref, i_ref, g_ref, lb_ref, on_ref, o_ref, st_ref, *, tb):
    C = HG_CHUNK
    nc = tb // C

    @pl.when(pl.program_id(1) == 0)
    def _():
        st_ref[...] = jnp.zeros_like(st_ref)

    q = q_ref[...]
    qs = q * _sigmoid(q)
    x = f_ref[...]
    iv = i_ref[...]
    lb = lb_ref[...]
    sp = jnp.log1p(jnp.exp(-jnp.abs(x)))
    lsig = jnp.minimum(x, 0.0) - sp
    a = jnp.log(jnp.maximum(lb, 1e-38))
    bterm = jnp.log1p(-lb) + lsig
    lae = jnp.maximum(a, bterm) + jnp.log1p(jnp.exp(-jnp.abs(a - bterm)))
    log_f = jnp.where(lb > 0.0, lae, bterm)
    k = (1.0 - lb) * jnp.exp(jnp.minimum(-x, 0.0) - sp)

    r = lax.broadcasted_iota(jnp.int32, (tb, tb), 0)
    cidx = lax.broadcasted_iota(jnp.int32, (tb, tb), 1)
    lmat = jnp.where(((r // C) == (cidx // C)) & (cidx <= r), 1.0, 0.0).astype(BF16)
    hi, mid, lo = _split3(log_f)
    b = _dot(lmat, hi) + _dot(lmat, mid) + _dot(lmat, lo)

    tmod = lax.broadcasted_iota(jnp.int32, (tb, HG_D), 0) % C
    ones = jnp.ones((HG_D, HG_D), BF16)
    o = jnp.zeros((tb, HG_D), F32)
    for d in range(C):
        if d == 0:
            p = qs * k
            isd = iv
        else:
            kd = pltpu.roll(k, d, 0)
            bd = pltpu.roll(b, d, 0)
            isd = pltpu.roll(iv, d, 0)
            p = jnp.where(tmod >= d, qs * kd * jnp.exp(b - bd), 0.0)
        phi = p.astype(BF16)
        plo = (p - phi.astype(F32)).astype(BF16)
        rsum = _dot(phi, ones) + _dot(plo, ones)
        o = o + rsum * isd

    eb = jnp.exp(b)
    qe = (qs * eb).astype(BF16)
    b3 = b.reshape(nc, C, HG_D)
    bend = b3[:, C - 1:C, :]
    kdec = (k.reshape(nc, C, HG_D) * jnp.exp(bend - b3)).reshape(tb, HG_D).astype(BF16)
    ebend = jnp.exp(bend)
    ib = iv.astype(BF16)
    st = st_ref[...]
    outs = []
    for c in range(nc):
        sl = slice(c * C, (c + 1) * C)
        outs.append(_dot_nt(qe[sl], st.astype(BF16)))
        ut = _dot_tn(ib[sl], kdec[sl])
        st = st * ebend[c] + ut
    st_ref[...] = st
    o = o + jnp.concatenate(outs, axis=0)

    on = o * lax.rsqrt(jnp.mean(o * o, axis=-1, keepdims=True) + EPS) * on_ref[...]
    g = g_ref[...]
    o_ref[...] = (on * (g * _sigmoid(g))).astype(o_ref.dtype)


def _hgrn(z, lb, onorm, batch, seq):
    N = z.shape[0]
    tb = 256
    nb = seq // tb
    H = HG_HEADS

    def zspec(off):
        return pl.BlockSpec((tb, HG_D), lambda p, j: ((p // H) * nb + j, off // HG_D + p % H))

    return pl.pallas_call(
        functools.partial(_hgrn_kernel, tb=tb),
        grid=(batch * H, nb),
        in_specs=[zspec(OFF_HQ), zspec(OFF_HF), zspec(OFF_HI), zspec(OFF_HG),
                  pl.BlockSpec((1, HG_D), lambda p, j: (0, p % H)),
                  pl.BlockSpec((1, HG_D), lambda p, j: (0, 0))],
        out_specs=pl.BlockSpec((tb, HG_D), lambda p, j: ((p // H) * nb + j, p % H)),
        out_shape=jax.ShapeDtypeStruct((N, H * HG_D), BF16),
        scratch_shapes=[pltpu.VMEM((HG_D, HG_D), F32)],
        compiler_params=_cparams(("parallel", "arbitrary")),
        name="hgrn2",
    )(z, z, z, z, lb, onorm)


def _ssm_kernel(u_ref, bm_ref, cr_ref, ci_ref, kt_ref, p1r_ref, p1i_ref, p2r_ref, p2i_ref,
                alr_ref, ali_ref, d_ref, o_ref, xr_ref, xi_ref, xpr_ref, xpi_ref, *, tt):
    LC = SSM_CHUNK
    nc = tt // LC
    W = SSM_GROUPS * SSM_N

    @pl.when(pl.program_id(1) == 0)
    def _():
        xr_ref[...] = jnp.zeros_like(xr_ref)
        xi_ref[...] = jnp.zeros_like(xi_ref)

    u = u_ref[...]
    ub = u.astype(BF16)
    bu = _dot(ub, bm_ref[...])
    bur = bu[:, :W].reshape(nc, LC, W)
    bui = bu[:, W:].reshape(nc, LC, W)
    p1r, p1i = p1r_ref[...], p1i_ref[...]
    vr = jnp.sum(bur * p1r - bui * p1i, axis=1)
    vi = jnp.sum(bur * p1i + bui * p1r, axis=1)

    alr, ali = alr_ref[...], ali_ref[...]
    xr, xi = xr_ref[...], xi_ref[...]
    for c in range(nc):
        xpr_ref[c:c + 1, :] = xr
        xpi_ref[c:c + 1, :] = xi
        nr = alr * xr - ali * xi + vr[c:c + 1, :]
        ni = alr * xi + ali * xr + vi[c:c + 1, :]
        xr, xi = nr, ni
    xr_ref[...] = xr
    xi_ref[...] = xi

    xpr = xpr_ref[...][:, None, :]
    xpi = xpi_ref[...][:, None, :]
    p2r, p2i = p2r_ref[...], p2i_ref[...]
    zr = (p2r * xpr - p2i * xpi).reshape(tt, W).astype(BF16)
    zi = (p2r * xpi + p2i * xpr).reshape(tt, W).astype(BF16)
    y = _dot(zr, cr_ref[...]) + _dot(zi, ci_ref[...])

    tmod = lax.broadcasted_iota(jnp.int32, u.shape, 0) % LC
    for tau in range(LC):
        if tau == 0:
            us = ub
        else:
            us = jnp.where(tmod >= tau, pltpu.roll(u, tau, 0), 0.0).astype(BF16)
        y = y + _dot(us, kt_ref[tau])
    y = y + d_ref[...] * u
    g = 0.5 * y * (1.0 + jnp.tanh(0.7978845608028654 * (y + 0.044715 * (y * y * y))))
    o_ref[...] = g.astype(o_ref.dtype)


def _ssm_params(a_re, a_im, log_dt, b_re, b_im, c_re, c_im):
    G, P, N, LC = SSM_GROUPS, SSM_P, SSM_N, SSM_CHUNK
    hp = lax.Precision.HIGHEST
    a_re = jnp.minimum(a_re.astype(F32), -1e-4)
    a_im = a_im.astype(F32)
    dt = jnp.exp(log_dt.astype(F32))[:, None]
    mag = jnp.exp(dt * a_re)
    ab_re, ab_im = mag * jnp.cos(dt * a_im), mag * jnp.sin(dt * a_im)
    den = a_re * a_re + a_im * a_im
    nr = ab_re - 1.0
    z_re = (nr * a_re + ab_im * a_im) / den
    z_im = (ab_im * a_re - nr * a_im) / den
    b_re, b_im = b_re.astype(F32), b_im.astype(F32)
    bb_re = z_re[..., None] * b_re - z_im[..., None] * b_im
    bb_im = z_re[..., None] * b_im + z_im[..., None] * b_re
    kk = jnp.arange(LC + 1, dtype=F32)[:, None, None]
    pm = jnp.exp(kk * dt * a_re)
    pw_re, pw_im = pm * jnp.cos(kk * dt * a_im), pm * jnp.sin(kk * dt * a_im)
    eye = jnp.eye(G, dtype=F32)
    bm_re = jnp.einsum('gnp,gh->gphn', bb_re, eye).reshape(G * P, G * N)
    bm_im = jnp.einsum('gnp,gh->gphn', bb_im, eye).reshape(G * P, G * N)
    bmat = jnp.concatenate([bm_re, bm_im], axis=1)
    c_re, c_im = c_re.astype(F32), c_im.astype(F32)
    cr = jnp.einsum('gpn,gh->gnhp', c_re, eye).reshape(G * N, G * P)
    ci = -jnp.einsum('gpn,gh->gnhp', c_im, eye).reshape(G * N, G * P)
    t_re = pw_re[:LC, :, :, None] * bb_re[None] - pw_im[:LC, :, :, None] * bb_im[None]
    t_im = pw_re[:LC, :, :, None] * bb_im[None] + pw_im[:LC, :, :, None] * bb_re[None]
    kt = (jnp.einsum('gqn,tgnp->tgpq', c_re, t_re, precision=hp)
          - jnp.einsum('gqn,tgnp->tgpq', c_im, t_im, precision=hp))
    ktm = jnp.einsum('tgpq,gh->tgphq', kt, eye).reshape(LC, G * P, G * P)
    flat = lambda x: x.reshape(x.shape[0], G * N)
    p1r, p1i = flat(pw_re[LC - 1::-1][:LC]), flat(pw_im[LC - 1::-1][:LC])
    p2r, p2i = flat(pw_re[1:LC + 1]), flat(pw_im[1:LC + 1])
    alr, ali = flat(pw_re[LC:LC + 1]), flat(pw_im[LC:LC + 1])
    return (bmat.astype(BF16), cr.astype(BF16), ci.astype(BF16), ktm.astype(BF16),
            p1r, p1i, p2r, p2i, alr, ali)


def _ssm(z, params, d_skip, batch, seq):
    N = z.shape[0]
    tt = 256
    nb = seq // tt
    CW = SSM_GROUPS * SSM_P
    W = SSM_GROUPS * SSM_N
    LC = SSM_CHUNK
    bmat, cr, ci, ktm, p1r, p1i, p2r, p2i, alr, ali = params
    full = lambda a: pl.BlockSpec(a.shape, lambda b, j, _n=a.ndim: (0,) * _n)
    return pl.pallas_call(
        functools.partial(_ssm_kernel, tt=tt),
        grid=(batch, nb),
        in_specs=[pl.BlockSpec((tt, CW), lambda b, j: (b * nb + j, OFF_SU // CW)),
                  full(bmat), full(cr), full(ci), full(ktm), full(p1r), full(p1i),
                  full(p2r), full(p2i), full(alr), full(ali),
                  pl.BlockSpec((1, CW), lambda b, j: (0, 0))],
        out_specs=pl.BlockSpec((tt, CW), lambda b, j: (b * nb + j, 0)),
        out_shape=jax.ShapeDtypeStruct((N, CW), BF16),
        scratch_shapes=[pltpu.VMEM((1, W), F32), pltpu.VMEM((1, W), F32),
                        pltpu.VMEM((tt // LC, W), F32), pltpu.VMEM((tt // LC, W), F32)],
        compiler_params=_cparams(("parallel", "arbitrary")),
        name="s5_ssm",
    )(z, bmat, cr, ci, ktm, p1r, p1i, p2r, p2i, alr, ali, d_skip)


def _t5_bucket(dist):
    n = jnp.maximum(dist, 0)
    max_exact = REL_BUCKETS // 2
    nf = jnp.maximum(n, 1).astype(F32)
    large = max_exact + (jnp.log(nf / max_exact) / math.log(REL_MAX_DIST / max_exact)
                         * (REL_BUCKETS - max_exact)).astype(jnp.int32)
    large = jnp.minimum(large, REL_BUCKETS - 1)
    return jnp.where(n < max_exact, n, large)


def _bias_kernel(tab_ref, bk_ref, o_ref):
    h = pl.program_id(0)
    bk = bk_ref[...]
    base = tab_ref[REL_BUCKETS - 1, h]
    acc = jnp.full(bk.shape, NEG, F32)
    for k in range(REL_BUCKETS):
        acc = jnp.where(bk == k, (tab_ref[k, h] - base) * LOG2E, acc)
    o_ref[0] = acc


def _bias_tables(rel_table, seq):
    i = jnp.arange(QT)[:, None]
    j = jnp.arange(QT)[None, :]
    d0 = i - j
    b0 = jnp.where(d0 >= 0, _t5_bucket(d0), REL_BUCKETS)
    b1 = _t5_bucket(QT + i - j)
    t = jnp.arange(seq)[:, None]
    n = jnp.arange(QT)[None, :]
    n_cmp = (seq - CMP_LEN) // CMP_STRIDE + 1
    dc = t - (n * CMP_STRIDE + CMP_LEN - 1)
    bc = jnp.where((dc >= 0) & (n < n_cmp), _t5_bucket(dc), REL_BUCKETS)
    bk = jnp.concatenate([b0, b1, bc], axis=0).astype(jnp.int32)
    R = bk.shape[0]
    nh = rel_table.shape[1]
    return pl.pallas_call(
        _bias_kernel,
        grid=(nh,),
        in_specs=[pl.BlockSpec(memory_space=pltpu.SMEM),
                  pl.BlockSpec((R, QT), lambda h: (0, 0))],
        out_specs=pl.BlockSpec((1, R, QT), lambda h: (h, 0, 0)),
        out_shape=jax.ShapeDtypeStruct((nh, R, QT), F32),
        compiler_params=_cparams(("arbitrary",)),
        name="nsa_bias_tables",
    )(rel_table.astype(F32), bk)


def _head_rms(x, gain):
    return x * lax.rsqrt(jnp.mean(x * x, axis=-1, keepdims=True) + EPS) * gain


def _nsa_prep_kernel(kc_ref, vc_ref, ks_ref, vs_ref, kw_ref, vw_ref, kg_ref, pek_ref, pev_ref,
                     phik_ref, phiv_ref, oks_ref, ovs_ref, okw_ref, ovw_ref, okc_ref, ovc_ref,
                     xpad_ref, *, seq):
    DH = NSA_DH
    kg = kg_ref[...]
    srow = lax.broadcasted_iota(jnp.int32, (seq, DH), 0)
    lane = lax.broadcasted_iota(jnp.int32, (seq, DH), 1)
    onehot = jnp.where(lane == srow // SEL_LEN, 1.0, 0.0).astype(BF16)
    ones = jnp.ones((seq, DH), BF16)
    ones_c = jnp.ones((QT, DH), BF16)
    xpad_ref[seq:seq + CMP_LEN, :] = jnp.zeros((CMP_LEN, DH), F32)
    for g in range(NSA_G):
        ls = slice(g * DH, (g + 1) * DH)
        ksn = _head_rms(ks_ref[:, ls], kg).astype(BF16)
        oks_ref[0, g] = jnp.concatenate([ksn, onehot], axis=1)
        okw_ref[0, g, 0:WIN, :] = jnp.zeros((WIN, DH), BF16)
        okw_ref[0, g, WIN:WIN + seq, :] = _head_rms(kw_ref[:, ls], kg).astype(BF16)
        ovs_ref[0, g] = jnp.concatenate([vs_ref[:, ls].astype(BF16), ones], axis=1)
        ovw_ref[0, g, 0:WIN, :] = jnp.zeros((WIN, 2 * DH), BF16)
        ovw_ref[0, g, WIN:WIN + seq, :] = jnp.concatenate([vw_ref[:, ls].astype(BF16), ones], axis=1)
        for src_ref, pe_ref, phi_ref, is_k in ((kc_ref, pek_ref, phik_ref, True),
                                               (vc_ref, pev_ref, phiv_ref, False)):
            xpad_ref[0:seq, :] = src_ref[:, ls]
            acc = jnp.zeros((QT, DH), F32)
            for l in range(CMP_LEN):
                xl = xpad_ref[pl.ds(l, QT, stride=CMP_STRIDE), :] + pe_ref[l:l + 1, :]
                acc = acc + _dot(xl.astype(BF16), phi_ref[l])
            if is_k:
                okc_ref[0, g] = _head_rms(acc, kg).astype(BF16)
            else:
                ovc_ref[0, g] = jnp.concatenate([acc.astype(BF16), ones_c], axis=1)


def _nsa_prep(z, k_gain, pe_k, pe_v, phi_k, phi_v, batch, seq):
    G, DH = NSA_G, NSA_DH
    KW = G * DH

    def zspec(off):
        return pl.BlockSpec((seq, KW), lambda b: (b, off // KW))

    full = lambda a: pl.BlockSpec(a.shape, lambda b, _n=a.ndim: (0,) * _n)

    def ospec(rows, w):
        return pl.BlockSpec((1, G, rows, w), lambda b: (b, 0, 0, 0))

    def oshape(rows, w):
        return jax.ShapeDtypeStruct((batch, G, rows, w), BF16)

    return pl.pallas_call(
        functools.partial(_nsa_prep_kernel, seq=seq),
        grid=(batch,),
        in_specs=[zspec(OFF_KC), zspec(OFF_VC), zspec(OFF_KS), zspec(OFF_VS), zspec(OFF_KW),
                  zspec(OFF_VW), full(k_gain), full(pe_k), full(pe_v), full(phi_k), full(phi_v)],
        out_specs=[ospec(seq, 2 * DH), ospec(seq, 2 * DH), ospec(seq + WIN, DH),
                   ospec(seq + WIN, 2 * DH), ospec(QT, DH), ospec(QT, 2 * DH)],
        out_shape=[oshape(seq, 2 * DH), oshape(seq, 2 * DH), oshape(seq + WIN, DH),
                   oshape(seq + WIN, 2 * DH), oshape(QT, DH), oshape(QT, 2 * DH)],
        scratch_shapes=[pltpu.VMEM((seq + CMP_LEN, DH), F32)],
        compiler_params=_cparams(("parallel",)),
        name="nsa_kv_prep",
    )(z, z, z, z, z, z, k_gain, pe_k, pe_v, phi_k, phi_v)


def _nsa_kernel(q_ref, gt_ref, ks_ref, vs_ref, kw_ref, vw_ref, kc_ref, vc_ref, bc_ref, nb_ref,
                qg_ref, ov_ref, o_ref, slog_ref, dlog_ref, mrun_ref, mb_ref, acc_ref):
    DH, HPG, TQ = NSA_DH, NSA_HPG, NSA_TQ
    NR = TQ // QT
    RB = HPG * QT
    R = NR * RB
    NSEL = 32
    qi = pl.program_id(1)
    t0 = qi * TQ

    def rows(r):
        return slice(r * RB, (r + 1) * RB)

    def to_rows(per_head):
        return jnp.concatenate([per_head[h][r * QT:(r + 1) * QT] for r in range(NR) for h in range(HPG)],
                               axis=0)

    def lanes(x, c):
        return x[:, c * QT:(c + 1) * QT]

    qblk = q_ref[...]
    qg = qg_ref[...] * (DH ** -0.5 * LOG2E)
    q4b = to_rows([_head_rms(qblk[:, h * DH:(h + 1) * DH], qg) for h in range(HPG)]).astype(BF16)
    nb0 = nb_ref[:, 0].reshape(RB, QT)
    nb1 = nb_ref[:, 1].reshape(RB, QT)

    sc = _dot_nt(q4b, kc_ref[0, 0]) + to_rows([bc_ref[h] for h in range(HPG)])
    mc = jnp.max(sc, axis=-1, keepdims=True)
    ec = jnp.where(sc > 0.5 * NEG, jnp.exp2(sc - mc), 0.0)
    lc = jnp.sum(ec, axis=-1, keepdims=True)
    pc = ec / jnp.where(lc > 0.0, lc, 1.0)
    o_cmp = _dot(pc.astype(BF16), vc_ref[0, 0])[:, :DH]

    psum = jnp.concatenate(
        [sum(pc[r * RB + h * QT:r * RB + (h + 1) * QT] for h in range(HPG)) for r in range(NR)], axis=0)
    ovm = ov_ref[...]
    hi, mid, lo = _split3(psum)
    imp_t = (_dot_nt(ovm, hi) + _dot_nt(ovm, mid) + _dot_nt(ovm, lo))[0:NSEL, :]
    jrow = lax.broadcasted_iota(jnp.int32, (NSEL, TQ), 0)
    blk = (t0 + lax.broadcasted_iota(jnp.int32, (NSEL, TQ), 1)) // SEL_LEN
    forced = (jrow == 0) | (jrow == blk) | (jrow == blk - 1)
    imp_t = jnp.where(forced, FORCE_SCORE, imp_t)
    imp_t = jnp.where(jrow <= blk, imp_t, NEG)
    cnt = jnp.zeros((NSEL, TQ), F32)
    for jp in range(NSEL):
        rowv = imp_t[jp:jp + 1, :]
        beats = (rowv > imp_t) | ((rowv == imp_t) & (jrow > jp))
        cnt = cnt + jnp.where(beats, 1.0, 0.0)
    selb_t = jnp.where(cnt < float(SEL_TOPN), 0.0, NEG)
    selb = jnp.concatenate([selb_t, jnp.zeros((QT - NSEL, TQ), F32)], axis=0).T
    selb4 = to_rows([selb[:, :DH]] * HPG).astype(BF16)
    qaug = jnp.concatenate([q4b, selb4], axis=1)

    mrun_ref[...] = jnp.full((R, QT), NEG, F32)

    def tile(ref, start, size):
        return ref[0, 0, pl.ds(pl.multiple_of(start, QT), size), :]

    def track_max(s):
        m = mrun_ref[...]
        for c in range(TQ // QT):
            m = jnp.maximum(m, lanes(s, c))
        mrun_ref[...] = m

    def sel_far(kt, carry):
        s = _dot_nt(qaug, tile(ks_ref, kt * TQ, TQ))
        slog_ref[kt] = s
        track_max(s)
        return carry

    lax.fori_loop(0, jnp.maximum(qi - 1, 0), sel_far, 0)

    @pl.when(qi >= 1)
    def _():
        s = _dot_nt(qaug, tile(ks_ref, (qi - 1) * TQ, TQ))
        top = s[0:RB]
        top = jnp.concatenate([top[:, :TQ - QT], top[:, TQ - QT:] + nb1], axis=1)
        s = jnp.concatenate([top, s[RB:]], axis=0)
        slog_ref[qi - 1] = s
        track_max(s)

    for r in range(NR):
        s = _dot_nt(qaug[rows(r)], tile(ks_ref, t0, (r + 1) * QT))
        parts = [lanes(s, c) for c in range(r + 1)]
        parts[r] = parts[r] + nb0
        if r >= 1:
            parts[r - 1] = parts[r - 1] + nb1
        m = mrun_ref[rows(r), :]
        for c in range(r + 1):
            dlog_ref[rows(r), c * QT:(c + 1) * QT] = parts[c]
            m = jnp.maximum(m, parts[c])
        mrun_ref[rows(r), :] = m
    mb_ref[...] = jnp.broadcast_to(jnp.max(mrun_ref[...], axis=-1, keepdims=True), (R, QT))

    acc_ref[...] = jnp.zeros_like(acc_ref)

    def sel_pv(kt, carry):
        mb = mb_ref[...]
        s = slog_ref[kt]
        p = jnp.concatenate([jnp.exp2(lanes(s, c) - mb) for c in range(TQ // QT)], axis=1)
        acc_ref[...] += _dot(p.astype(BF16), tile(vs_ref, kt * TQ, TQ))
        return carry

    lax.fori_loop(0, qi, sel_pv, 0)
    for r in range(NR):
        mb = mb_ref[rows(r), :]
        p = jnp.concatenate([jnp.exp2(dlog_ref[rows(r), c * QT:(c + 1) * QT] - mb)
                             for c in range(r + 1)], axis=1)
        acc_ref[rows(r), :] += _dot(p.astype(BF16), tile(vs_ref, t0, (r + 1) * QT))
    acc_s = acc_ref[...]

    ii = lax.broadcasted_iota(jnp.int32, (RB, QT), 0) % QT
    jj = lax.broadcasted_iota(jnp.int32, (RB, QT), 1)
    nwt = WIN // QT
    acc_w = []
    for r in range(NR):
        start = t0 + r * QT
        s = _dot_nt(q4b[rows(r)], tile(kw_ref, start, WIN + QT))
        parts = [lanes(s, c) for c in range(nwt + 1)]
        parts[nwt] = parts[nwt] + nb0
        parts[nwt - 1] = parts[nwt - 1] + nb1
        parts[0] = jnp.where(jj > ii, parts[0], NEG)
        for c in range(nwt - r):
            parts[c] = jnp.where(qi > 0, parts[c], NEG)
        m = parts[0]
        for c in range(1, nwt + 1):
            m = jnp.maximum(m, parts[c])
        m = jnp.max(m, axis=-1, keepdims=True)
        p = jnp.concatenate([jnp.exp2(parts[c] - m) for c in range(nwt + 1)], axis=1)
        acc_w.append(_dot(p.astype(BF16), tile(vw_ref, start, WIN + QT)))
    acc_w = jnp.concatenate(acc_w, axis=0)

    gt = _sigmoid(gt_ref[...])

    def gate_col(br):
        return to_rows([gt[:, br * HPG + h:br * HPG + h + 1] for h in range(HPG)])

    def inv_l(acc):
        l = acc[:, DH:DH + 1]
        return 1.0 / jnp.where(l > 0.0, l, 1.0)

    o = (gate_col(0) * o_cmp + (gate_col(1) * inv_l(acc_s)) * acc_s[:, :DH]
         + (gate_col(2) * inv_l(acc_w)) * acc_w[:, :DH])
    o_ref[...] = jnp.concatenate(
        [jnp.concatenate([o[r * RB + h * QT:r * RB + (h + 1) * QT] for r in range(NR)], axis=0)
         for h in range(HPG)], axis=1).astype(o_ref.dtype)


def _nsa(z, zg, prep, bias, q_gain, ovm, batch, seq):
    N = z.shape[0]
    G, HPG, DH, TQ = NSA_G, NSA_HPG, NSA_DH, NSA_TQ
    nq = seq // TQ
    R = HPG * TQ
    oks, ovs, okw, ovw, okc, ovc = prep
    QW = HPG * DH
    nbias = bias[:, :2 * QT].reshape(-1, 2, QT, QT)
    cbias = bias[:, 2 * QT:]

    def kvspec(a):
        return pl.BlockSpec((1, 1) + a.shape[2:], lambda p, i: (p // G, p % G, 0, 0))

    return pl.pallas_call(
        _nsa_kernel,
        grid=(batch * G, nq),
        in_specs=[pl.BlockSpec((TQ, QW), lambda p, i: ((p // G) * nq + i, OFF_NQ // QW + p % G)),
                  pl.BlockSpec((TQ, 128), lambda p, i: ((p // G) * nq + i, p % G)),
                  kvspec(oks), kvspec(ovs), kvspec(okw), kvspec(ovw), kvspec(okc), kvspec(ovc),
                  pl.BlockSpec((HPG, TQ, QT), lambda p, i: (p % G, i, 0)),
                  pl.BlockSpec((HPG, 2, QT, QT), lambda p, i: (p % G, 0, 0, 0)),
                  pl.BlockSpec((1, DH), lambda p, i: (0, 0)),
                  pl.BlockSpec((QT, QT), lambda p, i: (0, 0))],
        out_specs=pl.BlockSpec((TQ, QW), lambda p, i: ((p // G) * nq + i, p % G)),
        out_shape=jax.ShapeDtypeStruct((N, G * QW), BF16),
        scratch_shapes=[pltpu.VMEM((nq - 1, R, TQ), F32), pltpu.VMEM((R, TQ), F32),
                        pltpu.VMEM((R, QT), F32), pltpu.VMEM((R, QT), F32),
                        pltpu.VMEM((R, 2 * DH), F32)],
        compiler_params=_cparams(("parallel", "arbitrary")),
        name="nsa_attention",
    )(z, zg, oks, ovs, okw, ovw, okc, ovc, cbias, nbias, q_gain, ovm)


def _merge_kernel(h_ref, mod_ref, za_ref, zb_ref, zc_ref, oa_ref, ys_ref, oc_ref,
                  wa_ref, wb_ref, wc_ref, wo_ref, o_ref):
    D = h_ref.shape[1]
    ya = _dot(oa_ref[...], wa_ref[...])
    zz = _dot(ys_ref[...], wb_ref[...])
    yb = zz[:, :D] * _sigmoid(zz[:, D:])
    yc = _dot(oc_ref[...], wc_ref[...])
    merged = _sigmoid(za_ref[...]) * ya + _sigmoid(zb_ref[...]) * yb + _sigmoid(zc_ref[...]) * yc
    o_ref[...] = h_ref[...] + mod_ref[0, 5:6, :] * _dot(merged.astype(BF16), wo_ref[...])


def _merge(h, mod, z, oa, ys, oc, wa, wb, wc, wo, seq):
    N, D = h.shape
    tm = 256
    tpb = seq // tm
    row = lambda w: pl.BlockSpec((tm, w), lambda i: (i, 0))
    res = lambda a: pl.BlockSpec(a.shape, lambda i: (0, 0), pipeline_mode=pl.Buffered(1))
    return pl.pallas_call(
        _merge_kernel,
        grid=(N // tm,),
        in_specs=[row(D),
                  pl.BlockSpec((1, 9, D), lambda i: (i // tpb, 0, 0)),
                  pl.BlockSpec((tm, D), lambda i: (i, OFF_ZA // D)),
                  pl.BlockSpec((tm, D), lambda i: (i, OFF_ZB // D)),
                  pl.BlockSpec((tm, D), lambda i: (i, OFF_ZC // D)),
                  row(oa.shape[1]), row(ys.shape[1]), row(oc.shape[1]),
                  res(wa), res(wb), res(wc), res(wo)],
        out_specs=row(D),
        out_shape=jax.ShapeDtypeStruct((N, D), F32),
        compiler_params=_cparams(("parallel",)),
        name="mix_merge",
    )(h, mod, z, z, z, oa, ys, oc, wa, wb, wc, wo)


def _permute_w_in(w):
    gate0 = 4 * 512 + 512 + 1024 + 6 * 256
    ngate = 3 * NSA_G * NSA_HPG
    main = jnp.concatenate([w[:, gate0 + ngate:], w[:, :gate0]], axis=1)
    gw = w[:, gate0:gate0 + ngate].reshape(w.shape[0], 3, NSA_G, NSA_HPG)
    gw = gw.transpose(0, 2, 1, 3).reshape(w.shape[0], NSA_G, 3 * NSA_HPG)
    gw = jnp.pad(gw, ((0, 0), (0, 0), (0, 128 - 3 * NSA_HPG))).reshape(w.shape[0], NSA_G * 128)
    return main.astype(BF16), gw.astype(BF16)


def _overlap_matrix():
    j = jnp.arange(QT)[:, None]
    n = jnp.arange(QT)[None, :]
    st = n * CMP_STRIDE
    ov = (st < j * SEL_LEN + SEL_LEN) & (st + CMP_LEN > j * SEL_LEN)
    return jnp.where(ov, 1.0, 0.0).astype(BF16)


def kernel(x, c, ada_w, ada_b, norm_g, ffn1_wi, ffn1_wo, ffn2_wi, ffn2_wo, w_in, hg_lb_logits,
           hg_onorm, hg_proj, ssm_a_re, ssm_a_im, ssm_log_dt, ssm_b_re, ssm_b_im, ssm_c_re,
           ssm_c_im, ssm_d, ssm_glu_w, nsa_q_gain, nsa_k_gain, nsa_pe_k, nsa_pe_v, nsa_phi_k,
           nsa_phi_v, nsa_proj, rel_table, w_out):
    B, S, D = x.shape
    L = ada_w.shape[0]
    N = B * S
    assert S % 512 == 0 and S // SEL_LEN == 32 and S // QT == 16
    lb_cum = jnp.cumsum(jax.nn.softmax(hg_lb_logits.astype(F32), axis=0), axis=0)
    lower_bounds = lb_cum - lb_cum[0:1]
    mods = _mods(c, ada_w, ada_b).reshape(L, B, 9, D)
    bias = _bias_tables(rel_table, S)
    ovm = _overlap_matrix()
    h = x.reshape(N, D)
    for l in range(L):
        mod = mods[l]
        h = _ffn(h, mod, norm_g[l, 0:1], ffn1_wi[l].astype(BF16), ffn1_wo[l].astype(BF16), 0, S)
        w_main, w_gate = _permute_w_in(w_in[l])
        z, zg = _win(h, mod, norm_g[l, 1:2], w_main, w_gate, S)
        oa = _hgrn(z, lower_bounds[l:l + 1], hg_onorm[l:l + 1], B, S)
        sp = _ssm_params(ssm_a_re[l], ssm_a_im[l], ssm_log_dt[l], ssm_b_re[l], ssm_b_im[l],
                         ssm_c_re[l], ssm_c_im[l])
        ys = _ssm(z, sp, ssm_d[l:l + 1], B, S)
        prep = _nsa_prep(z, nsa_k_gain[l:l + 1], nsa_pe_k[l], nsa_pe_v[l],
                         nsa_phi_k[l].astype(BF16), nsa_phi_v[l].astype(BF16), B, S)
        oc = _nsa(z, zg, prep, bias, nsa_q_gain[l:l + 1], ovm, B, S)
        h = _merge(h, mod, z, oa, ys, oc, hg_proj[l].astype(BF16), ssm_glu_w[l].astype(BF16),
                   nsa_proj[l].astype(BF16), w_out[l].astype(BF16), S)
        h = _ffn(h, mod, norm_g[l, 2:3], ffn2_wi[l].astype(BF16), ffn2_wo[l].astype(BF16), 6, S)
    return h.reshape(B, S, D)
```

```python
import functools
import math

import jax
import jax.numpy as jnp
from jax import lax
from jax.experimental import pallas as pl
from jax.experimental.pallas import tpu as pltpu

F32 = jnp.float32
BF16 = jnp.bfloat16

EPS = 1e-6
NEG = -1e30
LOG2E = 1.4426950408889634
FORCE_SCORE = 1e4

HG_HEADS, HG_D, HG_CHUNK = 4, 128, 16
SSM_GROUPS, SSM_P, SSM_N, SSM_CHUNK = 32, 16, 64, 8
NSA_G, NSA_HPG, NSA_DH = 4, 4, 64
CMP_LEN, CMP_STRIDE, SEL_LEN, SEL_TOPN, WIN = 32, 16, 64, 16, 512
REL_BUCKETS, REL_MAX_DIST = 32, 128
QT = 128
NSA_TQ = 512

VMEM_LIMIT = 60 * 1024 * 1024

OFF_ZA, OFF_ZB, OFF_ZC = 0, 2048, 4096
OFF_HQ, OFF_HF, OFF_HI, OFF_HG = 6144, 6656, 7168, 7680
OFF_SU, OFF_NQ = 8192, 8704
OFF_KC, OFF_VC, OFF_KS, OFF_VS, OFF_KW, OFF_VW = 9728, 9984, 10240, 10496, 10752, 11008
Z_WIDTH = 11264


def _cparams(sem):
    return pltpu.CompilerParams(dimension_semantics=sem, vmem_limit_bytes=VMEM_LIMIT)


def _dot(a, b):
    return jnp.dot(a, b, preferred_element_type=F32)


def _dot_nt(a, b):
    return lax.dot_general(a, b, (((1,), (1,)), ((), ())), preferred_element_type=F32)


def _dot_tn(a, b):
    return lax.dot_general(a, b, (((0,), (0,)), ((), ())), preferred_element_type=F32)


def _sigmoid(x):
    return 1.0 / (1.0 + jnp.exp(-x))


def _split3(x):
    hi = x.astype(BF16)
    r = x - hi.astype(F32)
    mid = r.astype(BF16)
    lo = (r - mid.astype(F32)).astype(BF16)
    return hi, mid, lo


def _norm_mod(x, gain, shift, scale):
    ms = jnp.mean(x * x, axis=-1, keepdims=True)
    y = x * lax.rsqrt(ms + EPS) * gain
    return y * (1.0 + scale) + shift


def _mod_kernel(c_ref, w_ref, b_ref, o_ref):
    c = c_ref[...]
    ca = (c * _sigmoid(c)).astype(BF16)
    o_ref[0] = _dot(ca, w_ref[0].astype(BF16)) + b_ref[0]


def _mods(c, ada_w, ada_b):
    L, D, W = ada_w.shape
    B = c.shape[0]
    tn = 1024
    return pl.pallas_call(
        _mod_kernel,
        grid=(L, W // tn),
        in_specs=[pl.BlockSpec((B, D), lambda l, j: (0, 0)),
                  pl.BlockSpec((1, D, tn), lambda l, j: (l, 0, j)),
                  pl.BlockSpec((1, 1, tn), lambda l, j: (l, 0, j))],
        out_specs=pl.BlockSpec((1, B, tn), lambda l, j: (l, 0, j)),
        out_shape=jax.ShapeDtypeStruct((L, B, W), F32),
        compiler_params=_cparams(("parallel", "parallel")),
        name="adaln_mod",
    )(c, ada_w, ada_b.reshape(L, 1, W))


def _ffn_kernel(h_ref, mod_ref, g_ref, wi1_ref, wi2_ref, wo_ref, o_ref, u_s, *, k0, nf):
    f = pl.program_id(1)

    @pl.when(f == 0)
    def _():
        u = _norm_mod(h_ref[...], g_ref[...], mod_ref[0, k0:k0 + 1, :], mod_ref[0, k0 + 1:k0 + 2, :])
        u_s[...] = u.astype(BF16)
        o_ref[...] = jnp.zeros_like(o_ref)

    u = u_s[...]
    a1 = _dot(u, wi1_ref[0].astype(BF16))
    a2 = _dot(u, wi2_ref[0].astype(BF16))
    act = (a1 * _sigmoid(a1) * a2).astype(BF16)
    o_ref[...] += _dot(act, wo_ref[0].astype(BF16))

    @pl.when(f == nf - 1)
    def _():
        o_ref[...] = h_ref[...] + (0.5 * mod_ref[0, k0 + 2:k0 + 3, :]) * o_ref[...]


def _ffn(h, mod, gain, wi, wo, layer, k0, seq):
    N, D = h.shape
    dff = wo.shape[1]
    tm, tf = 1024, 256
    nf = dff // tf
    tpb = seq // tm
    return pl.pallas_call(
        functools.partial(_ffn_kernel, k0=k0, nf=nf),
        grid=(N // tm, nf),
        in_specs=[pl.BlockSpec((tm, D), lambda i, f: (i, 0)),
                  pl.BlockSpec((1, 9, D), lambda i, f: (i // tpb, 0, 0)),
                  pl.BlockSpec((1, D), lambda i, f: (0, 0)),
                  pl.BlockSpec((1, D, tf), lambda i, f: (layer, 0, f)),
                  pl.BlockSpec((1, D, tf), lambda i, f: (layer, 0, f + nf)),
                  pl.BlockSpec((1, tf, D), lambda i, f: (layer, f, 0))],
        out_specs=pl.BlockSpec((tm, D), lambda i, f: (i, 0)),
        out_shape=jax.ShapeDtypeStruct((N, D), F32),
        scratch_shapes=[pltpu.VMEM((tm, D), BF16)],
        compiler_params=_cparams(("parallel", "arbitrary")),
        name="ffn",
    )(h, mod, gain, wi, wi, wo)


def _win_kernel(h_ref, mod_ref, g_ref, w_ref, wg_ref, z_ref, zg_ref, u_s):
    j = pl.program_id(1)

    @pl.when(j == 0)
    def _():
        u = _norm_mod(h_ref[...], g_ref[...], mod_ref[0, 3:4, :], mod_ref[0, 4:5, :])
        ub = u.astype(BF16)
        u_s[...] = ub
        zg_ref[...] = _dot(ub, wg_ref[...])

    z_ref[...] = _dot(u_s[...], w_ref[...])


def _win(h, mod, gain, w_main, w_gate, seq):
    N, D = h.shape
    tm, tn = 1024, 1024
    tpb = seq // tm
    GW = w_gate.shape[1]
    return pl.pallas_call(
        _win_kernel,
        grid=(N // tm, Z_WIDTH // tn),
        in_specs=[pl.BlockSpec((tm, D), lambda i, j: (i, 0)),
                  pl.BlockSpec((1, 9, D), lambda i, j: (i // tpb, 0, 0)),
                  pl.BlockSpec((1, D), lambda i, j: (0, 0)),
                  pl.BlockSpec((D, tn), lambda i, j: (0, j)),
                  pl.BlockSpec((D, GW), lambda i, j: (0, 0))],
        out_specs=[pl.BlockSpec((tm, tn), lambda i, j: (i, j)),
                   pl.BlockSpec((tm, GW), lambda i, j: (i, 0))],
        out_shape=[jax.ShapeDtypeStruct((N, Z_WIDTH), F32),
                   jax.ShapeDtypeStruct((N, GW), F32)],
        scratch_shapes=[pltpu.VMEM((tm, D), BF16)],
        compiler_params=_cparams(("parallel", "arbitrary")),
        name="in_proj",
    )(h, mod, gain, w_main, w_gate)


def _hgrn_kernel(q_ref, f_ref, i_ref, g_ref, lb_ref, on_ref, o_ref, st_ref, *, tb):
    C = HG_CHUNK
    nc = tb // C

    @pl.when(pl.program_id(1) == 0)
    def _():
        st_ref[...] = jnp.zeros_like(st_ref)

    q = q_ref[...]
    qs = q * _sigmoid(q)
    x = f_ref[...]
    iv = i_ref[...]
    lb = lb_ref[...]
    sp = jnp.log1p(jnp.exp(-jnp.abs(x)))
    lsig = jnp.minimum(x, 0.0) - sp
    a = jnp.log(jnp.maximum(lb, 1e-38))
    bterm = jnp.log1p(-lb) + lsig
    lae = jnp.maximum(a, bterm) + jnp.log1p(jnp.exp(-jnp.abs(a - bterm)))
    log_f = jnp.where(lb > 0.0, lae, bterm)
    k = (1.0 - lb) * jnp.exp(jnp.minimum(-x, 0.0) - sp)

    r = lax.broadcasted_iota(jnp.int32, (tb, tb), 0)
    cidx = lax.broadcasted_iota(jnp.int32, (tb, tb), 1)
    lmat = jnp.where(((r // C) == (cidx // C)) & (cidx <= r), 1.0, 0.0).astype(BF16)
    hi, mid, lo = _split3(log_f)
    b = _dot(lmat, hi) + _dot(lmat, mid) + _dot(lmat, lo)

    tmod = lax.broadcasted_iota(jnp.int32, (tb, HG_D), 0) % C
    ones = jnp.ones((HG_D, HG_D), BF16)
    o = jnp.zeros((tb, HG_D), F32)
    for d in range(C):
        if d == 0:
            p = qs * k
            isd = iv
        else:
            kd = pltpu.roll(k, d, 0)
            bd = pltpu.roll(b, d, 0)
            isd = pltpu.roll(iv, d, 0)
            p = jnp.where(tmod >= d, qs * kd * jnp.exp(b - bd), 0.0)
        phi = p.astype(BF16)
        plo = (p - phi.astype(F32)).astype(BF16)
        rsum = _dot(phi, ones) + _dot(plo, ones)
        o = o + rsum * isd

    eb = jnp.exp(b)
    qe = (qs * eb).astype(BF16)
    b3 = b.reshape(nc, C, HG_D)
    bend = b3[:, C - 1:C, :]
    kdec = (k.reshape(nc, C, HG_D) * jnp.exp(bend - b3)).reshape(tb, HG_D).astype(BF16)
    ebend = jnp.exp(bend)
    ib = iv.astype(BF16)
    st = st_ref[...]
    outs = []
    for c in range(nc):
        sl = slice(c * C, (c + 1) * C)
        outs.append(_dot_nt(qe[sl], st.astype(BF16)))
        ut = _dot_tn(ib[sl], kdec[sl])
        st = st * ebend[c] + ut
    st_ref[...] = st
    o = o + jnp.concatenate(outs, axis=0)

    on = o * lax.rsqrt(jnp.mean(o * o, axis=-1, keepdims=True) + EPS) * on_ref[...]
    g = g_ref[...]
    o_ref[...] = (on * (g * _sigmoid(g))).astype(o_ref.dtype)


def _hgrn(z, lb, onorm, batch, seq):
    N = z.shape[0]
    tb = 256
    nb = seq // tb
    H = HG_HEADS

    def zspec(off):
        return pl.BlockSpec((tb, HG_D), lambda p, j: ((p // H) * nb + j, off // HG_D + p % H))

    return pl.pallas_call(
        functools.partial(_hgrn_kernel, tb=tb),
        grid=(batch * H, nb),
        in_specs=[zspec(OFF_HQ), zspec(OFF_HF), zspec(OFF_HI), zspec(OFF_HG),
                  pl.BlockSpec((1, HG_D), lambda p, j: (0, p % H)),
                  pl.BlockSpec((1, HG_D), lambda p, j: (0, 0))],
        out_specs=pl.BlockSpec((tb, HG_D), lambda p, j: ((p // H) * nb + j, p % H)),
        out_shape=jax.ShapeDtypeStruct((N, H * HG_D), BF16),
        scratch_shapes=[pltpu.VMEM((HG_D, HG_D), F32)],
        compiler_params=_cparams(("parallel", "arbitrary")),
        name="hgrn2",
    )(z, z, z, z, lb, onorm)


def _ssm_kernel(u_ref, bm_ref, cr_ref, ci_ref, kt_ref, p1r_ref, p1i_ref, p2r_ref, p2i_ref,
                alr_ref, ali_ref, d_ref, o_ref, xr_ref, xi_ref, xpr_ref, xpi_ref, *, tt):
    LC = SSM_CHUNK
    nc = tt // LC
    W = SSM_GROUPS * SSM_N

    @pl.when(pl.program_id(1) == 0)
    def _():
        xr_ref[...] = jnp.zeros_like(xr_ref)
        xi_ref[...] = jnp.zeros_like(xi_ref)

    u = u_ref[...]
    ub = u.astype(BF16)
    bu = _dot(ub, bm_ref[...])
    bur = bu[:, :W].reshape(nc, LC, W)
    bui = bu[:, W:].reshape(nc, LC, W)
    p1r, p1i = p1r_ref[...], p1i_ref[...]
    vr = jnp.sum(bur * p1r - bui * p1i, axis=1)
    vi = jnp.sum(bur * p1i + bui * p1r, axis=1)

    alr, ali = alr_ref[...], ali_ref[...]
    xr, xi = xr_ref[...], xi_ref[...]
    for c in range(nc):
        xpr_ref[c:c + 1, :] = xr
        xpi_ref[c:c + 1, :] = xi
        nr = alr * xr - ali * xi + vr[c:c + 1, :]
        ni = alr * xi + ali * xr + vi[c:c + 1, :]
        xr, xi = nr, ni
    xr_ref[...] = xr
    xi_ref[...] = xi

    xpr = xpr_ref[...][:, None, :]
    xpi = xpi_ref[...][:, None, :]
    p2r, p2i = p2r_ref[...], p2i_ref[...]
    zr = (p2r * xpr - p2i * xpi).reshape(tt, W).astype(BF16)
    zi = (p2r * xpi + p2i * xpr).reshape(tt, W).astype(BF16)
    y = _dot(zr, cr_ref[...]) + _dot(zi, ci_ref[...])

    tmod = lax.broadcasted_iota(jnp.int32, u.shape, 0) % LC
    for tau in range(LC):
        if tau == 0:
            us = ub
        else:
            us = jnp.where(tmod >= tau, pltpu.roll(u, tau, 0), 0.0).astype(BF16)
        y = y + _dot(us, kt_ref[tau])
    y = y + d_ref[...] * u
    g = 0.5 * y * (1.0 + jnp.tanh(0.7978845608028654 * (y + 0.044715 * (y * y * y))))
    o_ref[...] = g.astype(o_ref.dtype)


def _ssm_params(a_re, a_im, log_dt, b_re, b_im, c_re, c_im):
    G, P, N, LC = SSM_GROUPS, SSM_P, SSM_N, SSM_CHUNK
    hp = lax.Precision.HIGHEST
    a_re = jnp.minimum(a_re.astype(F32), -1e-4)
    a_im = a_im.astype(F32)
    dt = jnp.exp(log_dt.astype(F32))[:, None]
    mag = jnp.exp(dt * a_re)
    ab_re, ab_im = mag * jnp.cos(dt * a_im), mag * jnp.sin(dt * a_im)
    den = a_re * a_re + a_im * a_im
    nr = ab_re - 1.0
    z_re = (nr * a_re + ab_im * a_im) / den
    z_im = (ab_im * a_re - nr * a_im) / den
    b_re, b_im = b_re.astype(F32), b_im.astype(F32)
    bb_re = z_re[..., None] * b_re - z_im[..., None] * b_im
    bb_im = z_re[..., None] * b_im + z_im[..., None] * b_re
    kk = jnp.arange(LC + 1, dtype=F32)[:, None, None]
    pm = jnp.exp(kk * dt * a_re)
    pw_re, pw_im = pm * jnp.cos(kk * dt * a_im), pm * jnp.sin(kk * dt * a_im)
    eye = jnp.eye(G, dtype=F32)
    bm_re = jnp.einsum('gnp,gh->gphn', bb_re, eye).reshape(G * P, G * N)
    bm_im = jnp.einsum('gnp,gh->gphn', bb_im, eye).reshape(G * P, G * N)
    bmat = jnp.concatenate([bm_re, bm_im], axis=1)
    c_re, c_im = c_re.astype(F32), c_im.astype(F32)
    cr = jnp.einsum('gpn,gh->gnhp', c_re, eye).reshape(G * N, G * P)
    ci = -jnp.einsum('gpn,gh->gnhp', c_im, eye).reshape(G * N, G * P)
    t_re = pw_re[:LC, :, :, None] * bb_re[None] - pw_im[:LC, :, :, None] * bb_im[None]
    t_im = pw_re[:LC, :, :, None] * bb_im[None] + pw_im[:LC, :, :, None] * bb_re[None]
    kt = (jnp.einsum('gqn,tgnp->tgpq', c_re, t_re, precision=hp)
          - jnp.einsum('gqn,tgnp->tgpq', c_im, t_im, precision=hp))
    ktm = jnp.einsum('tgpq,gh->tgphq', kt, eye).reshape(LC, G * P, G * P)
    flat = lambda x: x.reshape(x.shape[0], G * N)
    p1r, p1i = flat(pw_re[LC - 1::-1][:LC]), flat(pw_im[LC - 1::-1][:LC])
    p2r, p2i = flat(pw_re[1:LC + 1]), flat(pw_im[1:LC + 1])
    alr, ali = flat(pw_re[LC:LC + 1]), flat(pw_im[LC:LC + 1])
    return (bmat.astype(BF16), cr.astype(BF16), ci.astype(BF16), ktm.astype(BF16),
            p1r, p1i, p2r, p2i, alr, ali)


def _ssm(z, params, d_skip, batch, seq):
    N = z.shape[0]
    tt = 256
    nb = seq // tt
    CW = SSM_GROUPS * SSM_P
    W = SSM_GROUPS * SSM_N
    LC = SSM_CHUNK
    bmat, cr, ci, ktm, p1r, p1i, p2r, p2i, alr, ali = params
    full = lambda a: pl.BlockSpec(a.shape, lambda b, j, _n=a.ndim: (0,) * _n)
    return pl.pallas_call(
        functools.partial(_ssm_kernel, tt=tt),
        grid=(batch, nb),
        in_specs=[pl.BlockSpec((tt, CW), lambda b, j: (b * nb + j, OFF_SU // CW)),
                  full(bmat), full(cr), full(ci), full(ktm), full(p1r), full(p1i),
                  full(p2r), full(p2i), full(alr), full(ali),
                  pl.BlockSpec((1, CW), lambda b, j: (0, 0))],
        out_specs=pl.BlockSpec((tt, CW), lambda b, j: (b * nb + j, 0)),
        out_shape=jax.ShapeDtypeStruct((N, CW), BF16),
        scratch_shapes=[pltpu.VMEM((1, W), F32), pltpu.VMEM((1, W), F32),
                        pltpu.VMEM((tt // LC, W), F32), pltpu.VMEM((tt // LC, W), F32)],
        compiler_params=_cparams(("parallel", "arbitrary")),
        name="s5_ssm",
    )(z, bmat, cr, ci, ktm, p1r, p1i, p2r, p2i, alr, ali, d_skip)


def _t5_bucket(dist):
    n = jnp.maximum(dist, 0)
    max_exact = REL_BUCKETS // 2
    nf = jnp.maximum(n, 1).astype(F32)
    large = max_exact + (jnp.log(nf / max_exact) / math.log(REL_MAX_DIST / max_exact)
                         * (REL_BUCKETS - max_exact)).astype(jnp.int32)
    large = jnp.minimum(large, REL_BUCKETS - 1)
    return jnp.where(n < max_exact, n, large)


def _bias_kernel(tab_ref, bk_ref, o_ref):
    h = pl.program_id(0)
    bk = bk_ref[...]
    base = tab_ref[REL_BUCKETS - 1, h]
    acc = jnp.full(bk.shape, NEG, F32)
    for k in range(REL_BUCKETS):
        acc = jnp.where(bk == k, (tab_ref[k, h] - base) * LOG2E, acc)
    o_ref[0] = acc


def _bias_tables(rel_table, seq):
    i = jnp.arange(QT)[:, None]
    j = jnp.arange(QT)[None, :]
    d0 = i - j
    b0 = jnp.where(d0 >= 0, _t5_bucket(d0), REL_BUCKETS)
    b1 = _t5_bucket(QT + i - j)
    t = jnp.arange(seq)[:, None]
    n = jnp.arange(QT)[None, :]
    n_cmp = (seq - CMP_LEN) // CMP_STRIDE + 1
    dc = t - (n * CMP_STRIDE + CMP_LEN - 1)
    bc = jnp.where((dc >= 0) & (n < n_cmp), _t5_bucket(dc), REL_BUCKETS)
    bk = jnp.concatenate([b0, b1, bc], axis=0).astype(jnp.int32)
    R = bk.shape[0]
    nh = rel_table.shape[1]
    return pl.pallas_call(
        _bias_kernel,
        grid=(nh,),
        in_specs=[pl.BlockSpec(memory_space=pltpu.SMEM),
                  pl.BlockSpec((R, QT), lambda h: (0, 0))],
        out_specs=pl.BlockSpec((1, R, QT), lambda h: (h, 0, 0)),
        out_shape=jax.ShapeDtypeStruct((nh, R, QT), F32),
        compiler_params=_cparams(("arbitrary",)),
        name="nsa_bias_tables",
    )(rel_table.astype(F32), bk)


def _head_rms(x, gain):
    return x * lax.rsqrt(jnp.mean(x * x, axis=-1, keepdims=True) + EPS) * gain


def _nsa_prep_kernel(kc_ref, vc_ref, ks_ref, vs_ref, kw_ref, vw_ref, kg_ref, pek_ref, pev_ref,
                     phik_ref, phiv_ref, oks_ref, ovs_ref, okw_ref, ovw_ref, okc_ref, ovc_ref,
                     xpad_ref, *, seq):
    DH = NSA_DH
    kg = kg_ref[...]
    srow = lax.broadcasted_iota(jnp.int32, (seq, DH), 0)
    lane = lax.broadcasted_iota(jnp.int32, (seq, DH), 1)
    onehot = jnp.where(lane == srow // SEL_LEN, 1.0, 0.0).astype(BF16)
    ones = jnp.ones((seq, DH), BF16)
    ones_c = jnp.ones((QT, DH), BF16)
    xpad_ref[seq:seq + CMP_LEN, :] = jnp.zeros((CMP_LEN, DH), F32)
    for g in range(NSA_G):
        ls = slice(g * DH, (g + 1) * DH)
        ksn = _head_rms(ks_ref[:, ls], kg).astype(BF16)
        oks_ref[0, g] = jnp.concatenate([ksn, onehot], axis=1)
        okw_ref[0, g, 0:WIN, :] = jnp.zeros((WIN, DH), BF16)
        okw_ref[0, g, WIN:WIN + seq, :] = _head_rms(kw_ref[:, ls], kg).astype(BF16)
        ovs_ref[0, g] = jnp.concatenate([vs_ref[:, ls].astype(BF16), ones], axis=1)
        ovw_ref[0, g, 0:WIN, :] = jnp.zeros((WIN, 2 * DH), BF16)
        ovw_ref[0, g, WIN:WIN + seq, :] = jnp.concatenate([vw_ref[:, ls].astype(BF16), ones], axis=1)
        for src_ref, pe_ref, phi_ref, is_k in ((kc_ref, pek_ref, phik_ref, True),
                                               (vc_ref, pev_ref, phiv_ref, False)):
            xpad_ref[0:seq, :] = src_ref[:, ls]
            acc = jnp.zeros((QT, DH), F32)
            for l in range(CMP_LEN):
                xl = xpad_ref[pl.ds(l, QT, stride=CMP_STRIDE), :] + pe_ref[l:l + 1, :]
                acc = acc + _dot(xl.astype(BF16), phi_ref[l])
            if is_k:
                okc_ref[0, g] = _head_rms(acc, kg).astype(BF16)
            else:
                ovc_ref[0, g] = jnp.concatenate([acc.astype(BF16), ones_c], axis=1)


def _nsa_prep(z, k_gain, pe_k, pe_v, phi_k, phi_v, batch, seq):
    G, DH = NSA_G, NSA_DH
    KW = G * DH

    def zspec(off):
        return pl.BlockSpec((seq, KW), lambda b: (b, off // KW))

    full = lambda a: pl.BlockSpec(a.shape, lambda b, _n=a.ndim: (0,) * _n)

    def ospec(rows, w):
        return pl.BlockSpec((1, G, rows, w), lambda b: (b, 0, 0, 0))

    def oshape(rows, w):
        return jax.ShapeDtypeStruct((batch, G, rows, w), BF16)

    return pl.pallas_call(
        functools.partial(_nsa_prep_kernel, seq=seq),
        grid=(batch,),
        in_specs=[zspec(OFF_KC), zspec(OFF_VC), zspec(OFF_KS), zspec(OFF_VS), zspec(OFF_KW),
                  zspec(OFF_VW), full(k_gain), full(pe_k), full(pe_v), full(phi_k), full(phi_v)],
        out_specs=[ospec(seq, 2 * DH), ospec(seq, 2 * DH), ospec(seq + WIN, DH),
                   ospec(seq + WIN, 2 * DH), ospec(QT, DH), ospec(QT, 2 * DH)],
        out_shape=[oshape(seq, 2 * DH), oshape(seq, 2 * DH), oshape(seq + WIN, DH),
                   oshape(seq + WIN, 2 * DH), oshape(QT, DH), oshape(QT, 2 * DH)],
        scratch_shapes=[pltpu.VMEM((seq + CMP_LEN, DH), F32)],
        compiler_params=_cparams(("parallel",)),
        name="nsa_kv_prep",
    )(z, z, z, z, z, z, k_gain, pe_k, pe_v, phi_k, phi_v)


def _nsa_kernel(q_ref, gt_ref, ks_ref, vs_ref, kw_ref, vw_ref, kc_ref, vc_ref, bc_ref, nb_ref,
                qg_ref, ov_ref, o_ref, slog_ref, dlog_ref, mrun_ref, mb_ref, acc_ref):
    DH, HPG, TQ = NSA_DH, NSA_HPG, NSA_TQ
    NR = TQ // QT
    RB = HPG * QT
    R = NR * RB
    NSEL = 32
    qi = pl.program_id(1)
    t0 = qi * TQ

    def rows(r):
        return slice(r * RB, (r + 1) * RB)

    def to_rows(per_head):
        return jnp.concatenate([per_head[h][r * QT:(r + 1) * QT] for r in range(NR) for h in range(HPG)],
                               axis=0)

    def lanes(x, c):
        return x[:, c * QT:(c + 1) * QT]

    qblk = q_ref[...]
    qg = qg_ref[...] * (DH ** -0.5 * LOG2E)
    q4b = to_rows([_head_rms(qblk[:, h * DH:(h + 1) * DH], qg) for h in range(HPG)]).astype(BF16)
    nb0 = nb_ref[:, 0].reshape(RB, QT)
    nb1 = nb_ref[:, 1].reshape(RB, QT)

    sc = _dot_nt(q4b, kc_ref[0, 0]) + to_rows([bc_ref[h] for h in range(HPG)])
    mc = jnp.max(sc, axis=-1, keepdims=True)
    ec = jnp.where(sc > 0.5 * NEG, jnp.exp2(sc - mc), 0.0)
    lc = jnp.sum(ec, axis=-1, keepdims=True)
    pc = ec / jnp.where(lc > 0.0, lc, 1.0)
    o_cmp = _dot(pc.astype(BF16), vc_ref[0, 0])[:, :DH]

    psum = jnp.concatenate(
        [sum(pc[r * RB + h * QT:r * RB + (h + 1) * QT] for h in range(HPG)) for r in range(NR)], axis=0)
    ovm = ov_ref[...]
    hi, mid, lo = _split3(psum)
    imp_t = (_dot_nt(ovm, hi) + _dot_nt(ovm, mid) + _dot_nt(ovm, lo))[0:NSEL, :]
    jrow = lax.broadcasted_iota(jnp.int32, (NSEL, TQ), 0)
    blk = (t0 + lax.broadcasted_iota(jnp.int32, (NSEL, TQ), 1)) // SEL_LEN
    forced = (jrow == 0) | (jrow == blk) | (jrow == blk - 1)
    imp_t = jnp.where(forced, FORCE_SCORE, imp_t)
    imp_t = jnp.where(jrow <= blk, imp_t, NEG)
    cnt = jnp.zeros((NSEL, TQ), F32)
    for jp in range(NSEL):
        rowv = imp_t[jp:jp + 1, :]
        beats = (rowv > imp_t) | ((rowv == imp_t) & (jrow > jp))
        cnt = cnt + jnp.where(beats, 1.0, 0.0)
    selb_t = jnp.where(cnt < float(SEL_TOPN), 0.0, NEG)
    selb = jnp.concatenate([selb_t, jnp.zeros((QT - NSEL, TQ), F32)], axis=0).T
    selb4 = to_rows([selb[:, :DH]] * HPG).astype(BF16)
    qaug = jnp.concatenate([q4b, selb4], axis=1)

    mrun_ref[...] = jnp.full((R, QT), NEG, F32)

    def tile(ref, start, size):
        return ref[0, 0, pl.ds(pl.multiple_of(start, QT), size), :]

    def track_max(s):
        m = mrun_ref[...]
        for c in range(TQ // QT):
            m = jnp.maximum(m, lanes(s, c))
        mrun_ref[...] = m

    def sel_far(kt, carry):
        s = _dot_nt(qaug, tile(ks_ref, kt * TQ, TQ))
        slog_ref[kt] = s
        track_max(s)
        return carry

    lax.fori_loop(0, jnp.maximum(qi - 1, 0), sel_far, 0)

    @pl.when(qi >= 1)
    def _():
        s = _dot_nt(qaug, tile(ks_ref, (qi - 1) * TQ, TQ))
        top = s[0:RB]
        top = jnp.concatenate([top[:, :TQ - QT], top[:, TQ - QT:] + nb1], axis=1)
        s = jnp.concatenate([top, s[RB:]], axis=0)
        slog_ref[qi - 1] = s
        track_max(s)

    for r in range(NR):
        s = _dot_nt(qaug[rows(r)], tile(ks_ref, t0, (r + 1) * QT))
        parts = [lanes(s, c) for c in range(r + 1)]
        parts[r] = parts[r] + nb0
        if r >= 1:
            parts[r - 1] = parts[r - 1] + nb1
        m = mrun_ref[rows(r), :]
        for c in range(r + 1):
            dlog_ref[rows(r), c * QT:(c + 1) * QT] = parts[c]
            m = jnp.maximum(m, parts[c])
        mrun_ref[rows(r), :] = m
    mb_ref[...] = jnp.broadcast_to(jnp.max(mrun_ref[...], axis=-1, keepdims=True), (R, QT))

    acc_ref[...] = jnp.zeros_like(acc_ref)

    def sel_pv(kt, carry):
        mb = mb_ref[...]
        s = slog_ref[kt]
        p = jnp.concatenate([jnp.exp2(lanes(s, c) - mb) for c in range(TQ // QT)], axis=1)
        acc_ref[...] += _dot(p.astype(BF16), tile(vs_ref, kt * TQ, TQ))
        return carry

    lax.fori_loop(0, qi, sel_pv, 0)
    for r in range(NR):
        mb = mb_ref[rows(r), :]
        p = jnp.concatenate([jnp.exp2(dlog_ref[rows(r), c * QT:(c + 1) * QT] - mb)
                             for c in range(r + 1)], axis=1)
        acc_ref[rows(r), :] += _dot(p.astype(BF16), tile(vs_ref, t0, (r + 1) * QT))
    acc_s = acc_ref[...]

    ii = lax.broadcasted_iota(jnp.int32, (RB, QT), 0) % QT
    jj = lax.broadcasted_iota(jnp.int32, (RB, QT), 1)
    nwt = WIN // QT
    acc_w = []
    for r in range(NR):
        start = t0 + r * QT
        s = _dot_nt(q4b[rows(r)], tile(kw_ref, start, WIN + QT))
        parts = [lanes(s, c) for c in range(nwt + 1)]
        parts[nwt] = parts[nwt] + nb0
        parts[nwt - 1] = parts[nwt - 1] + nb1
        parts[0] = jnp.where(jj > ii, parts[0], NEG)
        for c in range(nwt - r):
            parts[c] = jnp.where(qi > 0, parts[c], NEG)
        m = parts[0]
        for c in range(1, nwt + 1):
            m = jnp.maximum(m, parts[c])
        m = jnp.max(m, axis=-1, keepdims=True)
        p = jnp.concatenate([jnp.exp2(parts[c] - m) for c in range(nwt + 1)], axis=1)
        acc_w.append(_dot(p.astype(BF16), tile(vw_ref, start, WIN + QT)))
    acc_w = jnp.concatenate(acc_w, axis=0)

    gt = _sigmoid(gt_ref[...])

    def gate_col(br):
        return to_rows([gt[:, br * HPG + h:br * HPG + h + 1] for h in range(HPG)])

    def inv_l(acc):
        l = acc[:, DH:DH + 1]
        return 1.0 / jnp.where(l > 0.0, l, 1.0)

    o = (gate_col(0) * o_cmp + (gate_col(1) * inv_l(acc_s)) * acc_s[:, :DH]
         + (gate_col(2) * inv_l(acc_w)) * acc_w[:, :DH])
    o_ref[...] = jnp.concatenate(
        [jnp.concatenate([o[r * RB + h * QT:r * RB + (h + 1) * QT] for r in range(NR)], axis=0)
         for h in range(HPG)], axis=1).astype(o_ref.dtype)


def _nsa(z, zg, prep, bias, q_gain, ovm, batch, seq):
    N = z.shape[0]
    G, HPG, DH, TQ = NSA_G, NSA_HPG, NSA_DH, NSA_TQ
    nq = seq // TQ
    R = HPG * TQ
    oks, ovs, okw, ovw, okc, ovc = prep
    QW = HPG * DH
    nbias = bias[:, :2 * QT].reshape(-1, 2, QT, QT)
    cbias = bias[:, 2 * QT:]

    def kvspec(a):
        return pl.BlockSpec((1, 1) + a.shape[2:], lambda p, i: (p // G, p % G, 0, 0))

    return pl.pallas_call(
        _nsa_kernel,
        grid=(batch * G, nq),
        in_specs=[pl.BlockSpec((TQ, QW), lambda p, i: ((p // G) * nq + i, OFF_NQ // QW + p % G)),
                  pl.BlockSpec((TQ, 128), lambda p, i: ((p // G) * nq + i, p % G)),
                  kvspec(oks), kvspec(ovs), kvspec(okw), kvspec(ovw), kvspec(okc), kvspec(ovc),
                  pl.BlockSpec((HPG, TQ, QT), lambda p, i: (p % G, i, 0)),
                  pl.BlockSpec((HPG, 2, QT, QT), lambda p, i: (p % G, 0, 0, 0)),
                  pl.BlockSpec((1, DH), lambda p, i: (0, 0)),
                  pl.BlockSpec((QT, QT), lambda p, i: (0, 0))],
        out_specs=pl.BlockSpec((TQ, QW), lambda p, i: ((p // G) * nq + i, p % G)),
        out_shape=jax.ShapeDtypeStruct((N, G * QW), BF16),
        scratch_shapes=[pltpu.VMEM((nq - 1, R, TQ), F32), pltpu.VMEM((R, TQ), F32),
                        pltpu.VMEM((R, QT), F32), pltpu.VMEM((R, QT), F32),
                        pltpu.VMEM((R, 2 * DH), F32)],
        compiler_params=_cparams(("parallel", "arbitrary")),
        name="nsa_attention",
    )(z, zg, oks, ovs, okw, ovw, okc, ovc, cbias, nbias, q_gain, ovm)


def _merge_kernel(h_ref, mod_ref, za_ref, zb_ref, zc_ref, oa_ref, ys_ref, oc_ref,
                  wa_ref, wb_ref, wc_ref, wo_ref, o_ref):
    D = h_ref.shape[1]
    ya = _dot(oa_ref[...], wa_ref[...])
    zz = _dot(ys_ref[...], wb_ref[...])
    yb = zz[:, :D] * _sigmoid(zz[:, D:])
    yc = _dot(oc_ref[...], wc_ref[...])
    merged = _sigmoid(za_ref[...]) * ya + _sigmoid(zb_ref[...]) * yb + _sigmoid(zc_ref[...]) * yc
    o_ref[...] = h_ref[...] + mod_ref[0, 5:6, :] * _dot(merged.astype(BF16), wo_ref[...])


def _merge(h, mod, z, oa, ys, oc, wa, wb, wc, wo, seq):
    N, D = h.shape
    tm = 256
    tpb = seq // tm
    row = lambda w: pl.BlockSpec((tm, w), lambda i: (i, 0))
    res = lambda a: pl.BlockSpec(a.shape, lambda i: (0, 0), pipeline_mode=pl.Buffered(1))
    return pl.pallas_call(
        _merge_kernel,
        grid=(N // tm,),
        in_specs=[row(D),
                  pl.BlockSpec((1, 9, D), lambda i: (i // tpb, 0, 0)),
                  pl.BlockSpec((tm, D), lambda i: (i, OFF_ZA // D)),
                  pl.BlockSpec((tm, D), lambda i: (i, OFF_ZB // D)),
                  pl.BlockSpec((tm, D), lambda i: (i, OFF_ZC // D)),
                  row(oa.shape[1]), row(ys.shape[1]), row(oc.shape[1]),
                  res(wa), res(wb), res(wc), res(wo)],
        out_specs=row(D),
        out_shape=jax.ShapeDtypeStruct((N, D), F32),
        compiler_params=_cparams(("parallel",)),
        name="mix_merge",
    )(h, mod, z, z, z, oa, ys, oc, wa, wb, wc, wo)


def _permute_w_in(w):
    gate0 = 4 * 512 + 512 + 1024 + 6 * 256
    ngate = 3 * NSA_G * NSA_HPG
    main = jnp.concatenate([w[:, gate0 + ngate:], w[:, :gate0]], axis=1)
    gw = w[:, gate0:gate0 + ngate].reshape(w.shape[0], 3, NSA_G, NSA_HPG)
    gw = gw.transpose(0, 2, 1, 3).reshape(w.shape[0], NSA_G, 3 * NSA_HPG)
    gw = jnp.pad(gw, ((0, 0), (0, 0), (0, 128 - 3 * NSA_HPG))).reshape(w.shape[0], NSA_G * 128)
    return main.astype(BF16), gw.astype(BF16)


def _overlap_matrix():
    j = jnp.arange(QT)[:, None]
    n = jnp.arange(QT)[None, :]
    st = n * CMP_STRIDE
    ov = (st < j * SEL_LEN + SEL_LEN) & (st + CMP_LEN > j * SEL_LEN)
    return jnp.where(ov, 1.0, 0.0).astype(BF16)


def kernel(x, c, ada_w, ada_b, norm_g, ffn1_wi, ffn1_wo, ffn2_wi, ffn2_wo, w_in, hg_lb_logits,
           hg_onorm, hg_proj, ssm_a_re, ssm_a_im, ssm_log_dt, ssm_b_re, ssm_b_im, ssm_c_re,
           ssm_c_im, ssm_d, ssm_glu_w, nsa_q_gain, nsa_k_gain, nsa_pe_k, nsa_pe_v, nsa_phi_k,
           nsa_phi_v, nsa_proj, rel_table, w_out):
    B, S, D = x.shape
    L = ada_w.shape[0]
    N = B * S
    assert S % 512 == 0 and S // SEL_LEN == 32 and S // QT == 16
    lb_cum = jnp.cumsum(jax.nn.softmax(hg_lb_logits.astype(F32), axis=0), axis=0)
    lower_bounds = lb_cum - lb_cum[0:1]
    mods = _mods(c, ada_w, ada_b).reshape(L, B, 9, D)
    bias = _bias_tables(rel_table, S)
    ovm = _overlap_matrix()
    h = x.reshape(N, D)
    for l in range(L):
        mod = mods[l]
        h = _ffn(h, mod, norm_g[l, 0:1], ffn1_wi, ffn1_wo, l, 0, S)
        w_main, w_gate = _permute_w_in(w_in[l])
        z, zg = _win(h, mod, norm_g[l, 1:2], w_main, w_gate, S)
        oa = _hgrn(z, lower_bounds[l:l + 1], hg_onorm[l:l + 1], B, S)
        sp = _ssm_params(ssm_a_re[l], ssm_a_im[l], ssm_log_dt[l], ssm_b_re[l], ssm_b_im[l],
                         ssm_c_re[l], ssm_c_im[l])
        ys = _ssm(z, sp, ssm_d[l:l + 1], B, S)
        prep = _nsa_prep(z, nsa_k_gain[l:l + 1], nsa_pe_k[l], nsa_pe_v[l],
                         nsa_phi_k[l].astype(BF16), nsa_phi_v[l].astype(BF16), B, S)
        oc = _nsa(z, zg, prep, bias, nsa_q_gain[l:l + 1], ovm, B, S)
        h = _merge(h, mod, z, oa, ys, oc, hg_proj[l].astype(BF16), ssm_glu_w[l].astype(BF16),
                   nsa_proj[l].astype(BF16), w_out[l].astype(BF16), S)
        h = _ffn(h, mod, norm_g[l, 2:3], ffn2_wi, ffn2_wo, l, 6, S)
    return h.reshape(B, S, D)
```

```python
import functools
import math

import jax
import jax.numpy as jnp
from jax import lax
from jax.experimental import pallas as pl
from jax.experimental.pallas import tpu as pltpu

F32 = jnp.float32
BF16 = jnp.bfloat16

EPS = 1e-6
NEG = -1e30
LOG2E = 1.4426950408889634
FORCE_SCORE = 1e4

HG_HEADS, HG_D, HG_CHUNK = 4, 128, 16
SSM_GROUPS, SSM_P, SSM_N, SSM_CHUNK = 32, 16, 64, 8
NSA_G, NSA_HPG, NSA_DH = 4, 4, 64
CMP_LEN, CMP_STRIDE, SEL_LEN, SEL_TOPN, WIN = 32, 16, 64, 16, 512
REL_BUCKETS, REL_MAX_DIST = 32, 128
QT = 128
NSA_TQ = 512

VMEM_LIMIT = 60 * 1024 * 1024

OFF_ZA, OFF_ZB, OFF_ZC = 0, 2048, 4096
OFF_HQ, OFF_HF, OFF_HI, OFF_HG = 6144, 6656, 7168, 7680
OFF_SU, OFF_NQ = 8192, 8704
OFF_KC, OFF_VC, OFF_KS, OFF_VS, OFF_KW, OFF_VW = 9728, 9984, 10240, 10496, 10752, 11008
Z_WIDTH = 11264


def _cparams(sem):
    return pltpu.CompilerParams(dimension_semantics=sem, vmem_limit_bytes=VMEM_LIMIT)


def _dot(a, b):
    return jnp.dot(a, b, preferred_element_type=F32)


def _dot_nt(a, b):
    return lax.dot_general(a, b, (((1,), (1,)), ((), ())), preferred_element_type=F32)


def _dot_tn(a, b):
    return lax.dot_general(a, b, (((0,), (0,)), ((), ())), preferred_element_type=F32)


def _sigmoid(x):
    return 1.0 / (1.0 + jnp.exp(-x))


def _split3(x):
    hi = x.astype(BF16)
    r = x - hi.astype(F32)
    mid = r.astype(BF16)
    lo = (r - mid.astype(F32)).astype(BF16)
    return hi, mid, lo


def _norm_mod(x, gain, shift, scale):
    ms = jnp.mean(x * x, axis=-1, keepdims=True)
    y = x * lax.rsqrt(ms + EPS) * gain
    return y * (1.0 + scale) + shift


def _mod_kernel(c_ref, w_ref, b_ref, o_ref):
    c = c_ref[...]
    ca = (c * _sigmoid(c)).astype(BF16)
    o_ref[0] = _dot(ca, w_ref[0].astype(BF16)) + b_ref[0]


def _mods(c, ada_w, ada_b):
    L, D, W = ada_w.shape
    B = c.shape[0]
    tn = 1024
    return pl.pallas_call(
        _mod_kernel,
        grid=(L, W // tn),
        in_specs=[pl.BlockSpec((B, D), lambda l, j: (0, 0)),
                  pl.BlockSpec((1, D, tn), lambda l, j: (l, 0, j)),
                  pl.BlockSpec((1, 1, tn), lambda l, j: (l, 0, j))],
        out_specs=pl.BlockSpec((1, B, tn), lambda l, j: (l, 0, j)),
        out_shape=jax.ShapeDtypeStruct((L, B, W), F32),
        compiler_params=_cparams(("parallel", "parallel")),
        name="adaln_mod",
    )(c, ada_w, ada_b.reshape(L, 1, W))


def _ffn_kernel(h_ref, mod_ref, g_ref, wi1_ref, wi2_ref, wo_ref, o_ref, u_s, *, k0, nf):
    f = pl.program_id(1)

    @pl.when(f == 0)
    def _():
        u = _norm_mod(h_ref[...], g_ref[...], mod_ref[0, k0:k0 + 1, :], mod_ref[0, k0 + 1:k0 + 2, :])
        u_s[...] = u.astype(BF16)
        o_ref[...] = jnp.zeros_like(o_ref)

    u = u_s[...]
    a1 = _dot(u, wi1_ref[0].astype(BF16))
    a2 = _dot(u, wi2_ref[0].astype(BF16))
    act = (a1 * _sigmoid(a1) * a2).astype(BF16)
    o_ref[...] += _dot(act, wo_ref[0].astype(BF16))

    @pl.when(f == nf - 1)
    def _():
        o_ref[...] = h_ref[...] + (0.5 * mod_ref[0, k0 + 2:k0 + 3, :]) * o_ref[...]


def _ffn(h, mod, gain, wi, wo, layer, k0, seq):
    N, D = h.shape
    dff = wo.shape[1]
    tm, tf = 1024, 256
    nf = dff // tf
    tpb = seq // tm
    return pl.pallas_call(
        functools.partial(_ffn_kernel, k0=k0, nf=nf),
        grid=(N // tm, nf),
        in_specs=[pl.BlockSpec((tm, D), lambda i, f: (i, 0)),
                  pl.BlockSpec((1, 9, D), lambda i, f: (i // tpb, 0, 0)),
                  pl.BlockSpec((1, D), lambda i, f: (0, 0)),
                  pl.BlockSpec((1, D, tf), lambda i, f: (layer, 0, f)),
                  pl.BlockSpec((1, D, tf), lambda i, f: (layer, 0, f + nf)),
                  pl.BlockSpec((1, tf, D), lambda i, f: (layer, f, 0))],
        out_specs=pl.BlockSpec((tm, D), lambda i, f: (i, 0)),
        out_shape=jax.ShapeDtypeStruct((N, D), F32),
        scratch_shapes=[pltpu.VMEM((tm, D), BF16)],
        compiler_params=_cparams(("parallel", "arbitrary")),
        name="ffn",
    )(h, mod, gain, wi, wi, wo)


def _win_kernel(h_ref, mod_ref, g_ref, w_ref, wg_ref, z_ref, zg_ref, u_s):
    j = pl.program_id(1)

    @pl.when(j == 0)
    def _():
        u = _norm_mod(h_ref[...], g_ref[...], mod_ref[0, 3:4, :], mod_ref[0, 4:5, :])
        ub = u.astype(BF16)
        u_s[...] = ub
        zg_ref[...] = _dot(ub, wg_ref[...])

    z_ref[...] = _dot(u_s[...], w_ref[...])


def _win(h, mod, gain, w_main, w_gate, seq):
    N, D = h.shape
    tm, tn = 1024, 1024
    tpb = seq // tm
    GW = w_gate.shape[1]
    return pl.pallas_call(
        _win_kernel,
        grid=(N // tm, Z_WIDTH // tn),
        in_specs=[pl.BlockSpec((tm, D), lambda i, j: (i, 0)),
                  pl.BlockSpec((1, 9, D), lambda i, j: (i // tpb, 0, 0)),
                  pl.BlockSpec((1, D), lambda i, j: (0, 0)),
                  pl.BlockSpec((D, tn), lambda i, j: (0, j)),
                  pl.BlockSpec((D, GW), lambda i, j: (0, 0))],
        out_specs=[pl.BlockSpec((tm, tn), lambda i, j: (i, j)),
                   pl.BlockSpec((tm, GW), lambda i, j: (i, 0))],
        out_shape=[jax.ShapeDtypeStruct((N, Z_WIDTH), F32),
                   jax.ShapeDtypeStruct((N, GW), F32)],
        scratch_shapes=[pltpu.VMEM((tm, D), BF16)],
        compiler_params=_cparams(("parallel", "arbitrary")),
        name="in_proj",
    )(h, mod, gain, w_main, w_gate)


def _hgrn_kernel(q_ref, f_ref, i_ref, g_ref, lb_ref, on_ref, o_ref, st_ref, *, tb):
    C, H, DK = HG_CHUNK, HG_HEADS, HG_D
    nc = tb // C
    W = H * DK

    @pl.when(pl.program_id(1) == 0)
    def _():
        st_ref[...] = jnp.zeros_like(st_ref)

    q = q_ref[...]
    qs = q * _sigmoid(q)
    x = f_ref[...]
    iv = i_ref[...]
    lb = lb_ref[...]
    sp = jnp.log1p(jnp.exp(-jnp.abs(x)))
    lsig = jnp.minimum(x, 0.0) - sp
    a = jnp.log(jnp.maximum(lb, 1e-38))
    bterm = jnp.log1p(-lb) + lsig
    lae = jnp.maximum(a, bterm) + jnp.log1p(jnp.exp(-jnp.abs(a - bterm)))
    log_f = jnp.where(lb > 0.0, lae, bterm)
    k = (1.0 - lb) * jnp.exp(jnp.minimum(-x, 0.0) - sp)

    r = lax.broadcasted_iota(jnp.int32, (tb, tb), 0)
    cidx = lax.broadcasted_iota(jnp.int32, (tb, tb), 1)
    lmat = jnp.where(((r // C) == (cidx // C)) & (cidx <= r), 1.0, 0.0).astype(BF16)
    hi, mid, lo = _split3(log_f)
    b = _dot(lmat, hi) + _dot(lmat, mid) + _dot(lmat, lo)

    qe = (qs * jnp.exp(b)).astype(BF16)
    b3 = b.reshape(nc, C, W)
    bend = b3[:, C - 1:C, :]
    kdec = (k.reshape(nc, C, W) * jnp.exp(bend - b3)).reshape(tb, W).astype(BF16)
    ebend = jnp.exp(bend)
    ib = iv.astype(BF16)
    o_heads = []
    for h in range(H):
        ls = slice(h * DK, (h + 1) * DK)
        uts = [_dot_tn(ib[c * C:(c + 1) * C, ls], kdec[c * C:(c + 1) * C, ls]) for c in range(nc)]
        st = st_ref[h]
        sts = []
        for c in range(nc):
            sts.append(st.astype(BF16))
            st = st * ebend[c][:, ls] + uts[c]
        st_ref[h] = st
        o_heads.append(jnp.concatenate(
            [_dot_nt(qe[c * C:(c + 1) * C, ls], sts[c]) for c in range(nc)], axis=0))
    o = jnp.concatenate(o_heads, axis=1)

    b2 = b * LOG2E
    tmod = lax.broadcasted_iota(jnp.int32, (tb, W), 0) % C
    rr = lax.broadcasted_iota(jnp.int32, (W, W), 0) // DK
    cc = lax.broadcasted_iota(jnp.int32, (W, W), 1) // DK
    ones_bd = jnp.where(rr == cc, 1.0, 0.0).astype(BF16)
    for d in range(C):
        if d == 0:
            p = qs * k
            isd = iv
        else:
            kd = pltpu.roll(k, d, 0)
            bd = pltpu.roll(b2, d, 0)
            isd = pltpu.roll(iv, d, 0)
            p = jnp.where(tmod >= d, qs * kd * jnp.exp2(b2 - bd), 0.0)
        o = o + _dot(p.astype(BF16), ones_bd) * isd

    g = g_ref[...]
    gs = g * _sigmoid(g)
    onw = on_ref[...]
    outs = []
    for h in range(H):
        ls = slice(h * DK, (h + 1) * DK)
        oh = o[:, ls]
        outs.append(oh * lax.rsqrt(jnp.mean(oh * oh, axis=-1, keepdims=True) + EPS) * onw * gs[:, ls])
    o_ref[...] = jnp.concatenate(outs, axis=1).astype(o_ref.dtype)


def _hgrn(z, lb, onorm, batch, seq):
    N = z.shape[0]
    tb = 256
    nb = seq // tb
    W = HG_HEADS * HG_D

    def zspec(off):
        return pl.BlockSpec((tb, W), lambda b, j: (b * nb + j, off // W))

    return pl.pallas_call(
        functools.partial(_hgrn_kernel, tb=tb),
        grid=(batch, nb),
        in_specs=[zspec(OFF_HQ), zspec(OFF_HF), zspec(OFF_HI), zspec(OFF_HG),
                  pl.BlockSpec((1, W), lambda b, j: (0, 0)),
                  pl.BlockSpec((1, HG_D), lambda b, j: (0, 0))],
        out_specs=pl.BlockSpec((tb, W), lambda b, j: (b * nb + j, 0)),
        out_shape=jax.ShapeDtypeStruct((N, W), BF16),
        scratch_shapes=[pltpu.VMEM((HG_HEADS, HG_D, HG_D), F32)],
        compiler_params=_cparams(("parallel", "arbitrary")),
        name="hgrn2",
    )(z, z, z, z, lb, onorm)


def _ssm_kernel(u_ref, bm_ref, cr_ref, ci_ref, kt_ref, p1r_ref, p1i_ref, p2r_ref, p2i_ref,
                alr_ref, ali_ref, d_ref, o_ref, xr_ref, xi_ref, xpr_ref, xpi_ref, *, tt):
    LC = SSM_CHUNK
    nc = tt // LC
    W = SSM_GROUPS * SSM_N

    @pl.when(pl.program_id(1) == 0)
    def _():
        xr_ref[...] = jnp.zeros_like(xr_ref)
        xi_ref[...] = jnp.zeros_like(xi_ref)

    u = u_ref[...]
    ub = u.astype(BF16)
    bu = _dot(ub, bm_ref[...])
    bur = bu[:, :W].reshape(nc, LC, W)
    bui = bu[:, W:].reshape(nc, LC, W)
    p1r, p1i = p1r_ref[...], p1i_ref[...]
    vr = jnp.sum(bur * p1r - bui * p1i, axis=1)
    vi = jnp.sum(bur * p1i + bui * p1r, axis=1)

    alr, ali = alr_ref[...], ali_ref[...]
    xr, xi = xr_ref[...], xi_ref[...]
    for c in range(nc):
        xpr_ref[c:c + 1, :] = xr
        xpi_ref[c:c + 1, :] = xi
        nr = alr * xr - ali * xi + vr[c:c + 1, :]
        ni = alr * xi + ali * xr + vi[c:c + 1, :]
        xr, xi = nr, ni
    xr_ref[...] = xr
    xi_ref[...] = xi

    xpr = xpr_ref[...][:, None, :]
    xpi = xpi_ref[...][:, None, :]
    p2r, p2i = p2r_ref[...], p2i_ref[...]
    zr = (p2r * xpr - p2i * xpi).reshape(tt, W).astype(BF16)
    zi = (p2r * xpi + p2i * xpr).reshape(tt, W).astype(BF16)
    y = _dot(zr, cr_ref[...]) + _dot(zi, ci_ref[...])

    tmod = lax.broadcasted_iota(jnp.int32, u.shape, 0) % LC
    for tau in range(LC):
        if tau == 0:
            us = ub
        else:
            us = jnp.where(tmod >= tau, pltpu.roll(u, tau, 0), 0.0).astype(BF16)
        y = y + _dot(us, kt_ref[tau])
    y = y + d_ref[...] * u
    g = 0.5 * y * (1.0 + jnp.tanh(0.7978845608028654 * (y + 0.044715 * (y * y * y))))
    o_ref[...] = g.astype(o_ref.dtype)


def _ssm_params(a_re, a_im, log_dt, b_re, b_im, c_re, c_im):
    G, P, N, LC = SSM_GROUPS, SSM_P, SSM_N, SSM_CHUNK
    hp = lax.Precision.HIGHEST
    a_re = jnp.minimum(a_re.astype(F32), -1e-4)
    a_im = a_im.astype(F32)
    dt = jnp.exp(log_dt.astype(F32))[:, None]
    mag = jnp.exp(dt * a_re)
    ab_re, ab_im = mag * jnp.cos(dt * a_im), mag * jnp.sin(dt * a_im)
    den = a_re * a_re + a_im * a_im
    nr = ab_re - 1.0
    z_re = (nr * a_re + ab_im * a_im) / den
    z_im = (ab_im * a_re - nr * a_im) / den
    b_re, b_im = b_re.astype(F32), b_im.astype(F32)
    bb_re = z_re[..., None] * b_re - z_im[..., None] * b_im
    bb_im = z_re[..., None] * b_im + z_im[..., None] * b_re
    kk = jnp.arange(LC + 1, dtype=F32)[:, None, None]
    pm = jnp.exp(kk * dt * a_re)
    pw_re, pw_im = pm * jnp.cos(kk * dt * a_im), pm * jnp.sin(kk * dt * a_im)
    eye = jnp.eye(G, dtype=F32)
    bm_re = jnp.einsum('gnp,gh->gphn', bb_re, eye).reshape(G * P, G * N)
    bm_im = jnp.einsum('gnp,gh->gphn', bb_im, eye).reshape(G * P, G * N)
    bmat = jnp.concatenate([bm_re, bm_im], axis=1)
    c_re, c_im = c_re.astype(F32), c_im.astype(F32)
    cr = jnp.einsum('gpn,gh->gnhp', c_re, eye).reshape(G * N, G * P)
    ci = -jnp.einsum('gpn,gh->gnhp', c_im, eye).reshape(G * N, G * P)
    t_re = pw_re[:LC, :, :, None] * bb_re[None] - pw_im[:LC, :, :, None] * bb_im[None]
    t_im = pw_re[:LC, :, :, None] * bb_im[None] + pw_im[:LC, :, :, None] * bb_re[None]
    kt = (jnp.einsum('gqn,tgnp->tgpq', c_re, t_re, precision=hp)
          - jnp.einsum('gqn,tgnp->tgpq', c_im, t_im, precision=hp))
    ktm = jnp.einsum('tgpq,gh->tgphq', kt, eye).reshape(LC, G * P, G * P)
    flat = lambda x: x.reshape(x.shape[0], G * N)
    p1r, p1i = flat(pw_re[LC - 1::-1][:LC]), flat(pw_im[LC - 1::-1][:LC])
    p2r, p2i = flat(pw_re[1:LC + 1]), flat(pw_im[1:LC + 1])
    alr, ali = flat(pw_re[LC:LC + 1]), flat(pw_im[LC:LC + 1])
    return (bmat.astype(BF16), cr.astype(BF16), ci.astype(BF16), ktm.astype(BF16),
            p1r, p1i, p2r, p2i, alr, ali)


def _ssm(z, params, d_skip, batch, seq):
    N = z.shape[0]
    tt = 256
    nb = seq // tt
    CW = SSM_GROUPS * SSM_P
    W = SSM_GROUPS * SSM_N
    LC = SSM_CHUNK
    bmat, cr, ci, ktm, p1r, p1i, p2r, p2i, alr, ali = params
    full = lambda a: pl.BlockSpec(a.shape, lambda b, j, _n=a.ndim: (0,) * _n)
    return pl.pallas_call(
        functools.partial(_ssm_kernel, tt=tt),
        grid=(batch, nb),
        in_specs=[pl.BlockSpec((tt, CW), lambda b, j: (b * nb + j, OFF_SU // CW)),
                  full(bmat), full(cr), full(ci), full(ktm), full(p1r), full(p1i),
                  full(p2r), full(p2i), full(alr), full(ali),
                  pl.BlockSpec((1, CW), lambda b, j: (0, 0))],
        out_specs=pl.BlockSpec((tt, CW), lambda b, j: (b * nb + j, 0)),
        out_shape=jax.ShapeDtypeStruct((N, CW), BF16),
        scratch_shapes=[pltpu.VMEM((1, W), F32), pltpu.VMEM((1, W), F32),
                        pltpu.VMEM((tt // LC, W), F32), pltpu.VMEM((tt // LC, W), F32)],
        compiler_params=_cparams(("parallel", "arbitrary")),
        name="s5_ssm",
    )(z, bmat, cr, ci, ktm, p1r, p1i, p2r, p2i, alr, ali, d_skip)


def _t5_bucket(dist):
    n = jnp.maximum(dist, 0)
    max_exact = REL_BUCKETS // 2
    nf = jnp.maximum(n, 1).astype(F32)
    large = max_exact + (jnp.log(nf / max_exact) / math.log(REL_MAX_DIST / max_exact)
                         * (REL_BUCKETS - max_exact)).astype(jnp.int32)
    large = jnp.minimum(large, REL_BUCKETS - 1)
    return jnp.where(n < max_exact, n, large)


def _bias_kernel(tab_ref, bk_ref, o_ref):
    h = pl.program_id(0)
    bk = bk_ref[...]
    base = tab_ref[REL_BUCKETS - 1, h]
    acc = jnp.full(bk.shape, NEG, F32)
    for k in range(REL_BUCKETS):
        acc = jnp.where(bk == k, (tab_ref[k, h] - base) * LOG2E, acc)
    o_ref[0] = acc


def _bias_tables(rel_table, seq):
    i = jnp.arange(QT)[:, None]
    j = jnp.arange(QT)[None, :]
    d0 = i - j
    b0 = jnp.where(d0 >= 0, _t5_bucket(d0), REL_BUCKETS)
    b1 = _t5_bucket(QT + i - j)
    t = jnp.arange(seq)[:, None]
    n = jnp.arange(QT)[None, :]
    n_cmp = (seq - CMP_LEN) // CMP_STRIDE + 1
    dc = t - (n * CMP_STRIDE + CMP_LEN - 1)
    bc = jnp.where((dc >= 0) & (n < n_cmp), _t5_bucket(dc), REL_BUCKETS)
    bk = jnp.concatenate([b0, b1, bc], axis=0).astype(jnp.int32)
    R = bk.shape[0]
    nh = rel_table.shape[1]
    return pl.pallas_call(
        _bias_kernel,
        grid=(nh,),
        in_specs=[pl.BlockSpec(memory_space=pltpu.SMEM),
                  pl.BlockSpec((R, QT), lambda h: (0, 0))],
        out_specs=pl.BlockSpec((1, R, QT), lambda h: (h, 0, 0)),
        out_shape=jax.ShapeDtypeStruct((nh, R, QT), F32),
        compiler_params=_cparams(("arbitrary",)),
        name="nsa_bias_tables",
    )(rel_table.astype(F32), bk)


def _head_rms(x, gain):
    return x * lax.rsqrt(jnp.mean(x * x, axis=-1, keepdims=True) + EPS) * gain


def _nsa_prep_kernel(kc_ref, vc_ref, ks_ref, vs_ref, kw_ref, vw_ref, kg_ref, pek_ref, pev_ref,
                     phik_ref, phiv_ref, oks_ref, ovs_ref, okw_ref, ovw_ref, okc_ref, ovc_ref,
                     xpad_ref, *, seq):
    DH = NSA_DH
    kg = kg_ref[...]
    srow = lax.broadcasted_iota(jnp.int32, (seq, DH), 0)
    lane = lax.broadcasted_iota(jnp.int32, (seq, DH), 1)
    onehot = jnp.where(lane == srow // SEL_LEN, 1.0, 0.0).astype(BF16)
    ones = jnp.ones((seq, DH), BF16)
    ones_c = jnp.ones((QT, DH), BF16)
    xpad_ref[seq:seq + CMP_LEN, :] = jnp.zeros((CMP_LEN, DH), F32)
    for g in range(NSA_G):
        ls = slice(g * DH, (g + 1) * DH)
        ksn = _head_rms(ks_ref[:, ls], kg).astype(BF16)
        oks_ref[0, g] = jnp.concatenate([ksn, onehot], axis=1)
        okw_ref[0, g, 0:WIN, :] = jnp.zeros((WIN, DH), BF16)
        okw_ref[0, g, WIN:WIN + seq, :] = _head_rms(kw_ref[:, ls], kg).astype(BF16)
        ovs_ref[0, g] = jnp.concatenate([vs_ref[:, ls].astype(BF16), ones], axis=1)
        ovw_ref[0, g, 0:WIN, :] = jnp.zeros((WIN, 2 * DH), BF16)
        ovw_ref[0, g, WIN:WIN + seq, :] = jnp.concatenate([vw_ref[:, ls].astype(BF16), ones], axis=1)
        for src_ref, pe_ref, phi_ref, is_k in ((kc_ref, pek_ref, phik_ref, True),
                                               (vc_ref, pev_ref, phiv_ref, False)):
            xpad_ref[0:seq, :] = src_ref[:, ls]
            acc = jnp.zeros((QT, DH), F32)
            for l in range(CMP_LEN):
                xl = xpad_ref[pl.ds(l, QT, stride=CMP_STRIDE), :] + pe_ref[l:l + 1, :]
                acc = acc + _dot(xl.astype(BF16), phi_ref[l])
            if is_k:
                okc_ref[0, g] = _head_rms(acc, kg).astype(BF16)
            else:
                ovc_ref[0, g] = jnp.concatenate([acc.astype(BF16), ones_c], axis=1)


def _nsa_prep(z, k_gain, pe_k, pe_v, phi_k, phi_v, batch, seq):
    G, DH = NSA_G, NSA_DH
    KW = G * DH

    def zspec(off):
        return pl.BlockSpec((seq, KW), lambda b: (b, off // KW))

    full = lambda a: pl.BlockSpec(a.shape, lambda b, _n=a.ndim: (0,) * _n)

    def ospec(rows, w):
        return pl.BlockSpec((1, G, rows, w), lambda b: (b, 0, 0, 0))

    def oshape(rows, w):
        return jax.ShapeDtypeStruct((batch, G, rows, w), BF16)

    return pl.pallas_call(
        functools.partial(_nsa_prep_kernel, seq=seq),
        grid=(batch,),
        in_specs=[zspec(OFF_KC), zspec(OFF_VC), zspec(OFF_KS), zspec(OFF_VS), zspec(OFF_KW),
                  zspec(OFF_VW), full(k_gain), full(pe_k), full(pe_v), full(phi_k), full(phi_v)],
        out_specs=[ospec(seq, 2 * DH), ospec(seq, 2 * DH), ospec(seq + WIN, DH),
                   ospec(seq + WIN, 2 * DH), ospec(QT, DH), ospec(QT, 2 * DH)],
        out_shape=[oshape(seq, 2 * DH), oshape(seq, 2 * DH), oshape(seq + WIN, DH),
                   oshape(seq + WIN, 2 * DH), oshape(QT, DH), oshape(QT, 2 * DH)],
        scratch_shapes=[pltpu.VMEM((seq + CMP_LEN, DH), F32)],
        compiler_params=_cparams(("parallel",)),
        name="nsa_kv_prep",
    )(z, z, z, z, z, z, k_gain, pe_k, pe_v, phi_k, phi_v)


def _nsa_kernel(q_ref, gt_ref, ks_ref, vs_ref, kw_ref, vw_ref, kc_ref, vc_ref, bc_ref, nb_ref,
                qg_ref, ov_ref, o_ref, slog_ref, dlog_ref, mrun_ref, mb_ref, acc_ref):
    DH, HPG, TQ = NSA_DH, NSA_HPG, NSA_TQ
    NR = TQ // QT
    RB = HPG * QT
    R = NR * RB
    NSEL = 32
    qi = pl.program_id(1)
    t0 = qi * TQ

    def rows(r):
        return slice(r * RB, (r + 1) * RB)

    def to_rows(per_head):
        return jnp.concatenate([per_head[h][r * QT:(r + 1) * QT] for r in range(NR) for h in range(HPG)],
                               axis=0)

    def lanes(x, c):
        return x[:, c * QT:(c + 1) * QT]

    qblk = q_ref[...]
    qg = qg_ref[...] * (DH ** -0.5 * LOG2E)
    q4b = to_rows([_head_rms(qblk[:, h * DH:(h + 1) * DH], qg) for h in range(HPG)]).astype(BF16)
    nb0 = nb_ref[:, 0].reshape(RB, QT)
    nb1 = nb_ref[:, 1].reshape(RB, QT)

    sc = _dot_nt(q4b, kc_ref[0, 0]) + to_rows([bc_ref[h] for h in range(HPG)])
    mc = jnp.max(sc, axis=-1, keepdims=True)
    ec = jnp.where(sc > 0.5 * NEG, jnp.exp2(sc - mc), 0.0)
    lc = jnp.sum(ec, axis=-1, keepdims=True)
    pc = ec / jnp.where(lc > 0.0, lc, 1.0)
    o_cmp = _dot(pc.astype(BF16), vc_ref[0, 0])[:, :DH]

    psum = jnp.concatenate(
        [sum(pc[r * RB + h * QT:r * RB + (h + 1) * QT] for h in range(HPG)) for r in range(NR)], axis=0)
    ovm = ov_ref[...]
    hi, mid, lo = _split3(psum)
    imp_t = (_dot_nt(ovm, hi) + _dot_nt(ovm, mid) + _dot_nt(ovm, lo))[0:NSEL, :]
    jrow = lax.broadcasted_iota(jnp.int32, (NSEL, TQ), 0)
    blk = (t0 + lax.broadcasted_iota(jnp.int32, (NSEL, TQ), 1)) // SEL_LEN
    forced = (jrow == 0) | (jrow == blk) | (jrow == blk - 1)
    imp_t = jnp.where(forced, FORCE_SCORE, imp_t)
    imp_t = jnp.where(jrow <= blk, imp_t, NEG)
    cnt = jnp.zeros((NSEL, TQ), F32)
    for jp in range(NSEL):
        rowv = imp_t[jp:jp + 1, :]
        beats = (rowv > imp_t) | ((rowv == imp_t) & (jrow > jp))
        cnt = cnt + jnp.where(beats, 1.0, 0.0)
    selb_t = jnp.where(cnt < float(SEL_TOPN), 0.0, NEG)
    selb = jnp.concatenate([selb_t, jnp.zeros((QT - NSEL, TQ), F32)], axis=0).T
    selb4 = to_rows([selb[:, :DH]] * HPG).astype(BF16)
    qaug = jnp.concatenate([q4b, selb4], axis=1)

    mrun_ref[...] = jnp.full((R, QT), NEG, F32)

    def tile(ref, start, size):
        return ref[0, 0, pl.ds(pl.multiple_of(start, QT), size), :]

    def track_max(s):
        m = mrun_ref[...]
        for c in range(TQ // QT):
            m = jnp.maximum(m, lanes(s, c))
        mrun_ref[...] = m

    def sel_far(kt, carry):
        s = _dot_nt(qaug, tile(ks_ref, kt * TQ, TQ))
        slog_ref[kt] = s
        track_max(s)
        return carry

    lax.fori_loop(0, jnp.maximum(qi - 1, 0), sel_far, 0)

    @pl.when(qi >= 1)
    def _():
        s = _dot_nt(qaug, tile(ks_ref, (qi - 1) * TQ, TQ))
        top = s[0:RB]
        top = jnp.concatenate([top[:, :TQ - QT], top[:, TQ - QT:] + nb1], axis=1)
        s = jnp.concatenate([top, s[RB:]], axis=0)
        slog_ref[qi - 1] = s
        track_max(s)

    for r in range(NR):
        s = _dot_nt(qaug[rows(r)], tile(ks_ref, t0, (r + 1) * QT))
        parts = [lanes(s, c) for c in range(r + 1)]
        parts[r] = parts[r] + nb0
        if r >= 1:
            parts[r - 1] = parts[r - 1] + nb1
        m = mrun_ref[rows(r), :]
        for c in range(r + 1):
            dlog_ref[rows(r), c * QT:(c + 1) * QT] = parts[c]
            m = jnp.maximum(m, parts[c])
        mrun_ref[rows(r), :] = m
    mb_ref[...] = jnp.broadcast_to(jnp.max(mrun_ref[...], axis=-1, keepdims=True), (R, QT))

    acc_ref[...] = jnp.zeros_like(acc_ref)

    def sel_pv(kt, carry):
        mb = mb_ref[...]
        s = slog_ref[kt]
        p = jnp.concatenate([jnp.exp2(lanes(s, c) - mb) for c in range(TQ // QT)], axis=1)
        acc_ref[...] += _dot(p.astype(BF16), tile(vs_ref, kt * TQ, TQ))
        return carry

    lax.fori_loop(0, qi, sel_pv, 0)
    for r in range(NR):
        mb = mb_ref[rows(r), :]
        p = jnp.concatenate([jnp.exp2(dlog_ref[rows(r), c * QT:(c + 1) * QT] - mb)
                             for c in range(r + 1)], axis=1)
        acc_ref[rows(r), :] += _dot(p.astype(BF16), tile(vs_ref, t0, (r + 1) * QT))
    acc_s = acc_ref[...]

    ii = lax.broadcasted_iota(jnp.int32, (RB, QT), 0) % QT
    jj = lax.broadcasted_iota(jnp.int32, (RB, QT), 1)
    nwt = WIN // QT
    acc_w = []
    for r in range(NR):
        start = t0 + r * QT
        s = _dot_nt(q4b[rows(r)], tile(kw_ref, start, WIN + QT))
        parts = [lanes(s, c) for c in range(nwt + 1)]
        parts[nwt] = parts[nwt] + nb0
        parts[nwt - 1] = parts[nwt - 1] + nb1
        parts[0] = jnp.where(jj > ii, parts[0], NEG)
        for c in range(nwt - r):
            parts[c] = jnp.where(qi > 0, parts[c], NEG)
        m = parts[0]
        for c in range(1, nwt + 1):
            m = jnp.maximum(m, parts[c])
        m = jnp.max(m, axis=-1, keepdims=True)
        p = jnp.concatenate([jnp.exp2(parts[c] - m) for c in range(nwt + 1)], axis=1)
        acc_w.append(_dot(p.astype(BF16), tile(vw_ref, start, WIN + QT)))
    acc_w = jnp.concatenate(acc_w, axis=0)

    gt = _sigmoid(gt_ref[...])

    def gate_col(br):
        return to_rows([gt[:, br * HPG + h:br * HPG + h + 1] for h in range(HPG)])

    def inv_l(acc):
        l = acc[:, DH:DH + 1]
        return 1.0 / jnp.where(l > 0.0, l, 1.0)

    o = (gate_col(0) * o_cmp + (gate_col(1) * inv_l(acc_s)) * acc_s[:, :DH]
         + (gate_col(2) * inv_l(acc_w)) * acc_w[:, :DH])
    o_ref[...] = jnp.concatenate(
        [jnp.concatenate([o[r * RB + h * QT:r * RB + (h + 1) * QT] for r in range(NR)], axis=0)
         for h in range(HPG)], axis=1).astype(o_ref.dtype)


def _nsa(z, zg, prep, bias, q_gain, ovm, batch, seq):
    N = z.shape[0]
    G, HPG, DH, TQ = NSA_G, NSA_HPG, NSA_DH, NSA_TQ
    nq = seq // TQ
    R = HPG * TQ
    oks, ovs, okw, ovw, okc, ovc = prep
    QW = HPG * DH
    nbias = bias[:, :2 * QT].reshape(-1, 2, QT, QT)
    cbias = bias[:, 2 * QT:]

    def kvspec(a):
        return pl.BlockSpec((1, 1) + a.shape[2:], lambda p, i: (p // G, p % G, 0, 0))

    return pl.pallas_call(
        _nsa_kernel,
        grid=(batch * G, nq),
        in_specs=[pl.BlockSpec((TQ, QW), lambda p, i: ((p // G) * nq + i, OFF_NQ // QW + p % G)),
                  pl.BlockSpec((TQ, 128), lambda p, i: ((p // G) * nq + i, p % G)),
                  kvspec(oks), kvspec(ovs), kvspec(okw), kvspec(ovw), kvspec(okc), kvspec(ovc),
                  pl.BlockSpec((HPG, TQ, QT), lambda p, i: (p % G, i, 0)),
                  pl.BlockSpec((HPG, 2, QT, QT), lambda p, i: (p % G, 0, 0, 0)),
                  pl.BlockSpec((1, DH), lambda p, i: (0, 0)),
                  pl.BlockSpec((QT, QT), lambda p, i: (0, 0))],
        out_specs=pl.BlockSpec((TQ, QW), lambda p, i: ((p // G) * nq + i, p % G)),
        out_shape=jax.ShapeDtypeStruct((N, G * QW), BF16),
        scratch_shapes=[pltpu.VMEM((nq - 1, R, TQ), F32), pltpu.VMEM((R, TQ), F32),
                        pltpu.VMEM((R, QT), F32), pltpu.VMEM((R, QT), F32),
                        pltpu.VMEM((R, 2 * DH), F32)],
        compiler_params=_cparams(("parallel", "arbitrary")),
        name="nsa_attention",
    )(z, zg, oks, ovs, okw, ovw, okc, ovc, cbias, nbias, q_gain, ovm)


def _merge_kernel(h_ref, mod_ref, za_ref, zb_ref, zc_ref, oa_ref, ys_ref, oc_ref,
                  wa_ref, wb_ref, wc_ref, wo_ref, o_ref):
    D = h_ref.shape[1]
    ya = _dot(oa_ref[...], wa_ref[...])
    zz = _dot(ys_ref[...], wb_ref[...])
    yb = zz[:, :D] * _sigmoid(zz[:, D:])
    yc = _dot(oc_ref[...], wc_ref[...])
    merged = _sigmoid(za_ref[...]) * ya + _sigmoid(zb_ref[...]) * yb + _sigmoid(zc_ref[...]) * yc
    o_ref[...] = h_ref[...] + mod_ref[0, 5:6, :] * _dot(merged.astype(BF16), wo_ref[...])


def _merge(h, mod, z, oa, ys, oc, wa, wb, wc, wo, seq):
    N, D = h.shape
    tm = 256
    tpb = seq // tm
    row = lambda w: pl.BlockSpec((tm, w), lambda i: (i, 0))
    res = lambda a: pl.BlockSpec(a.shape, lambda i: (0, 0), pipeline_mode=pl.Buffered(1))
    return pl.pallas_call(
        _merge_kernel,
        grid=(N // tm,),
        in_specs=[row(D),
                  pl.BlockSpec((1, 9, D), lambda i: (i // tpb, 0, 0)),
                  pl.BlockSpec((tm, D), lambda i: (i, OFF_ZA // D)),
                  pl.BlockSpec((tm, D), lambda i: (i, OFF_ZB // D)),
                  pl.BlockSpec((tm, D), lambda i: (i, OFF_ZC // D)),
                  row(oa.shape[1]), row(ys.shape[1]), row(oc.shape[1]),
                  res(wa), res(wb), res(wc), res(wo)],
        out_specs=row(D),
        out_shape=jax.ShapeDtypeStruct((N, D), F32),
        compiler_params=_cparams(("parallel",)),
        name="mix_merge",
    )(h, mod, z, z, z, oa, ys, oc, wa, wb, wc, wo)


def _permute_w_in(w):
    gate0 = 4 * 512 + 512 + 1024 + 6 * 256
    ngate = 3 * NSA_G * NSA_HPG
    main = jnp.concatenate([w[:, gate0 + ngate:], w[:, :gate0]], axis=1)
    gw = w[:, gate0:gate0 + ngate].reshape(w.shape[0], 3, NSA_G, NSA_HPG)
    gw = gw.transpose(0, 2, 1, 3).reshape(w.shape[0], NSA_G, 3 * NSA_HPG)
    gw = jnp.pad(gw, ((0, 0), (0, 0), (0, 128 - 3 * NSA_HPG))).reshape(w.shape[0], NSA_G * 128)
    return main.astype(BF16), gw.astype(BF16)


def _overlap_matrix():
    j = jnp.arange(QT)[:, None]
    n = jnp.arange(QT)[None, :]
    st = n * CMP_STRIDE
    ov = (st < j * SEL_LEN + SEL_LEN) & (st + CMP_LEN > j * SEL_LEN)
    return jnp.where(ov, 1.0, 0.0).astype(BF16)


def kernel(x, c, ada_w, ada_b, norm_g, ffn1_wi, ffn1_wo, ffn2_wi, ffn2_wo, w_in, hg_lb_logits,
           hg_onorm, hg_proj, ssm_a_re, ssm_a_im, ssm_log_dt, ssm_b_re, ssm_b_im, ssm_c_re,
           ssm_c_im, ssm_d, ssm_glu_w, nsa_q_gain, nsa_k_gain, nsa_pe_k, nsa_pe_v, nsa_phi_k,
           nsa_phi_v, nsa_proj, rel_table, w_out):
    B, S, D = x.shape
    L = ada_w.shape[0]
    N = B * S
    assert S % 512 == 0 and S // SEL_LEN == 32 and S // QT == 16
    lb_cum = jnp.cumsum(jax.nn.softmax(hg_lb_logits.astype(F32), axis=0), axis=0)
    lower_bounds = lb_cum - lb_cum[0:1]
    mods = _mods(c, ada_w, ada_b).reshape(L, B, 9, D)
    bias = _bias_tables(rel_table, S)
    ovm = _overlap_matrix()
    h = x.reshape(N, D)
    for l in range(L):
        mod = mods[l]
        h = _ffn(h, mod, norm_g[l, 0:1], ffn1_wi, ffn1_wo, l, 0, S)
        w_main, w_gate = _permute_w_in(w_in[l])
        z, zg = _win(h, mod, norm_g[l, 1:2], w_main, w_gate, S)
        oa = _hgrn(z, lower_bounds[l:l + 1], hg_onorm[l:l + 1], B, S)
        sp = _ssm_params(ssm_a_re[l], ssm_a_im[l], ssm_log_dt[l], ssm_b_re[l], ssm_b_im[l],
                         ssm_c_re[l], ssm_c_im[l])
        ys = _ssm(z, sp, ssm_d[l:l + 1], B, S)
        prep = _nsa_prep(z, nsa_k_gain[l:l + 1], nsa_pe_k[l], nsa_pe_v[l],
                         nsa_phi_k[l].astype(BF16), nsa_phi_v[l].astype(BF16), B, S)
        oc = _nsa(z, zg, prep, bias, nsa_q_gain[l:l + 1], ovm, B, S)
        h = _merge(h, mod, z, oa, ys, oc, hg_proj[l].astype(BF16), ssm_glu_w[l].astype(BF16),
                   nsa_proj[l].astype(BF16), w_out[l].astype(BF16), S)
        h = _ffn(h, mod, norm_g[l, 2:3], ffn2_wi, ffn2_wo, l, 6, S)
    return h.reshape(B, S, D)
```

```python
import functools
import math

import jax
import jax.numpy as jnp
from jax import lax
from jax.experimental import pallas as pl
from jax.experimental.pallas import tpu as pltpu

F32 = jnp.float32
BF16 = jnp.bfloat16

EPS = 1e-6
NEG = -1e30
LOG2E = 1.4426950408889634
FORCE_SCORE = 1e4

HG_HEADS, HG_D, HG_CHUNK = 4, 128, 16
SSM_GROUPS, SSM_P, SSM_N, SSM_CHUNK = 32, 16, 64, 8
NSA_G, NSA_HPG, NSA_DH = 4, 4, 64
CMP_LEN, CMP_STRIDE, SEL_LEN, SEL_TOPN, WIN = 32, 16, 64, 16, 512
REL_BUCKETS, REL_MAX_DIST = 32, 128
QT = 128
NSA_TQ = 512

VMEM_LIMIT = 60 * 1024 * 1024

OFF_ZA, OFF_ZB, OFF_ZC = 0, 2048, 4096
OFF_HQ, OFF_HF, OFF_HI, OFF_HG = 6144, 6656, 7168, 7680
OFF_SU, OFF_NQ = 8192, 8704
OFF_KC, OFF_VC, OFF_KS, OFF_VS, OFF_KW, OFF_VW = 9728, 9984, 10240, 10496, 10752, 11008
Z_WIDTH = 11264


def _cparams(sem):
    return pltpu.CompilerParams(dimension_semantics=sem, vmem_limit_bytes=VMEM_LIMIT)


def _dot(a, b):
    return jnp.dot(a, b, preferred_element_type=F32)


def _dot_nt(a, b):
    return lax.dot_general(a, b, (((1,), (1,)), ((), ())), preferred_element_type=F32)


def _dot_tn(a, b):
    return lax.dot_general(a, b, (((0,), (0,)), ((), ())), preferred_element_type=F32)


def _sigmoid(x):
    return 1.0 / (1.0 + jnp.exp(-x))


def _split3(x):
    hi = x.astype(BF16)
    r = x - hi.astype(F32)
    mid = r.astype(BF16)
    lo = (r - mid.astype(F32)).astype(BF16)
    return hi, mid, lo


def _norm_mod(x, gain, shift, scale):
    ms = jnp.mean(x * x, axis=-1, keepdims=True)
    y = x * lax.rsqrt(ms + EPS) * gain
    return y * (1.0 + scale) + shift


def _mod_kernel(c_ref, w_ref, b_ref, o_ref):
    c = c_ref[...]
    ca = (c * _sigmoid(c)).astype(BF16)
    o_ref[0] = _dot(ca, w_ref[0].astype(BF16)) + b_ref[0]


def _mods(c, ada_w, ada_b):
    L, D, W = ada_w.shape
    B = c.shape[0]
    tn = 1024
    return pl.pallas_call(
        _mod_kernel,
        grid=(L, W // tn),
        in_specs=[pl.BlockSpec((B, D), lambda l, j: (0, 0)),
                  pl.BlockSpec((1, D, tn), lambda l, j: (l, 0, j)),
                  pl.BlockSpec((1, 1, tn), lambda l, j: (l, 0, j))],
        out_specs=pl.BlockSpec((1, B, tn), lambda l, j: (l, 0, j)),
        out_shape=jax.ShapeDtypeStruct((L, B, W), F32),
        compiler_params=_cparams(("parallel", "parallel")),
        name="adaln_mod",
    )(c, ada_w, ada_b.reshape(L, 1, W))


def _ffn_kernel(h_ref, mod_ref, g_ref, wi1_ref, wi2_ref, wo_ref, o_ref, u_s, *, k0, nf):
    f = pl.program_id(1)

    @pl.when(f == 0)
    def _():
        u = _norm_mod(h_ref[...], g_ref[...], mod_ref[0, k0:k0 + 1, :], mod_ref[0, k0 + 1:k0 + 2, :])
        u_s[...] = u.astype(BF16)
        o_ref[...] = jnp.zeros_like(o_ref)

    u = u_s[...]
    a1 = _dot(u, wi1_ref[0].astype(BF16))
    a2 = _dot(u, wi2_ref[0].astype(BF16))
    act = (a1 * _sigmoid(a1) * a2).astype(BF16)
    o_ref[...] += _dot(act, wo_ref[0].astype(BF16))

    @pl.when(f == nf - 1)
    def _():
        o_ref[...] = h_ref[...] + (0.5 * mod_ref[0, k0 + 2:k0 + 3, :]) * o_ref[...]


def _ffn(h, mod, gain, wi, wo, layer, k0, seq):
    N, D = h.shape
    dff = wo.shape[1]
    tm, tf = 1024, 256
    nf = dff // tf
    tpb = seq // tm
    return pl.pallas_call(
        functools.partial(_ffn_kernel, k0=k0, nf=nf),
        grid=(N // tm, nf),
        in_specs=[pl.BlockSpec((tm, D), lambda i, f: (i, 0)),
                  pl.BlockSpec((1, 9, D), lambda i, f: (i // tpb, 0, 0)),
                  pl.BlockSpec((1, D), lambda i, f: (0, 0)),
                  pl.BlockSpec((1, D, tf), lambda i, f: (layer, 0, f)),
                  pl.BlockSpec((1, D, tf), lambda i, f: (layer, 0, f + nf)),
                  pl.BlockSpec((1, tf, D), lambda i, f: (layer, f, 0))],
        out_specs=pl.BlockSpec((tm, D), lambda i, f: (i, 0)),
        out_shape=jax.ShapeDtypeStruct((N, D), F32),
        scratch_shapes=[pltpu.VMEM((tm, D), BF16)],
        compiler_params=_cparams(("parallel", "arbitrary")),
        name="ffn",
    )(h, mod, gain, wi, wi, wo)


def _win_kernel(h_ref, mod_ref, g_ref, w_ref, wg_ref, z_ref, zg_ref, u_s):
    j = pl.program_id(1)

    @pl.when(j == 0)
    def _():
        u = _norm_mod(h_ref[...], g_ref[...], mod_ref[0, 3:4, :], mod_ref[0, 4:5, :])
        ub = u.astype(BF16)
        u_s[...] = ub
        zg_ref[...] = _dot(ub, wg_ref[...])

    z_ref[...] = _dot(u_s[...], w_ref[...]).astype(z_ref.dtype)


def _win(h, mod, gain, w_main, w_gate, seq):
    N, D = h.shape
    tm, tn = 1024, 1024
    tpb = seq // tm
    GW = w_gate.shape[1]
    return pl.pallas_call(
        _win_kernel,
        grid=(N // tm, Z_WIDTH // tn),
        in_specs=[pl.BlockSpec((tm, D), lambda i, j: (i, 0)),
                  pl.BlockSpec((1, 9, D), lambda i, j: (i // tpb, 0, 0)),
                  pl.BlockSpec((1, D), lambda i, j: (0, 0)),
                  pl.BlockSpec((D, tn), lambda i, j: (0, j)),
                  pl.BlockSpec((D, GW), lambda i, j: (0, 0))],
        out_specs=[pl.BlockSpec((tm, tn), lambda i, j: (i, j)),
                   pl.BlockSpec((tm, GW), lambda i, j: (i, 0))],
        out_shape=[jax.ShapeDtypeStruct((N, Z_WIDTH), BF16),
                   jax.ShapeDtypeStruct((N, GW), F32)],
        scratch_shapes=[pltpu.VMEM((tm, D), BF16)],
        compiler_params=_cparams(("parallel", "arbitrary")),
        name="in_proj",
    )(h, mod, gain, w_main, w_gate)


def _hgrn_kernel(q_ref, f_ref, i_ref, g_ref, lb_ref, on_ref, o_ref, st_ref, *, tb):
    C, H, DK = HG_CHUNK, HG_HEADS, HG_D
    nc = tb // C
    W = H * DK

    @pl.when(pl.program_id(1) == 0)
    def _():
        st_ref[...] = jnp.zeros_like(st_ref)

    q = q_ref[...].astype(F32)
    qs = q * _sigmoid(q)
    x = f_ref[...].astype(F32)
    iv = i_ref[...].astype(F32)
    lb = lb_ref[...]
    sp = jnp.log1p(jnp.exp(-jnp.abs(x)))
    lsig = jnp.minimum(x, 0.0) - sp
    a = jnp.log(jnp.maximum(lb, 1e-38))
    bterm = jnp.log1p(-lb) + lsig
    lae = jnp.maximum(a, bterm) + jnp.log1p(jnp.exp(-jnp.abs(a - bterm)))
    log_f = jnp.where(lb > 0.0, lae, bterm)
    k = (1.0 - lb) * jnp.exp(jnp.minimum(-x, 0.0) - sp)

    r = lax.broadcasted_iota(jnp.int32, (tb, tb), 0)
    cidx = lax.broadcasted_iota(jnp.int32, (tb, tb), 1)
    lmat = jnp.where(((r // C) == (cidx // C)) & (cidx <= r), 1.0, 0.0).astype(BF16)
    hi, mid, lo = _split3(log_f)
    b = _dot(lmat, hi) + _dot(lmat, mid) + _dot(lmat, lo)

    qe = (qs * jnp.exp(b)).astype(BF16)
    b3 = b.reshape(nc, C, W)
    bend = b3[:, C - 1:C, :]
    kdec = (k.reshape(nc, C, W) * jnp.exp(bend - b3)).reshape(tb, W).astype(BF16)
    ebend = jnp.exp(bend)
    ib = iv.astype(BF16)
    o_heads = []
    for h in range(H):
        ls = slice(h * DK, (h + 1) * DK)
        uts = [_dot_tn(ib[c * C:(c + 1) * C, ls], kdec[c * C:(c + 1) * C, ls]) for c in range(nc)]
        st = st_ref[h]
        sts = []
        for c in range(nc):
            sts.append(st.astype(BF16))
            st = st * ebend[c][:, ls] + uts[c]
        st_ref[h] = st
        o_heads.append(jnp.concatenate(
            [_dot_nt(qe[c * C:(c + 1) * C, ls], sts[c]) for c in range(nc)], axis=0))
    o = jnp.concatenate(o_heads, axis=1)

    b2 = b * LOG2E
    tmod = lax.broadcasted_iota(jnp.int32, (tb, W), 0) % C
    rr = lax.broadcasted_iota(jnp.int32, (W, W), 0) // DK
    cc = lax.broadcasted_iota(jnp.int32, (W, W), 1) // DK
    ones_bd = jnp.where(rr == cc, 1.0, 0.0).astype(BF16)
    for d in range(C):
        if d == 0:
            p = qs * k
            isd = iv
        else:
            kd = pltpu.roll(k, d, 0)
            bd = pltpu.roll(b2, d, 0)
            isd = pltpu.roll(iv, d, 0)
            p = jnp.where(tmod >= d, qs * kd * jnp.exp2(b2 - bd), 0.0)
        o = o + _dot(p.astype(BF16), ones_bd) * isd

    g = g_ref[...].astype(F32)
    gs = g * _sigmoid(g)
    onw = on_ref[...]
    outs = []
    for h in range(H):
        ls = slice(h * DK, (h + 1) * DK)
        oh = o[:, ls]
        outs.append(oh * lax.rsqrt(jnp.mean(oh * oh, axis=-1, keepdims=True) + EPS) * onw * gs[:, ls])
    o_ref[...] = jnp.concatenate(outs, axis=1).astype(o_ref.dtype)


def _hgrn(z, lb, onorm, batch, seq):
    N = z.shape[0]
    tb = 256
    nb = seq // tb
    W = HG_HEADS * HG_D

    def zspec(off):
        return pl.BlockSpec((tb, W), lambda b, j: (b * nb + j, off // W))

    return pl.pallas_call(
        functools.partial(_hgrn_kernel, tb=tb),
        grid=(batch, nb),
        in_specs=[zspec(OFF_HQ), zspec(OFF_HF), zspec(OFF_HI), zspec(OFF_HG),
                  pl.BlockSpec((1, W), lambda b, j: (0, 0)),
                  pl.BlockSpec((1, HG_D), lambda b, j: (0, 0))],
        out_specs=pl.BlockSpec((tb, W), lambda b, j: (b * nb + j, 0)),
        out_shape=jax.ShapeDtypeStruct((N, W), BF16),
        scratch_shapes=[pltpu.VMEM((HG_HEADS, HG_D, HG_D), F32)],
        compiler_params=_cparams(("parallel", "arbitrary")),
        name="hgrn2",
    )(z, z, z, z, lb, onorm)


def _ssm_kernel(u_ref, bm_ref, cr_ref, ci_ref, kt_ref, p1r_ref, p1i_ref, p2r_ref, p2i_ref,
                alr_ref, ali_ref, d_ref, o_ref, xr_ref, xi_ref, xpr_ref, xpi_ref, *, tt):
    LC = SSM_CHUNK
    nc = tt // LC
    W = SSM_GROUPS * SSM_N

    @pl.when(pl.program_id(1) == 0)
    def _():
        xr_ref[...] = jnp.zeros_like(xr_ref)
        xi_ref[...] = jnp.zeros_like(xi_ref)

    ub = u_ref[...].astype(BF16)
    u = ub.astype(F32)
    bu = _dot(ub, bm_ref[...])
    bur = bu[:, :W].reshape(nc, LC, W)
    bui = bu[:, W:].reshape(nc, LC, W)
    p1r, p1i = p1r_ref[...], p1i_ref[...]
    vr = jnp.sum(bur * p1r - bui * p1i, axis=1)
    vi = jnp.sum(bur * p1i + bui * p1r, axis=1)

    alr, ali = alr_ref[...], ali_ref[...]
    xr, xi = xr_ref[...], xi_ref[...]
    for c in range(nc):
        xpr_ref[c:c + 1, :] = xr
        xpi_ref[c:c + 1, :] = xi
        nr = alr * xr - ali * xi + vr[c:c + 1, :]
        ni = alr * xi + ali * xr + vi[c:c + 1, :]
        xr, xi = nr, ni
    xr_ref[...] = xr
    xi_ref[...] = xi

    xpr = xpr_ref[...][:, None, :]
    xpi = xpi_ref[...][:, None, :]
    p2r, p2i = p2r_ref[...], p2i_ref[...]
    zr = (p2r * xpr - p2i * xpi).reshape(tt, W).astype(BF16)
    zi = (p2r * xpi + p2i * xpr).reshape(tt, W).astype(BF16)
    y = _dot(zr, cr_ref[...]) + _dot(zi, ci_ref[...])

    tmod = lax.broadcasted_iota(jnp.int32, u.shape, 0) % LC
    for tau in range(LC):
        if tau == 0:
            us = ub
        else:
            us = jnp.where(tmod >= tau, pltpu.roll(u, tau, 0), 0.0).astype(BF16)
        y = y + _dot(us, kt_ref[tau])
    y = y + d_ref[...] * u
    g = 0.5 * y * (1.0 + jnp.tanh(0.7978845608028654 * (y + 0.044715 * (y * y * y))))
    o_ref[...] = g.astype(o_ref.dtype)


def _ssm_params(a_re, a_im, log_dt, b_re, b_im, c_re, c_im):
    G, P, N, LC = SSM_GROUPS, SSM_P, SSM_N, SSM_CHUNK
    hp = lax.Precision.HIGHEST
    a_re = jnp.minimum(a_re.astype(F32), -1e-4)
    a_im = a_im.astype(F32)
    dt = jnp.exp(log_dt.astype(F32))[:, None]
    mag = jnp.exp(dt * a_re)
    ab_re, ab_im = mag * jnp.cos(dt * a_im), mag * jnp.sin(dt * a_im)
    den = a_re * a_re + a_im * a_im
    nr = ab_re - 1.0
    z_re = (nr * a_re + ab_im * a_im) / den
    z_im = (ab_im * a_re - nr * a_im) / den
    b_re, b_im = b_re.astype(F32), b_im.astype(F32)
    bb_re = z_re[..., None] * b_re - z_im[..., None] * b_im
    bb_im = z_re[..., None] * b_im + z_im[..., None] * b_re
    kk = jnp.arange(LC + 1, dtype=F32)[:, None, None]
    pm = jnp.exp(kk * dt * a_re)
    pw_re, pw_im = pm * jnp.cos(kk * dt * a_im), pm * jnp.sin(kk * dt * a_im)
    eye = jnp.eye(G, dtype=F32)
    bm_re = jnp.einsum('gnp,gh->gphn', bb_re, eye).reshape(G * P, G * N)
    bm_im = jnp.einsum('gnp,gh->gphn', bb_im, eye).reshape(G * P, G * N)
    bmat = jnp.concatenate([bm_re, bm_im], axis=1)
    c_re, c_im = c_re.astype(F32), c_im.astype(F32)
    cr = jnp.einsum('gpn,gh->gnhp', c_re, eye).reshape(G * N, G * P)
    ci = -jnp.einsum('gpn,gh->gnhp', c_im, eye).reshape(G * N, G * P)
    t_re = pw_re[:LC, :, :, None] * bb_re[None] - pw_im[:LC, :, :, None] * bb_im[None]
    t_im = pw_re[:LC, :, :, None] * bb_im[None] + pw_im[:LC, :, :, None] * bb_re[None]
    kt = (jnp.einsum('gqn,tgnp->tgpq', c_re, t_re, precision=hp)
          - jnp.einsum('gqn,tgnp->tgpq', c_im, t_im, precision=hp))
    ktm = jnp.einsum('tgpq,gh->tgphq', kt, eye).reshape(LC, G * P, G * P)
    flat = lambda x: x.reshape(x.shape[0], G * N)
    p1r, p1i = flat(pw_re[LC - 1::-1][:LC]), flat(pw_im[LC - 1::-1][:LC])
    p2r, p2i = flat(pw_re[1:LC + 1]), flat(pw_im[1:LC + 1])
    alr, ali = flat(pw_re[LC:LC + 1]), flat(pw_im[LC:LC + 1])
    return (bmat.astype(BF16), cr.astype(BF16), ci.astype(BF16), ktm.astype(BF16),
            p1r, p1i, p2r, p2i, alr, ali)


def _ssm(z, params, d_skip, batch, seq):
    N = z.shape[0]
    tt = 256
    nb = seq // tt
    CW = SSM_GROUPS * SSM_P
    W = SSM_GROUPS * SSM_N
    LC = SSM_CHUNK
    bmat, cr, ci, ktm, p1r, p1i, p2r, p2i, alr, ali = params
    full = lambda a: pl.BlockSpec(a.shape, lambda b, j, _n=a.ndim: (0,) * _n)
    return pl.pallas_call(
        functools.partial(_ssm_kernel, tt=tt),
        grid=(batch, nb),
        in_specs=[pl.BlockSpec((tt, CW), lambda b, j: (b * nb + j, OFF_SU // CW)),
                  full(bmat), full(cr), full(ci), full(ktm), full(p1r), full(p1i),
                  full(p2r), full(p2i), full(alr), full(ali),
                  pl.BlockSpec((1, CW), lambda b, j: (0, 0))],
        out_specs=pl.BlockSpec((tt, CW), lambda b, j: (b * nb + j, 0)),
        out_shape=jax.ShapeDtypeStruct((N, CW), BF16),
        scratch_shapes=[pltpu.VMEM((1, W), F32), pltpu.VMEM((1, W), F32),
                        pltpu.VMEM((tt // LC, W), F32), pltpu.VMEM((tt // LC, W), F32)],
        compiler_params=_cparams(("parallel", "arbitrary")),
        name="s5_ssm",
    )(z, bmat, cr, ci, ktm, p1r, p1i, p2r, p2i, alr, ali, d_skip)


def _t5_bucket(dist):
    n = jnp.maximum(dist, 0)
    max_exact = REL_BUCKETS // 2
    nf = jnp.maximum(n, 1).astype(F32)
    large = max_exact + (jnp.log(nf / max_exact) / math.log(REL_MAX_DIST / max_exact)
                         * (REL_BUCKETS - max_exact)).astype(jnp.int32)
    large = jnp.minimum(large, REL_BUCKETS - 1)
    return jnp.where(n < max_exact, n, large)


def _bias_kernel(tab_ref, bk_ref, o_ref):
    h = pl.program_id(0)
    bk = bk_ref[...]
    base = tab_ref[REL_BUCKETS - 1, h]
    acc = jnp.full(bk.shape, NEG, F32)
    for k in range(REL_BUCKETS):
        acc = jnp.where(bk == k, (tab_ref[k, h] - base) * LOG2E, acc)
    o_ref[0] = acc


def _bias_tables(rel_table, seq):
    i = jnp.arange(QT)[:, None]
    j = jnp.arange(QT)[None, :]
    d0 = i - j
    b0 = jnp.where(d0 >= 0, _t5_bucket(d0), REL_BUCKETS)
    b1 = _t5_bucket(QT + i - j)
    t = jnp.arange(seq)[:, None]
    n = jnp.arange(QT)[None, :]
    n_cmp = (seq - CMP_LEN) // CMP_STRIDE + 1
    dc = t - (n * CMP_STRIDE + CMP_LEN - 1)
    bc = jnp.where((dc >= 0) & (n < n_cmp), _t5_bucket(dc), REL_BUCKETS)
    bk = jnp.concatenate([b0, b1, bc], axis=0).astype(jnp.int32)
    R = bk.shape[0]
    nh = rel_table.shape[1]
    return pl.pallas_call(
        _bias_kernel,
        grid=(nh,),
        in_specs=[pl.BlockSpec(memory_space=pltpu.SMEM),
                  pl.BlockSpec((R, QT), lambda h: (0, 0))],
        out_specs=pl.BlockSpec((1, R, QT), lambda h: (h, 0, 0)),
        out_shape=jax.ShapeDtypeStruct((nh, R, QT), F32),
        compiler_params=_cparams(("arbitrary",)),
        name="nsa_bias_tables",
    )(rel_table.astype(F32), bk)


def _head_rms(x, gain):
    return x * lax.rsqrt(jnp.mean(x * x, axis=-1, keepdims=True) + EPS) * gain


def _nsa_prep_kernel(kc_ref, vc_ref, ks_ref, vs_ref, kw_ref, vw_ref, kg_ref, pek_ref, pev_ref,
                     phik_ref, phiv_ref, oks_ref, ovs_ref, okw_ref, ovw_ref, okc_ref, ovc_ref,
                     xpad_ref, *, seq):
    DH = NSA_DH
    kg = kg_ref[...]
    srow = lax.broadcasted_iota(jnp.int32, (seq, DH), 0)
    lane = lax.broadcasted_iota(jnp.int32, (seq, DH), 1)
    onehot = jnp.where(lane == srow // SEL_LEN, 1.0, 0.0).astype(BF16)
    ones = jnp.ones((seq, DH), BF16)
    ones_c = jnp.ones((QT, DH), BF16)
    xpad_ref[seq:seq + CMP_LEN, :] = jnp.zeros((CMP_LEN, DH), F32)
    ksf, kwf = ks_ref[...].astype(F32), kw_ref[...].astype(F32)
    vsf, vwf = vs_ref[...].astype(F32), vw_ref[...].astype(F32)
    kcf, vcf = kc_ref[...].astype(F32), vc_ref[...].astype(F32)
    for g in range(NSA_G):
        ls = slice(g * DH, (g + 1) * DH)
        ksn = _head_rms(ksf[:, ls], kg).astype(BF16)
        oks_ref[0, g] = jnp.concatenate([ksn, onehot], axis=1)
        okw_ref[0, g, 0:WIN, :] = jnp.zeros((WIN, DH), BF16)
        okw_ref[0, g, WIN:WIN + seq, :] = _head_rms(kwf[:, ls], kg).astype(BF16)
        ovs_ref[0, g] = jnp.concatenate([vsf[:, ls].astype(BF16), ones], axis=1)
        ovw_ref[0, g, 0:WIN, :] = jnp.zeros((WIN, 2 * DH), BF16)
        ovw_ref[0, g, WIN:WIN + seq, :] = jnp.concatenate([vwf[:, ls].astype(BF16), ones], axis=1)
        for src, pe_ref, phi_ref, is_k in ((kcf, pek_ref, phik_ref, True),
                                           (vcf, pev_ref, phiv_ref, False)):
            xpad_ref[0:seq, :] = src[:, ls]
            acc = jnp.zeros((QT, DH), F32)
            for l in range(CMP_LEN):
                xl = xpad_ref[pl.ds(l, QT, stride=CMP_STRIDE), :] + pe_ref[l:l + 1, :]
                acc = acc + _dot(xl.astype(BF16), phi_ref[l])
            if is_k:
                okc_ref[0, g] = _head_rms(acc, kg).astype(BF16)
            else:
                ovc_ref[0, g] = jnp.concatenate([acc.astype(BF16), ones_c], axis=1)


def _nsa_prep(z, k_gain, pe_k, pe_v, phi_k, phi_v, batch, seq):
    G, DH = NSA_G, NSA_DH
    KW = G * DH

    def zspec(off):
        return pl.BlockSpec((seq, KW), lambda b: (b, off // KW))

    full = lambda a: pl.BlockSpec(a.shape, lambda b, _n=a.ndim: (0,) * _n)

    def ospec(rows, w):
        return pl.BlockSpec((1, G, rows, w), lambda b: (b, 0, 0, 0))

    def oshape(rows, w):
        return jax.ShapeDtypeStruct((batch, G, rows, w), BF16)

    return pl.pallas_call(
        functools.partial(_nsa_prep_kernel, seq=seq),
        grid=(batch,),
        in_specs=[zspec(OFF_KC), zspec(OFF_VC), zspec(OFF_KS), zspec(OFF_VS), zspec(OFF_KW),
                  zspec(OFF_VW), full(k_gain), full(pe_k), full(pe_v), full(phi_k), full(phi_v)],
        out_specs=[ospec(seq, 2 * DH), ospec(seq, 2 * DH), ospec(seq + WIN, DH),
                   ospec(seq + WIN, 2 * DH), ospec(QT, DH), ospec(QT, 2 * DH)],
        out_shape=[oshape(seq, 2 * DH), oshape(seq, 2 * DH), oshape(seq + WIN, DH),
                   oshape(seq + WIN, 2 * DH), oshape(QT, DH), oshape(QT, 2 * DH)],
        scratch_shapes=[pltpu.VMEM((seq + CMP_LEN, DH), F32)],
        compiler_params=_cparams(("parallel",)),
        name="nsa_kv_prep",
    )(z, z, z, z, z, z, k_gain, pe_k, pe_v, phi_k, phi_v)


def _nsa_kernel(q_ref, gt_ref, ks_ref, vs_ref, kw_ref, vw_ref, kc_ref, vc_ref, bc_ref, nb_ref,
                qg_ref, ov_ref, o_ref, slog_ref, dlog_ref, mrun_ref, mb_ref, acc_ref):
    DH, HPG, TQ = NSA_DH, NSA_HPG, NSA_TQ
    NR = TQ // QT
    RB = HPG * QT
    R = NR * RB
    NSEL = 32
    qi = pl.program_id(1)
    t0 = qi * TQ

    def rows(r):
        return slice(r * RB, (r + 1) * RB)

    def to_rows(per_head):
        return jnp.concatenate([per_head[h][r * QT:(r + 1) * QT] for r in range(NR) for h in range(HPG)],
                               axis=0)

    def lanes(x, c):
        return x[:, c * QT:(c + 1) * QT]

    qblk = q_ref[...].astype(F32)
    qg = qg_ref[...] * (DH ** -0.5 * LOG2E)
    q4b = to_rows([_head_rms(qblk[:, h * DH:(h + 1) * DH], qg) for h in range(HPG)]).astype(BF16)
    nb0 = nb_ref[:, 0].reshape(RB, QT)
    nb1 = nb_ref[:, 1].reshape(RB, QT)

    sc = _dot_nt(q4b, kc_ref[0, 0]) + to_rows([bc_ref[h] for h in range(HPG)])
    mc = jnp.max(sc, axis=-1, keepdims=True)
    ec = jnp.where(sc > 0.5 * NEG, jnp.exp2(sc - mc), 0.0)
    lc = jnp.sum(ec, axis=-1, keepdims=True)
    pc = ec / jnp.where(lc > 0.0, lc, 1.0)
    o_cmp = _dot(pc.astype(BF16), vc_ref[0, 0])[:, :DH]

    psum = jnp.concatenate(
        [sum(pc[r * RB + h * QT:r * RB + (h + 1) * QT] for h in range(HPG)) for r in range(NR)], axis=0)
    ovm = ov_ref[...]
    hi, mid, lo = _split3(psum)
    imp_t = (_dot_nt(ovm, hi) + _dot_nt(ovm, mid) + _dot_nt(ovm, lo))[0:NSEL, :]
    jrow = lax.broadcasted_iota(jnp.int32, (NSEL, TQ), 0)
    blk = (t0 + lax.broadcasted_iota(jnp.int32, (NSEL, TQ), 1)) // SEL_LEN
    forced = (jrow == 0) | (jrow == blk) | (jrow == blk - 1)
    imp_t = jnp.where(forced, FORCE_SCORE, imp_t)
    imp_t = jnp.where(jrow <= blk, imp_t, NEG)
    cnt = jnp.zeros((NSEL, TQ), F32)
    for jp in range(NSEL):
        rowv = imp_t[jp:jp + 1, :]
        beats = (rowv > imp_t) | ((rowv == imp_t) & (jrow > jp))
        cnt = cnt + jnp.where(beats, 1.0, 0.0)
    selb_t = jnp.where(cnt < float(SEL_TOPN), 0.0, NEG)
    selb = jnp.concatenate([selb_t, jnp.zeros((QT - NSEL, TQ), F32)], axis=0).T
    selb4 = to_rows([selb[:, :DH]] * HPG).astype(BF16)
    qaug = jnp.concatenate([q4b, selb4], axis=1)

    mrun_ref[...] = jnp.full((R, QT), NEG, F32)

    def tile(ref, start, size):
        return ref[0, 0, pl.ds(pl.multiple_of(start, QT), size), :]

    def track_max(s):
        m = mrun_ref[...]
        for c in range(TQ // QT):
            m = jnp.maximum(m, lanes(s, c))
        mrun_ref[...] = m

    def sel_far(kt, carry):
        s = _dot_nt(qaug, tile(ks_ref, kt * TQ, TQ))
        slog_ref[kt] = s
        track_max(s)
        return carry

    lax.fori_loop(0, jnp.maximum(qi - 1, 0), sel_far, 0)

    @pl.when(qi >= 1)
    def _():
        s = _dot_nt(qaug, tile(ks_ref, (qi - 1) * TQ, TQ))
        top = s[0:RB]
        top = jnp.concatenate([top[:, :TQ - QT], top[:, TQ - QT:] + nb1], axis=1)
        s = jnp.concatenate([top, s[RB:]], axis=0)
        slog_ref[qi - 1] = s
        track_max(s)

    for r in range(NR):
        s = _dot_nt(qaug[rows(r)], tile(ks_ref, t0, (r + 1) * QT))
        parts = [lanes(s, c) for c in range(r + 1)]
        parts[r] = parts[r] + nb0
        if r >= 1:
            parts[r - 1] = parts[r - 1] + nb1
        m = mrun_ref[rows(r), :]
        for c in range(r + 1):
            dlog_ref[rows(r), c * QT:(c + 1) * QT] = parts[c]
            m = jnp.maximum(m, parts[c])
        mrun_ref[rows(r), :] = m
    mb_ref[...] = jnp.broadcast_to(jnp.max(mrun_ref[...], axis=-1, keepdims=True), (R, QT))

    acc_ref[...] = jnp.zeros_like(acc_ref)

    def sel_pv(kt, carry):
        mb = mb_ref[...]
        s = slog_ref[kt]
        p = jnp.concatenate([jnp.exp2(lanes(s, c) - mb) for c in range(TQ // QT)], axis=1)
        acc_ref[...] += _dot(p.astype(BF16), tile(vs_ref, kt * TQ, TQ))
        return carry

    lax.fori_loop(0, qi, sel_pv, 0)
    for r in range(NR):
        mb = mb_ref[rows(r), :]
        p = jnp.concatenate([jnp.exp2(dlog_ref[rows(r), c * QT:(c + 1) * QT] - mb)
                             for c in range(r + 1)], axis=1)
        acc_ref[rows(r), :] += _dot(p.astype(BF16), tile(vs_ref, t0, (r + 1) * QT))
    acc_s = acc_ref[...]

    ii = lax.broadcasted_iota(jnp.int32, (RB, QT), 0) % QT
    jj = lax.broadcasted_iota(jnp.int32, (RB, QT), 1)
    nwt = WIN // QT
    acc_w = []
    for r in range(NR):
        start = t0 + r * QT
        s = _dot_nt(q4b[rows(r)], tile(kw_ref, start, WIN + QT))
        parts = [lanes(s, c) for c in range(nwt + 1)]
        parts[nwt] = parts[nwt] + nb0
        parts[nwt - 1] = parts[nwt - 1] + nb1
        parts[0] = jnp.where(jj > ii, parts[0], NEG)
        for c in range(nwt - r):
            parts[c] = jnp.where(qi > 0, parts[c], NEG)
        m = parts[0]
        for c in range(1, nwt + 1):
            m = jnp.maximum(m, parts[c])
        m = jnp.max(m, axis=-1, keepdims=True)
        p = jnp.concatenate([jnp.exp2(parts[c] - m) for c in range(nwt + 1)], axis=1)
        acc_w.append(_dot(p.astype(BF16), tile(vw_ref, start, WIN + QT)))
    acc_w = jnp.concatenate(acc_w, axis=0)

    gt = _sigmoid(gt_ref[...])

    def gate_col(br):
        return to_rows([gt[:, br * HPG + h:br * HPG + h + 1] for h in range(HPG)])

    def inv_l(acc):
        l = acc[:, DH:DH + 1]
        return 1.0 / jnp.where(l > 0.0, l, 1.0)

    o = (gate_col(0) * o_cmp + (gate_col(1) * inv_l(acc_s)) * acc_s[:, :DH]
         + (gate_col(2) * inv_l(acc_w)) * acc_w[:, :DH])
    o_ref[...] = jnp.concatenate(
        [jnp.concatenate([o[r * RB + h * QT:r * RB + (h + 1) * QT] for r in range(NR)], axis=0)
         for h in range(HPG)], axis=1).astype(o_ref.dtype)


def _nsa(z, zg, prep, bias, q_gain, ovm, batch, seq):
    N = z.shape[0]
    G, HPG, DH, TQ = NSA_G, NSA_HPG, NSA_DH, NSA_TQ
    nq = seq // TQ
    R = HPG * TQ
    oks, ovs, okw, ovw, okc, ovc = prep
    QW = HPG * DH
    nbias = bias[:, :2 * QT].reshape(-1, 2, QT, QT)
    cbias = bias[:, 2 * QT:]

    def kvspec(a):
        return pl.BlockSpec((1, 1) + a.shape[2:], lambda p, i: (p // G, p % G, 0, 0))

    return pl.pallas_call(
        _nsa_kernel,
        grid=(batch * G, nq),
        in_specs=[pl.BlockSpec((TQ, QW), lambda p, i: ((p // G) * nq + i, OFF_NQ // QW + p % G)),
                  pl.BlockSpec((TQ, 128), lambda p, i: ((p // G) * nq + i, p % G)),
                  kvspec(oks), kvspec(ovs), kvspec(okw), kvspec(ovw), kvspec(okc), kvspec(ovc),
                  pl.BlockSpec((HPG, TQ, QT), lambda p, i: (p % G, i, 0)),
                  pl.BlockSpec((HPG, 2, QT, QT), lambda p, i: (p % G, 0, 0, 0)),
                  pl.BlockSpec((1, DH), lambda p, i: (0, 0)),
                  pl.BlockSpec((QT, QT), lambda p, i: (0, 0))],
        out_specs=pl.BlockSpec((TQ, QW), lambda p, i: ((p // G) * nq + i, p % G)),
        out_shape=jax.ShapeDtypeStruct((N, G * QW), BF16),
        scratch_shapes=[pltpu.VMEM((nq - 1, R, TQ), F32), pltpu.VMEM((R, TQ), F32),
                        pltpu.VMEM((R, QT), F32), pltpu.VMEM((R, QT), F32),
                        pltpu.VMEM((R, 2 * DH), F32)],
        compiler_params=_cparams(("parallel", "arbitrary")),
        name="nsa_attention",
    )(z, zg, oks, ovs, okw, ovw, okc, ovc, cbias, nbias, q_gain, ovm)


def _merge_kernel(h_ref, mod_ref, za_ref, zb_ref, zc_ref, oa_ref, ys_ref, oc_ref,
                  wa_ref, wb_ref, wc_ref, wo_ref, o_ref):
    D = h_ref.shape[1]
    ya = _dot(oa_ref[...], wa_ref[...])
    zz = _dot(ys_ref[...], wb_ref[...])
    yb = zz[:, :D] * _sigmoid(zz[:, D:])
    yc = _dot(oc_ref[...], wc_ref[...])
    sg = lambda ref: _sigmoid(ref[...].astype(F32))
    merged = sg(za_ref) * ya + sg(zb_ref) * yb + sg(zc_ref) * yc
    o_ref[...] = h_ref[...] + mod_ref[0, 5:6, :] * _dot(merged.astype(BF16), wo_ref[...])


def _merge(h, mod, z, oa, ys, oc, wa, wb, wc, wo, seq):
    N, D = h.shape
    tm = 256
    tpb = seq // tm
    row = lambda w: pl.BlockSpec((tm, w), lambda i: (i, 0))
    res = lambda a: pl.BlockSpec(a.shape, lambda i: (0, 0), pipeline_mode=pl.Buffered(1))
    return pl.pallas_call(
        _merge_kernel,
        grid=(N // tm,),
        in_specs=[row(D),
                  pl.BlockSpec((1, 9, D), lambda i: (i // tpb, 0, 0)),
                  pl.BlockSpec((tm, D), lambda i: (i, OFF_ZA // D)),
                  pl.BlockSpec((tm, D), lambda i: (i, OFF_ZB // D)),
                  pl.BlockSpec((tm, D), lambda i: (i, OFF_ZC // D)),
                  row(oa.shape[1]), row(ys.shape[1]), row(oc.shape[1]),
                  res(wa), res(wb), res(wc), res(wo)],
        out_specs=row(D),
        out_shape=jax.ShapeDtypeStruct((N, D), F32),
        compiler_params=_cparams(("parallel",)),
        name="mix_merge",
    )(h, mod, z, z, z, oa, ys, oc, wa, wb, wc, wo)


def _permute_w_in(w):
    gate0 = 4 * 512 + 512 + 1024 + 6 * 256
    ngate = 3 * NSA_G * NSA_HPG
    main = jnp.concatenate([w[:, gate0 + ngate:], w[:, :gate0]], axis=1)
    gw = w[:, gate0:gate0 + ngate].reshape(w.shape[0], 3, NSA_G, NSA_HPG)
    gw = gw.transpose(0, 2, 1, 3).reshape(w.shape[0], NSA_G, 3 * NSA_HPG)
    gw = jnp.pad(gw, ((0, 0), (0, 0), (0, 128 - 3 * NSA_HPG))).reshape(w.shape[0], NSA_G * 128)
    return main.astype(BF16), gw.astype(BF16)


def _overlap_matrix():
    j = jnp.arange(QT)[:, None]
    n = jnp.arange(QT)[None, :]
    st = n * CMP_STRIDE
    ov = (st < j * SEL_LEN + SEL_LEN) & (st + CMP_LEN > j * SEL_LEN)
    return jnp.where(ov, 1.0, 0.0).astype(BF16)


def kernel(x, c, ada_w, ada_b, norm_g, ffn1_wi, ffn1_wo, ffn2_wi, ffn2_wo, w_in, hg_lb_logits,
           hg_onorm, hg_proj, ssm_a_re, ssm_a_im, ssm_log_dt, ssm_b_re, ssm_b_im, ssm_c_re,
           ssm_c_im, ssm_d, ssm_glu_w, nsa_q_gain, nsa_k_gain, nsa_pe_k, nsa_pe_v, nsa_phi_k,
           nsa_phi_v, nsa_proj, rel_table, w_out):
    B, S, D = x.shape
    L = ada_w.shape[0]
    N = B * S
    assert S % 512 == 0 and S // SEL_LEN == 32 and S // QT == 16
    lb_cum = jnp.cumsum(jax.nn.softmax(hg_lb_logits.astype(F32), axis=0), axis=0)
    lower_bounds = lb_cum - lb_cum[0:1]
    mods = _mods(c, ada_w, ada_b).reshape(L, B, 9, D)
    bias = _bias_tables(rel_table, S)
    ovm = _overlap_matrix()
    h = x.reshape(N, D)
    for l in range(L):
        mod = mods[l]
        h = _ffn(h, mod, norm_g[l, 0:1], ffn1_wi, ffn1_wo, l, 0, S)
        w_main, w_gate = _permute_w_in(w_in[l])
        z, zg = _win(h, mod, norm_g[l, 1:2], w_main, w_gate, S)
        oa = _hgrn(z, lower_bounds[l:l + 1], hg_onorm[l:l + 1], B, S)
        sp = _ssm_params(ssm_a_re[l], ssm_a_im[l], ssm_log_dt[l], ssm_b_re[l], ssm_b_im[l],
                         ssm_c_re[l], ssm_c_im[l])
        ys = _ssm(z, sp, ssm_d[l:l + 1], B, S)
        prep = _nsa_prep(z, nsa_k_gain[l:l + 1], nsa_pe_k[l], nsa_pe_v[l],
                         nsa_phi_k[l].astype(BF16), nsa_phi_v[l].astype(BF16), B, S)
        oc = _nsa(z, zg, prep, bias, nsa_q_gain[l:l + 1], ovm, B, S)
        h = _merge(h, mod, z, oa, ys, oc, hg_proj[l].astype(BF16), ssm_glu_w[l].astype(BF16),
                   nsa_proj[l].astype(BF16), w_out[l].astype(BF16), S)
        h = _ffn(h, mod, norm_g[l, 2:3], ffn2_wi, ffn2_wo, l, 6, S)
    return h.reshape(B, S, D)
```

```python
import functools
import math

import jax
import jax.numpy as jnp
from jax import lax
from jax.experimental import pallas as pl
from jax.experimental.pallas import tpu as pltpu

F32 = jnp.float32
BF16 = jnp.bfloat16

EPS = 1e-6
NEG = -1e30
LOG2E = 1.4426950408889634
FORCE_SCORE = 1e4

HG_HEADS, HG_D, HG_CHUNK = 4, 128, 16
SSM_GROUPS, SSM_P, SSM_N, SSM_CHUNK = 32, 16, 64, 8
NSA_G, NSA_HPG, NSA_DH = 4, 4, 64
CMP_LEN, CMP_STRIDE, SEL_LEN, SEL_TOPN, WIN = 32, 16, 64, 16, 512
REL_BUCKETS, REL_MAX_DIST = 32, 128
QT = 128
NSA_TQ = 512

VMEM_LIMIT = 60 * 1024 * 1024

OFF_ZA, OFF_ZB, OFF_ZC = 0, 2048, 4096
OFF_HQ, OFF_HF, OFF_HI, OFF_HG = 6144, 6656, 7168, 7680
OFF_SU, OFF_NQ = 8192, 8704
OFF_KC, OFF_VC, OFF_KS, OFF_VS, OFF_KW, OFF_VW = 9728, 9984, 10240, 10496, 10752, 11008
Z_WIDTH = 11264


def _cparams(sem):
    return pltpu.CompilerParams(dimension_semantics=sem, vmem_limit_bytes=VMEM_LIMIT)


def _dot(a, b):
    return jnp.dot(a, b, preferred_element_type=F32)


def _dot_nt(a, b):
    return lax.dot_general(a, b, (((1,), (1,)), ((), ())), preferred_element_type=F32)


def _dot_tn(a, b):
    return lax.dot_general(a, b, (((0,), (0,)), ((), ())), preferred_element_type=F32)


def _sigmoid(x):
    return 1.0 / (1.0 + jnp.exp(-x))


def _split3(x):
    hi = x.astype(BF16)
    r = x - hi.astype(F32)
    mid = r.astype(BF16)
    lo = (r - mid.astype(F32)).astype(BF16)
    return hi, mid, lo


def _norm_mod(x, gain, shift, scale):
    ms = jnp.mean(x * x, axis=-1, keepdims=True)
    y = x * lax.rsqrt(ms + EPS) * gain
    return y * (1.0 + scale) + shift


def _mod_kernel(c_ref, w_ref, b_ref, o_ref):
    c = c_ref[...]
    ca = (c * _sigmoid(c)).astype(BF16)
    o_ref[0] = _dot(ca, w_ref[0].astype(BF16)) + b_ref[0]


def _mods(c, ada_w, ada_b):
    L, D, W = ada_w.shape
    B = c.shape[0]
    tn = 1024
    return pl.pallas_call(
        _mod_kernel,
        grid=(L, W // tn),
        in_specs=[pl.BlockSpec((B, D), lambda l, j: (0, 0)),
                  pl.BlockSpec((1, D, tn), lambda l, j: (l, 0, j)),
                  pl.BlockSpec((1, 1, tn), lambda l, j: (l, 0, j))],
        out_specs=pl.BlockSpec((1, B, tn), lambda l, j: (l, 0, j)),
        out_shape=jax.ShapeDtypeStruct((L, B, W), F32),
        compiler_params=_cparams(("parallel", "parallel")),
        name="adaln_mod",
    )(c, ada_w, ada_b.reshape(L, 1, W))


def _ffn_kernel(h_ref, mod_ref, g_ref, wi1_ref, wi2_ref, wo_ref, o_ref, u_s, *, k0, nf):
    f = pl.program_id(1)

    @pl.when(f == 0)
    def _():
        u = _norm_mod(h_ref[...], g_ref[...], mod_ref[0, k0:k0 + 1, :], mod_ref[0, k0 + 1:k0 + 2, :])
        u_s[...] = u.astype(BF16)
        o_ref[...] = jnp.zeros_like(o_ref)

    u = u_s[...]
    a1 = _dot(u, wi1_ref[0].astype(BF16))
    a2 = _dot(u, wi2_ref[0].astype(BF16))
    act = (a1 * _sigmoid(a1) * a2).astype(BF16)
    o_ref[...] += _dot(act, wo_ref[0].astype(BF16))

    @pl.when(f == nf - 1)
    def _():
        o_ref[...] = h_ref[...] + (0.5 * mod_ref[0, k0 + 2:k0 + 3, :]) * o_ref[...]


def _ffn(h, mod, gain, wi, wo, layer, k0, seq):
    N, D = h.shape
    dff = wo.shape[1]
    tm, tf = 1024, 256
    nf = dff // tf
    tpb = seq // tm
    return pl.pallas_call(
        functools.partial(_ffn_kernel, k0=k0, nf=nf),
        grid=(N // tm, nf),
        in_specs=[pl.BlockSpec((tm, D), lambda i, f: (i, 0)),
                  pl.BlockSpec((1, 9, D), lambda i, f: (i // tpb, 0, 0)),
                  pl.BlockSpec((1, D), lambda i, f: (0, 0)),
                  pl.BlockSpec((1, D, tf), lambda i, f: (layer, 0, f)),
                  pl.BlockSpec((1, D, tf), lambda i, f: (layer, 0, f + nf)),
                  pl.BlockSpec((1, tf, D), lambda i, f: (layer, f, 0))],
        out_specs=pl.BlockSpec((tm, D), lambda i, f: (i, 0)),
        out_shape=jax.ShapeDtypeStruct((N, D), F32),
        scratch_shapes=[pltpu.VMEM((tm, D), BF16)],
        compiler_params=_cparams(("parallel", "arbitrary")),
        name="ffn",
    )(h, mod, gain, wi, wi, wo)


def _win_kernel(h_ref, mod_ref, g_ref, w_ref, wg_ref, z_ref, zg_ref, u_s):
    j = pl.program_id(1)

    @pl.when(j == 0)
    def _():
        u = _norm_mod(h_ref[...], g_ref[...], mod_ref[0, 3:4, :], mod_ref[0, 4:5, :])
        ub = u.astype(BF16)
        u_s[...] = ub
        zg_ref[...] = _dot(ub, wg_ref[...])

    z_ref[...] = _dot(u_s[...], w_ref[...])


def _win(h, mod, gain, w_main, w_gate, seq):
    N, D = h.shape
    tm, tn = 1024, 1024
    tpb = seq // tm
    GW = w_gate.shape[1]
    return pl.pallas_call(
        _win_kernel,
        grid=(N // tm, Z_WIDTH // tn),
        in_specs=[pl.BlockSpec((tm, D), lambda i, j: (i, 0)),
                  pl.BlockSpec((1, 9, D), lambda i, j: (i // tpb, 0, 0)),
                  pl.BlockSpec((1, D), lambda i, j: (0, 0)),
                  pl.BlockSpec((D, tn), lambda i, j: (0, j)),
                  pl.BlockSpec((D, GW), lambda i, j: (0, 0))],
        out_specs=[pl.BlockSpec((tm, tn), lambda i, j: (i, j)),
                   pl.BlockSpec((tm, GW), lambda i, j: (i, 0))],
        out_shape=[jax.ShapeDtypeStruct((N, Z_WIDTH), F32),
                   jax.ShapeDtypeStruct((N, GW), F32)],
        scratch_shapes=[pltpu.VMEM((tm, D), BF16)],
        compiler_params=_cparams(("parallel", "arbitrary")),
        name="in_proj",
    )(h, mod, gain, w_main, w_gate)


def _hgrn_kernel(q_ref, f_ref, i_ref, g_ref, lb_ref, on_ref, o_ref, st_ref, *, tb):
    C, H, DK = HG_CHUNK, HG_HEADS, HG_D
    nc = tb // C
    W = H * DK

    @pl.when(pl.program_id(1) == 0)
    def _():
        st_ref[...] = jnp.zeros_like(st_ref)

    q = q_ref[...]
    qs = q * _sigmoid(q)
    x = f_ref[...]
    iv = i_ref[...]
    lb = lb_ref[...]
    sp = jnp.log1p(jnp.exp(-jnp.abs(x)))
    lsig = jnp.minimum(x, 0.0) - sp
    a = jnp.log(jnp.maximum(lb, 1e-38))
    bterm = jnp.log1p(-lb) + lsig
    lae = jnp.maximum(a, bterm) + jnp.log1p(jnp.exp(-jnp.abs(a - bterm)))
    log_f = jnp.where(lb > 0.0, lae, bterm)
    k = (1.0 - lb) * jnp.exp(jnp.minimum(-x, 0.0) - sp)

    r = lax.broadcasted_iota(jnp.int32, (tb, tb), 0)
    cidx = lax.broadcasted_iota(jnp.int32, (tb, tb), 1)
    lmat = jnp.where(((r // C) == (cidx // C)) & (cidx <= r), 1.0, 0.0).astype(BF16)
    hi, mid, lo = _split3(log_f)
    b = _dot(lmat, hi) + _dot(lmat, mid) + _dot(lmat, lo)

    qe = (qs * jnp.exp(b)).astype(BF16)
    b3 = b.reshape(nc, C, W)
    bend = b3[:, C - 1:C, :]
    kdec = (k.reshape(nc, C, W) * jnp.exp(bend - b3)).reshape(tb, W).astype(BF16)
    ebend = jnp.exp(bend)
    ib = iv.astype(BF16)
    o_heads = []
    for h in range(H):
        ls = slice(h * DK, (h + 1) * DK)
        uts = [_dot_tn(ib[c * C:(c + 1) * C, ls], kdec[c * C:(c + 1) * C, ls]) for c in range(nc)]
        st = st_ref[h]
        sts = []
        for c in range(nc):
            sts.append(st.astype(BF16))
            st = st * ebend[c][:, ls] + uts[c]
        st_ref[h] = st
        o_heads.append(jnp.concatenate(
            [_dot_nt(qe[c * C:(c + 1) * C, ls], sts[c]) for c in range(nc)], axis=0))
    o = jnp.concatenate(o_heads, axis=1)

    b2 = b * LOG2E
    tmod = lax.broadcasted_iota(jnp.int32, (tb, W), 0) % C
    rr = lax.broadcasted_iota(jnp.int32, (W, W), 0) // DK
    cc = lax.broadcasted_iota(jnp.int32, (W, W), 1) // DK
    ones_bd = jnp.where(rr == cc, 1.0, 0.0).astype(BF16)
    for d in range(C):
        if d == 0:
            p = qs * k
            isd = iv
        else:
            kd = pltpu.roll(k, d, 0)
            bd = pltpu.roll(b2, d, 0)
            isd = pltpu.roll(iv, d, 0)
            p = jnp.where(tmod >= d, qs * kd * jnp.exp2(b2 - bd), 0.0)
        o = o + _dot(p.astype(BF16), ones_bd) * isd

    g = g_ref[...]
    gs = g * _sigmoid(g)
    onw = on_ref[...]
    outs = []
    for h in range(H):
        ls = slice(h * DK, (h + 1) * DK)
        oh = o[:, ls]
        outs.append(oh * lax.rsqrt(jnp.mean(oh * oh, axis=-1, keepdims=True) + EPS) * onw * gs[:, ls])
    o_ref[...] = jnp.concatenate(outs, axis=1).astype(o_ref.dtype)


def _hgrn(z, lb, onorm, batch, seq):
    N = z.shape[0]
    tb = 256
    nb = seq // tb
    W = HG_HEADS * HG_D

    def zspec(off):
        return pl.BlockSpec((tb, W), lambda b, j: (b * nb + j, off // W))

    return pl.pallas_call(
        functools.partial(_hgrn_kernel, tb=tb),
        grid=(batch, nb),
        in_specs=[zspec(OFF_HQ), zspec(OFF_HF), zspec(OFF_HI), zspec(OFF_HG),
                  pl.BlockSpec((1, W), lambda b, j: (0, 0)),
                  pl.BlockSpec((1, HG_D), lambda b, j: (0, 0))],
        out_specs=pl.BlockSpec((tb, W), lambda b, j: (b * nb + j, 0)),
        out_shape=jax.ShapeDtypeStruct((N, W), BF16),
        scratch_shapes=[pltpu.VMEM((HG_HEADS, HG_D, HG_D), F32)],
        compiler_params=_cparams(("parallel", "arbitrary")),
        name="hgrn2",
    )(z, z, z, z, lb, onorm)


def _ssm_kernel(u_ref, bm_ref, cr_ref, ci_ref, kt_ref, p1r_ref, p1i_ref, p2r_ref, p2i_ref,
                alr_ref, ali_ref, d_ref, o_ref, xr_ref, xi_ref, xpr_ref, xpi_ref, *, tt):
    LC = SSM_CHUNK
    nc = tt // LC
    W = SSM_GROUPS * SSM_N

    @pl.when(pl.program_id(1) == 0)
    def _():
        xr_ref[...] = jnp.zeros_like(xr_ref)
        xi_ref[...] = jnp.zeros_like(xi_ref)

    u = u_ref[...]
    ub = u.astype(BF16)
    bu = _dot(ub, bm_ref[...])
    bur = bu[:, :W].reshape(nc, LC, W)
    bui = bu[:, W:].reshape(nc, LC, W)
    p1r, p1i = p1r_ref[...], p1i_ref[...]
    vr = jnp.sum(bur * p1r - bui * p1i, axis=1)
    vi = jnp.sum(bur * p1i + bui * p1r, axis=1)

    alr, ali = alr_ref[...], ali_ref[...]
    xr, xi = xr_ref[...], xi_ref[...]
    for c in range(nc):
        xpr_ref[c:c + 1, :] = xr
        xpi_ref[c:c + 1, :] = xi
        nr = alr * xr - ali * xi + vr[c:c + 1, :]
        ni = alr * xi + ali * xr + vi[c:c + 1, :]
        xr, xi = nr, ni
    xr_ref[...] = xr
    xi_ref[...] = xi

    xpr = xpr_ref[...][:, None, :]
    xpi = xpi_ref[...][:, None, :]
    p2r, p2i = p2r_ref[...], p2i_ref[...]
    zr = (p2r * xpr - p2i * xpi).reshape(tt, W).astype(BF16)
    zi = (p2r * xpi + p2i * xpr).reshape(tt, W).astype(BF16)
    y = _dot(zr, cr_ref[...]) + _dot(zi, ci_ref[...])

    tmod = lax.broadcasted_iota(jnp.int32, u.shape, 0) % LC
    for tau in range(LC):
        if tau == 0:
            us = ub
        else:
            us = jnp.where(tmod >= tau, pltpu.roll(u, tau, 0), 0.0).astype(BF16)
        y = y + _dot(us, kt_ref[tau])
    y = y + d_ref[...] * u
    g = 0.5 * y * (1.0 + jnp.tanh(0.7978845608028654 * (y + 0.044715 * (y * y * y))))
    o_ref[...] = g.astype(o_ref.dtype)


def _ssm_params(a_re, a_im, log_dt, b_re, b_im, c_re, c_im):
    G, P, N, LC = SSM_GROUPS, SSM_P, SSM_N, SSM_CHUNK
    hp = lax.Precision.HIGHEST
    a_re = jnp.minimum(a_re.astype(F32), -1e-4)
    a_im = a_im.astype(F32)
    dt = jnp.exp(log_dt.astype(F32))[:, None]
    mag = jnp.exp(dt * a_re)
    ab_re, ab_im = mag * jnp.cos(dt * a_im), mag * jnp.sin(dt * a_im)
    den = a_re * a_re + a_im * a_im
    nr = ab_re - 1.0
    z_re = (nr * a_re + ab_im * a_im) / den
    z_im = (ab_im * a_re - nr * a_im) / den
    b_re, b_im = b_re.astype(F32), b_im.astype(F32)
    bb_re = z_re[..., None] * b_re - z_im[..., None] * b_im
    bb_im = z_re[..., None] * b_im + z_im[..., None] * b_re
    kk = jnp.arange(LC + 1, dtype=F32)[:, None, None]
    pm = jnp.exp(kk * dt * a_re)
    pw_re, pw_im = pm * jnp.cos(kk * dt * a_im), pm * jnp.sin(kk * dt * a_im)
    eye = jnp.eye(G, dtype=F32)
    bm_re = jnp.einsum('gnp,gh->gphn', bb_re, eye).reshape(G * P, G * N)
    bm_im = jnp.einsum('gnp,gh->gphn', bb_im, eye).reshape(G * P, G * N)
    bmat = jnp.concatenate([bm_re, bm_im], axis=1)
    c_re, c_im = c_re.astype(F32), c_im.astype(F32)
    cr = jnp.einsum('gpn,gh->gnhp', c_re, eye).reshape(G * N, G * P)
    ci = -jnp.einsum('gpn,gh->gnhp', c_im, eye).reshape(G * N, G * P)
    t_re = pw_re[:LC, :, :, None] * bb_re[None] - pw_im[:LC, :, :, None] * bb_im[None]
    t_im = pw_re[:LC, :, :, None] * bb_im[None] + pw_im[:LC, :, :, None] * bb_re[None]
    kt = (jnp.einsum('gqn,tgnp->tgpq', c_re, t_re, precision=hp)
          - jnp.einsum('gqn,tgnp->tgpq', c_im, t_im, precision=hp))
    ktm = jnp.einsum('tgpq,gh->tgphq', kt, eye).reshape(LC, G * P, G * P)
    flat = lambda x: x.reshape(x.shape[0], G * N)
    p1r, p1i = flat(pw_re[LC - 1::-1][:LC]), flat(pw_im[LC - 1::-1][:LC])
    p2r, p2i = flat(pw_re[1:LC + 1]), flat(pw_im[1:LC + 1])
    alr, ali = flat(pw_re[LC:LC + 1]), flat(pw_im[LC:LC + 1])
    return (bmat.astype(BF16), cr.astype(BF16), ci.astype(BF16), ktm.astype(BF16),
            p1r, p1i, p2r, p2i, alr, ali)


def _ssm(z, params, d_skip, batch, seq):
    N = z.shape[0]
    tt = 256
    nb = seq // tt
    CW = SSM_GROUPS * SSM_P
    W = SSM_GROUPS * SSM_N
    LC = SSM_CHUNK
    bmat, cr, ci, ktm, p1r, p1i, p2r, p2i, alr, ali = params
    full = lambda a: pl.BlockSpec(a.shape, lambda b, j, _n=a.ndim: (0,) * _n)
    return pl.pallas_call(
        functools.partial(_ssm_kernel, tt=tt),
        grid=(batch, nb),
        in_specs=[pl.BlockSpec((tt, CW), lambda b, j: (b * nb + j, OFF_SU // CW)),
                  full(bmat), full(cr), full(ci), full(ktm), full(p1r), full(p1i),
                  full(p2r), full(p2i), full(alr), full(ali),
                  pl.BlockSpec((1, CW), lambda b, j: (0, 0))],
        out_specs=pl.BlockSpec((tt, CW), lambda b, j: (b * nb + j, 0)),
        out_shape=jax.ShapeDtypeStruct((N, CW), BF16),
        scratch_shapes=[pltpu.VMEM((1, W), F32), pltpu.VMEM((1, W), F32),
                        pltpu.VMEM((tt // LC, W), F32), pltpu.VMEM((tt // LC, W), F32)],
        compiler_params=_cparams(("parallel", "arbitrary")),
        name="s5_ssm",
    )(z, bmat, cr, ci, ktm, p1r, p1i, p2r, p2i, alr, ali, d_skip)


def _t5_bucket(dist):
    n = jnp.maximum(dist, 0)
    max_exact = REL_BUCKETS // 2
    nf = jnp.maximum(n, 1).astype(F32)
    large = max_exact + (jnp.log(nf / max_exact) / math.log(REL_MAX_DIST / max_exact)
                         * (REL_BUCKETS - max_exact)).astype(jnp.int32)
    large = jnp.minimum(large, REL_BUCKETS - 1)
    return jnp.where(n < max_exact, n, large)


def _bias_kernel(tab_ref, bk_ref, o_ref):
    h = pl.program_id(0)
    bk = bk_ref[...]
    base = tab_ref[REL_BUCKETS - 1, h]
    acc = jnp.full(bk.shape, NEG, F32)
    for k in range(REL_BUCKETS):
        acc = jnp.where(bk == k, (tab_ref[k, h] - base) * LOG2E, acc)
    o_ref[0] = acc


def _bias_tables(rel_table, seq):
    j = jnp.arange(QT)[:, None]
    i = jnp.arange(QT)[None, :]
    d0 = i - j
    b0 = jnp.where(d0 >= 0, _t5_bucket(d0), REL_BUCKETS)
    b1 = _t5_bucket(QT + i - j)
    n = jnp.arange(QT)[:, None]
    t = jnp.arange(seq)[None, :]
    n_cmp = (seq - CMP_LEN) // CMP_STRIDE + 1
    dc = t - (n * CMP_STRIDE + CMP_LEN - 1)
    bc = jnp.where((dc >= 0) & (n < n_cmp), _t5_bucket(dc), REL_BUCKETS)
    nh = rel_table.shape[1]
    tab = rel_table.astype(F32)

    def expand(bk):
        rows, lanes = bk.shape
        return pl.pallas_call(
            _bias_kernel,
            grid=(nh,),
            in_specs=[pl.BlockSpec(memory_space=pltpu.SMEM),
                      pl.BlockSpec((rows, lanes), lambda h: (0, 0))],
            out_specs=pl.BlockSpec((1, rows, lanes), lambda h: (h, 0, 0)),
            out_shape=jax.ShapeDtypeStruct((nh, rows, lanes), F32),
            compiler_params=_cparams(("arbitrary",)),
            name="nsa_bias_tables",
        )(tab, bk.astype(jnp.int32))

    near = expand(jnp.concatenate([b0, b1], axis=0)).reshape(nh, 2, QT, QT)
    return near, expand(bc)


def _head_rms(x, gain):
    return x * lax.rsqrt(jnp.mean(x * x, axis=-1, keepdims=True) + EPS) * gain


def _nsa_prep_kernel(kc_ref, vc_ref, ks_ref, vs_ref, kw_ref, vw_ref, kg_ref, pek_ref, pev_ref,
                     phik_ref, phiv_ref, oks_ref, ovs_ref, okw_ref, ovw_ref, okc_ref, ovc_ref,
                     xpad_ref, *, seq):
    DH = NSA_DH
    kg = kg_ref[...]
    srow = lax.broadcasted_iota(jnp.int32, (seq, DH), 0)
    lane = lax.broadcasted_iota(jnp.int32, (seq, DH), 1)
    onehot = jnp.where(lane == srow // SEL_LEN, 1.0, 0.0).astype(BF16)
    nkt = seq // QT
    npad = WIN // QT
    xpad_ref[seq:seq + CMP_LEN, :] = jnp.zeros((CMP_LEN, DH), F32)

    def value_tiles(v):
        return jnp.concatenate([v, jnp.ones_like(v)], axis=1).T.astype(BF16)

    for g in range(NSA_G):
        ls = slice(g * DH, (g + 1) * DH)
        ksn = _head_rms(ks_ref[:, ls], kg).astype(BF16)
        oks_ref[0, g] = jnp.concatenate([ksn, onehot], axis=1)
        okw_ref[0, g, 0:WIN, :] = jnp.zeros((WIN, DH), BF16)
        okw_ref[0, g, WIN:WIN + seq, :] = _head_rms(kw_ref[:, ls], kg).astype(BF16)
        vst = value_tiles(vs_ref[:, ls])
        vwt = value_tiles(vw_ref[:, ls])
        ovw_ref[0, g, 0:npad] = jnp.zeros((npad, 2 * DH, QT), BF16)
        for kt in range(nkt):
            ovs_ref[0, g, kt] = vst[:, kt * QT:(kt + 1) * QT]
            ovw_ref[0, g, npad + kt] = vwt[:, kt * QT:(kt + 1) * QT]
        for src_ref, pe_ref, phi_ref, is_k in ((kc_ref, pek_ref, phik_ref, True),
                                               (vc_ref, pev_ref, phiv_ref, False)):
            xpad_ref[0:seq, :] = src_ref[:, ls]
            acc = jnp.zeros((QT, DH), F32)
            for l in range(CMP_LEN):
                xl = xpad_ref[pl.ds(l, QT, stride=CMP_STRIDE), :] + pe_ref[l:l + 1, :]
                acc = acc + _dot(xl.astype(BF16), phi_ref[l])
            if is_k:
                okc_ref[0, g] = _head_rms(acc, kg).astype(BF16)
            else:
                ovc_ref[0, g] = value_tiles(acc)


def _nsa_prep(z, k_gain, pe_k, pe_v, phi_k, phi_v, batch, seq):
    G, DH = NSA_G, NSA_DH
    KW = G * DH

    def zspec(off):
        return pl.BlockSpec((seq, KW), lambda b: (b, off // KW))

    full = lambda a: pl.BlockSpec(a.shape, lambda b, _n=a.ndim: (0,) * _n)

    def ospec(*dims):
        return pl.BlockSpec((1, G) + dims, lambda b, _n=len(dims): (b, 0) + (0,) * _n)

    def oshape(*dims):
        return jax.ShapeDtypeStruct((batch, G) + dims, BF16)

    outs = [(seq, 2 * DH), (seq // QT, 2 * DH, QT), (seq + WIN, DH), ((seq + WIN) // QT, 2 * DH, QT),
            (QT, DH), (2 * DH, QT)]
    return pl.pallas_call(
        functools.partial(_nsa_prep_kernel, seq=seq),
        grid=(batch,),
        in_specs=[zspec(OFF_KC), zspec(OFF_VC), zspec(OFF_KS), zspec(OFF_VS), zspec(OFF_KW),
                  zspec(OFF_VW), full(k_gain), full(pe_k), full(pe_v), full(phi_k), full(phi_v)],
        out_specs=[ospec(*d) for d in outs],
        out_shape=[oshape(*d) for d in outs],
        scratch_shapes=[pltpu.VMEM((seq + CMP_LEN, DH), F32)],
        compiler_params=_cparams(("parallel",)),
        name="nsa_kv_prep",
    )(z, z, z, z, z, z, k_gain, pe_k, pe_v, phi_k, phi_v)


def _nsa_kernel(q_ref, gt_ref, ks_ref, vs_ref, kw_ref, vw_ref, kc_ref, vc_ref, cb_ref, nb_ref,
                qg_ref, ov_ref, o_ref):
    DH, HPG, TQ = NSA_DH, NSA_HPG, NSA_TQ
    NR = TQ // QT
    RB = HPG * QT
    R = NR * RB
    NSEL = 32
    nwt = WIN // QT
    qi = pl.program_id(1)
    t0 = qi * TQ

    def cols(r):
        return slice(r * RB, (r + 1) * RB)

    def to_cols(per_head):
        return jnp.concatenate([per_head[h][:, r * QT:(r + 1) * QT] for r in range(NR) for h in range(HPG)],
                               axis=1)

    def krows(x, c):
        return x[c * QT:(c + 1) * QT]

    def vslab(ref, tile0, n):
        return jnp.concatenate([ref[0, 0, tile0 + c] for c in range(n)], axis=1)

    qt = q_ref[...].T
    gain = jnp.concatenate([qg_ref[...]] * NR, axis=1) * (DH ** -0.5 * LOG2E)
    qh = []
    for h in range(HPG):
        x = qt[h * DH:(h + 1) * DH]
        qh.append(x * lax.rsqrt(jnp.mean(x * x, axis=0, keepdims=True) + EPS) * gain)
    qb = to_cols(qh).astype(BF16)
    nb0 = jnp.concatenate([nb_ref[h, 0] for h in range(HPG)], axis=1)
    nb1 = jnp.concatenate([nb_ref[h, 1] for h in range(HPG)], axis=1)

    sc = _dot(kc_ref[0, 0], qb) + to_cols([cb_ref[h] for h in range(HPG)])
    mc = jnp.max(sc, axis=0, keepdims=True)
    ec = jnp.where(sc > 0.5 * NEG, jnp.exp2(sc - mc), 0.0)
    lc = jnp.sum(ec, axis=0, keepdims=True)
    pc = ec / jnp.where(lc > 0.0, lc, 1.0)
    o_cmp = _dot(vc_ref[0, 0], pc.astype(BF16))[:DH]

    psum = jnp.concatenate(
        [sum(pc[:, r * RB + h * QT:r * RB + (h + 1) * QT] for h in range(HPG)) for r in range(NR)], axis=1)
    ovm = ov_ref[...]
    hi, mid, lo = _split3(psum)
    imp = (_dot(ovm, hi) + _dot(ovm, mid) + _dot(ovm, lo))[0:NSEL]
    jrow = lax.broadcasted_iota(jnp.int32, (NSEL, TQ), 0)
    blk = (t0 + lax.broadcasted_iota(jnp.int32, (NSEL, TQ), 1)) // SEL_LEN
    forced = (jrow == 0) | (jrow == blk) | (jrow == blk - 1)
    imp = jnp.where(forced, FORCE_SCORE, imp)
    imp = jnp.where(jrow <= blk, imp, NEG)
    cnt = jnp.zeros((NSEL, TQ), F32)
    for jp in range(NSEL):
        rowv = imp[jp:jp + 1, :]
        beats = (rowv > imp) | ((rowv == imp) & (jrow > jp))
        cnt = cnt + jnp.where(beats, 1.0, 0.0)
    selb = jnp.where(cnt < float(SEL_TOPN), 0.0, NEG)
    qaug = jnp.concatenate([qb, to_cols([selb] * HPG).astype(BF16),
                            jnp.zeros((2 * DH - DH - NSEL, R), BF16)], axis=0)

    def ktile(ref, start, size):
        return ref[0, 0, pl.ds(pl.multiple_of(start, QT), size), :]

    def step(scores, vt, state):
        m_new = [jnp.maximum(state[r][0], jnp.max(scores[r], axis=0, keepdims=True)) for r in range(NR)]
        out = []
        for r in range(NR):
            p = jnp.exp2(scores[r] - m_new[r]).astype(BF16)
            acc = state[r][1] * jnp.exp2(state[r][0] - m_new[r]) + _dot(vt[r], p)
            out.append((m_new[r], acc))
        return out

    scores, vts = [], []
    for r in range(NR):
        s = _dot(ktile(ks_ref, t0, (r + 1) * QT), qaug[:, cols(r)])
        parts = [krows(s, c) for c in range(r + 1)]
        parts[r] = parts[r] + nb0
        if r >= 1:
            parts[r - 1] = parts[r - 1] + nb1
        scores.append(jnp.concatenate(parts, axis=0))
        vts.append(vslab(vs_ref, qi * NR, r + 1))
    state = step(scores, vts, [(jnp.full((1, RB), NEG, F32), jnp.zeros((2 * DH, RB), F32))] * NR)

    def past_tile(kt, flat):
        kk = ktile(ks_ref, kt * TQ, TQ)
        vt = vslab(vs_ref, kt * NR, NR)
        scores = [_dot(kk, qaug[:, cols(r)]) for r in range(NR)]
        corner = scores[0][TQ - QT:] + jnp.where(kt == qi - 1, nb1, 0.0)
        scores[0] = jnp.concatenate([scores[0][:TQ - QT], corner], axis=0)
        new = step(scores, [vt] * NR, [(flat[2 * r], flat[2 * r + 1]) for r in range(NR)])
        return tuple(x for pair in new for x in pair)

    flat = lax.fori_loop(0, qi, past_tile, tuple(x for pair in state for x in pair))
    acc_s = jnp.concatenate([flat[2 * r + 1] for r in range(NR)], axis=1)

    jj = lax.broadcasted_iota(jnp.int32, (QT, RB), 0)
    ii = lax.broadcasted_iota(jnp.int32, (QT, RB), 1) % QT
    acc_w = []
    for r in range(NR):
        s = _dot(ktile(kw_ref, t0 + r * QT, WIN + QT), qb[:, cols(r)])
        parts = [krows(s, c) for c in range(nwt + 1)]
        parts[nwt] = parts[nwt] + nb0
        parts[nwt - 1] = parts[nwt - 1] + nb1
        parts[0] = jnp.where(jj > ii, parts[0], NEG)
        for c in range(nwt - r):
            parts[c] = jnp.where(qi > 0, parts[c], NEG)
        s = jnp.concatenate(parts, axis=0)
        p = jnp.exp2(s - jnp.max(s, axis=0, keepdims=True)).astype(BF16)
        acc_w.append(_dot(vslab(vw_ref, qi * NR + r, nwt + 1), p))
    acc_w = jnp.concatenate(acc_w, axis=1)

    gt = _sigmoid(gt_ref[...]).T

    def gate_row(br):
        return to_cols([gt[br * HPG + h:br * HPG + h + 1, :] for h in range(HPG)])

    def inv_l(acc):
        l = acc[DH:DH + 1, :]
        return 1.0 / jnp.where(l > 0.0, l, 1.0)

    o = (gate_row(0) * o_cmp + (gate_row(1) * inv_l(acc_s)) * acc_s[:DH]
         + (gate_row(2) * inv_l(acc_w)) * acc_w[:DH])
    o_hd = jnp.concatenate(
        [jnp.concatenate([o[:, r * RB + h * QT:r * RB + (h + 1) * QT] for r in range(NR)], axis=1)
         for h in range(HPG)], axis=0)
    o_ref[...] = o_hd.T.astype(o_ref.dtype)


def _nsa(z, zg, prep, bias, q_gain, ovm, batch, seq):
    N = z.shape[0]
    G, HPG, DH, TQ = NSA_G, NSA_HPG, NSA_DH, NSA_TQ
    nq = seq // TQ
    R = HPG * TQ
    oks, ovs, okw, ovw, okc, ovc = prep
    nbias, cbias = bias
    QW = HPG * DH
    qg = jnp.broadcast_to(q_gain.reshape(DH, 1), (DH, QT))

    def kvspec(a):
        return pl.BlockSpec((1, 1) + a.shape[2:], lambda p, i, _n=a.ndim - 2: (p // G, p % G) + (0,) * _n)

    return pl.pallas_call(
        _nsa_kernel,
        grid=(batch * G, nq),
        in_specs=[pl.BlockSpec((TQ, QW), lambda p, i: ((p // G) * nq + i, OFF_NQ // QW + p % G)),
                  pl.BlockSpec((TQ, 128), lambda p, i: ((p // G) * nq + i, p % G)),
                  kvspec(oks), kvspec(ovs), kvspec(okw), kvspec(ovw), kvspec(okc), kvspec(ovc),
                  pl.BlockSpec((HPG, QT, TQ), lambda p, i: (p % G, 0, i)),
                  pl.BlockSpec((HPG, 2, QT, QT), lambda p, i: (p % G, 0, 0, 0)),
                  pl.BlockSpec((DH, QT), lambda p, i: (0, 0)),
                  pl.BlockSpec((QT, QT), lambda p, i: (0, 0))],
        out_specs=pl.BlockSpec((TQ, QW), lambda p, i: ((p // G) * nq + i, p % G)),
        out_shape=jax.ShapeDtypeStruct((N, G * QW), BF16),
        compiler_params=_cparams(("parallel", "arbitrary")),
        name="nsa_attention",
    )(z, zg, oks, ovs, okw, ovw, okc, ovc, cbias, nbias, qg, ovm)


def _merge_kernel(h_ref, mod_ref, za_ref, zb_ref, zc_ref, oa_ref, ys_ref, oc_ref,
                  wa_ref, wb_ref, wc_ref, wo_ref, o_ref):
    D = h_ref.shape[1]
    ya = _dot(oa_ref[...], wa_ref[...])
    zz = _dot(ys_ref[...], wb_ref[...])
    yb = zz[:, :D] * _sigmoid(zz[:, D:])
    yc = _dot(oc_ref[...], wc_ref[...])
    merged = _sigmoid(za_ref[...]) * ya + _sigmoid(zb_ref[...]) * yb + _sigmoid(zc_ref[...]) * yc
    o_ref[...] = h_ref[...] + mod_ref[0, 5:6, :] * _dot(merged.astype(BF16), wo_ref[...])


def _merge(h, mod, z, oa, ys, oc, wa, wb, wc, wo, seq):
    N, D = h.shape
    tm = 256
    tpb = seq // tm
    row = lambda w: pl.BlockSpec((tm, w), lambda i: (i, 0))
    res = lambda a: pl.BlockSpec(a.shape, lambda i: (0, 0), pipeline_mode=pl.Buffered(1))
    return pl.pallas_call(
        _merge_kernel,
        grid=(N // tm,),
        in_specs=[row(D),
                  pl.BlockSpec((1, 9, D), lambda i: (i // tpb, 0, 0)),
                  pl.BlockSpec((tm, D), lambda i: (i, OFF_ZA // D)),
                  pl.BlockSpec((tm, D), lambda i: (i, OFF_ZB // D)),
                  pl.BlockSpec((tm, D), lambda i: (i, OFF_ZC // D)),
                  row(oa.shape[1]), row(ys.shape[1]), row(oc.shape[1]),
                  res(wa), res(wb), res(wc), res(wo)],
        out_specs=row(D),
        out_shape=jax.ShapeDtypeStruct((N, D), F32),
        compiler_params=_cparams(("parallel",)),
        name="mix_merge",
    )(h, mod, z, z, z, oa, ys, oc, wa, wb, wc, wo)


def _permute_w_in(w):
    gate0 = 4 * 512 + 512 + 1024 + 6 * 256
    ngate = 3 * NSA_G * NSA_HPG
    main = jnp.concatenate([w[:, gate0 + ngate:], w[:, :gate0]], axis=1)
    gw = w[:, gate0:gate0 + ngate].reshape(w.shape[0], 3, NSA_G, NSA_HPG)
    gw = gw.transpose(0, 2, 1, 3).reshape(w.shape[0], NSA_G, 3 * NSA_HPG)
    gw = jnp.pad(gw, ((0, 0), (0, 0), (0, 128 - 3 * NSA_HPG))).reshape(w.shape[0], NSA_G * 128)
    return main.astype(BF16), gw.astype(BF16)


def _overlap_matrix():
    j = jnp.arange(QT)[:, None]
    n = jnp.arange(QT)[None, :]
    st = n * CMP_STRIDE
    ov = (st < j * SEL_LEN + SEL_LEN) & (st + CMP_LEN > j * SEL_LEN)
    return jnp.where(ov, 1.0, 0.0).astype(BF16)


def kernel(x, c, ada_w, ada_b, norm_g, ffn1_wi, ffn1_wo, ffn2_wi, ffn2_wo, w_in, hg_lb_logits,
           hg_onorm, hg_proj, ssm_a_re, ssm_a_im, ssm_log_dt, ssm_b_re, ssm_b_im, ssm_c_re,
           ssm_c_im, ssm_d, ssm_glu_w, nsa_q_gain, nsa_k_gain, nsa_pe_k, nsa_pe_v, nsa_phi_k,
           nsa_phi_v, nsa_proj, rel_table, w_out):
    B, S, D = x.shape
    L = ada_w.shape[0]
    N = B * S
    assert S % 512 == 0 and S // SEL_LEN == 32 and S // QT == 16
    lb_cum = jnp.cumsum(jax.nn.softmax(hg_lb_logits.astype(F32), axis=0), axis=0)
    lower_bounds = lb_cum - lb_cum[0:1]
    mods = _mods(c, ada_w, ada_b).reshape(L, B, 9, D)
    bias = _bias_tables(rel_table, S)
    ovm = _overlap_matrix()
    h = x.reshape(N, D)
    for l in range(L):
        mod = mods[l]
        h = _ffn(h, mod, norm_g[l, 0:1], ffn1_wi, ffn1_wo, l, 0, S)
        w_main, w_gate = _permute_w_in(w_in[l])
        z, zg = _win(h, mod, norm_g[l, 1:2], w_main, w_gate, S)
        oa = _hgrn(z, lower_bounds[l:l + 1], hg_onorm[l:l + 1], B, S)
        sp = _ssm_params(ssm_a_re[l], ssm_a_im[l], ssm_log_dt[l], ssm_b_re[l], ssm_b_im[l],
                         ssm_c_re[l], ssm_c_im[l])
        ys = _ssm(z, sp, ssm_d[l:l + 1], B, S)
        prep = _nsa_prep(z, nsa_k_gain[l:l + 1], nsa_pe_k[l], nsa_pe_v[l],
                         nsa_phi_k[l].astype(BF16), nsa_phi_v[l].astype(BF16), B, S)
        oc = _nsa(z, zg, prep, bias, nsa_q_gain[l:l + 1], ovm, B, S)
        h = _merge(h, mod, z, oa, ys, oc, hg_proj[l].astype(BF16), ssm_glu_w[l].astype(BF16),
                   nsa_proj[l].astype(BF16), w_out[l].astype(BF16), S)
        h = _ffn(h, mod, norm_g[l, 2:3], ffn2_wi, ffn2_wo, l, 6, S)
    return h.reshape(B, S, D)
```

```python
import functools
import math

import jax
import jax.numpy as jnp
from jax import lax
from jax.experimental import pallas as pl
from jax.experimental.pallas import tpu as pltpu

F32 = jnp.float32
BF16 = jnp.bfloat16

EPS = 1e-6
NEG = -1e30
LOG2E = 1.4426950408889634
FORCE_SCORE = 1e4

HG_HEADS, HG_D, HG_CHUNK = 4, 128, 16
SSM_GROUPS, SSM_P, SSM_N, SSM_CHUNK = 32, 16, 64, 8
NSA_G, NSA_HPG, NSA_DH = 4, 4, 64
CMP_LEN, CMP_STRIDE, SEL_LEN, SEL_TOPN, WIN = 32, 16, 64, 16, 512
REL_BUCKETS, REL_MAX_DIST = 32, 128
QT = 128
NSA_TQ = 512
NSA_VR = NSA_DH + 16

VMEM_LIMIT = 60 * 1024 * 1024

OFF_ZA, OFF_ZB, OFF_ZC = 0, 2048, 4096
OFF_HQ, OFF_HF, OFF_HI, OFF_HG = 6144, 6656, 7168, 7680
OFF_SU, OFF_NQ = 8192, 8704
OFF_KC, OFF_VC, OFF_KS, OFF_VS, OFF_KW, OFF_VW = 9728, 9984, 10240, 10496, 10752, 11008
Z_WIDTH = 11264


def _cparams(sem):
    return pltpu.CompilerParams(dimension_semantics=sem, vmem_limit_bytes=VMEM_LIMIT)


def _dot(a, b):
    return jnp.dot(a, b, preferred_element_type=F32)


def _dot_nt(a, b):
    return lax.dot_general(a, b, (((1,), (1,)), ((), ())), preferred_element_type=F32)


def _dot_tn(a, b):
    return lax.dot_general(a, b, (((0,), (0,)), ((), ())), preferred_element_type=F32)


def _sigmoid(x):
    return 1.0 / (1.0 + jnp.exp(-x))


def _split3(x):
    hi = x.astype(BF16)
    r = x - hi.astype(F32)
    mid = r.astype(BF16)
    lo = (r - mid.astype(F32)).astype(BF16)
    return hi, mid, lo


def _norm_mod(x, gain, shift, scale):
    ms = jnp.mean(x * x, axis=-1, keepdims=True)
    y = x * lax.rsqrt(ms + EPS) * gain
    return y * (1.0 + scale) + shift


def _mod_kernel(c_ref, w_ref, b_ref, o_ref):
    c = c_ref[...]
    ca = (c * _sigmoid(c)).astype(BF16)
    o_ref[0] = _dot(ca, w_ref[0].astype(BF16)) + b_ref[0]


def _mods(c, ada_w, ada_b):
    L, D, W = ada_w.shape
    B = c.shape[0]
    tn = 1024
    return pl.pallas_call(
        _mod_kernel,
        grid=(L, W // tn),
        in_specs=[pl.BlockSpec((B, D), lambda l, j: (0, 0)),
                  pl.BlockSpec((1, D, tn), lambda l, j: (l, 0, j)),
                  pl.BlockSpec((1, 1, tn), lambda l, j: (l, 0, j))],
        out_specs=pl.BlockSpec((1, B, tn), lambda l, j: (l, 0, j)),
        out_shape=jax.ShapeDtypeStruct((L, B, W), F32),
        compiler_params=_cparams(("parallel", "parallel")),
        name="adaln_mod",
    )(c, ada_w, ada_b.reshape(L, 1, W))


def _ffn_kernel(h_ref, mod_ref, g_ref, wi1_ref, wi2_ref, wo_ref, o_ref, u_s, *, k0, nf):
    f = pl.program_id(1)

    @pl.when(f == 0)
    def _():
        u = _norm_mod(h_ref[...], g_ref[...], mod_ref[0, k0:k0 + 1, :], mod_ref[0, k0 + 1:k0 + 2, :])
        u_s[...] = u.astype(BF16)
        o_ref[...] = jnp.zeros_like(o_ref)

    u = u_s[...]
    a1 = _dot(u, wi1_ref[0].astype(BF16))
    a2 = _dot(u, wi2_ref[0].astype(BF16))
    act = (a1 * _sigmoid(a1) * a2).astype(BF16)
    o_ref[...] += _dot(act, wo_ref[0].astype(BF16))

    @pl.when(f == nf - 1)
    def _():
        o_ref[...] = h_ref[...] + (0.5 * mod_ref[0, k0 + 2:k0 + 3, :]) * o_ref[...]


def _ffn(h, mod, gain, wi, wo, layer, k0, seq):
    N, D = h.shape
    dff = wo.shape[1]
    tm, tf = 1024, 256
    nf = dff // tf
    tpb = seq // tm
    return pl.pallas_call(
        functools.partial(_ffn_kernel, k0=k0, nf=nf),
        grid=(N // tm, nf),
        in_specs=[pl.BlockSpec((tm, D), lambda i, f: (i, 0)),
                  pl.BlockSpec((1, 9, D), lambda i, f: (i // tpb, 0, 0)),
                  pl.BlockSpec((1, D), lambda i, f: (0, 0)),
                  pl.BlockSpec((1, D, tf), lambda i, f: (layer, 0, f)),
                  pl.BlockSpec((1, D, tf), lambda i, f: (layer, 0, f + nf)),
                  pl.BlockSpec((1, tf, D), lambda i, f: (layer, f, 0))],
        out_specs=pl.BlockSpec((tm, D), lambda i, f: (i, 0)),
        out_shape=jax.ShapeDtypeStruct((N, D), F32),
        scratch_shapes=[pltpu.VMEM((tm, D), BF16)],
        compiler_params=_cparams(("parallel", "arbitrary")),
        name="ffn",
    )(h, mod, gain, wi, wi, wo)


def _win_kernel(h_ref, mod_ref, g_ref, wm_ref, wz_ref, wg_ref, z_ref, zg_ref, u_s, *, nz):
    j = pl.program_id(1)

    @pl.when(j == 0)
    def _():
        u = _norm_mod(h_ref[...], g_ref[...], mod_ref[0, 3:4, :], mod_ref[0, 4:5, :])
        ub = u.astype(BF16)
        u_s[...] = ub
        zg_ref[...] = _dot(ub, wg_ref[0])

    @pl.when(j < nz)
    def _():
        z_ref[...] = _dot(u_s[...], wz_ref[0])

    @pl.when(j >= nz)
    def _():
        z_ref[...] = _dot(u_s[...], wm_ref[0])


def _win(h, mod, gain, w_mix, w_z, w_gate, layer, seq):
    N, D = h.shape
    tm, tn = 1024, 1024
    tpb = seq // tm
    GW = w_gate.shape[2]
    nz = w_z.shape[2] // tn
    nmix = w_mix.shape[2] // tn
    return pl.pallas_call(
        functools.partial(_win_kernel, nz=nz),
        grid=(N // tm, nz + nmix),
        in_specs=[pl.BlockSpec((tm, D), lambda i, j: (i, 0)),
                  pl.BlockSpec((1, 9, D), lambda i, j: (i // tpb, 0, 0)),
                  pl.BlockSpec((1, D), lambda i, j: (0, 0)),
                  pl.BlockSpec((1, D, tn), lambda i, j: (layer, 0, jnp.maximum(j - nz, 0))),
                  pl.BlockSpec((1, D, tn), lambda i, j: (layer, 0, jnp.minimum(j, nz - 1))),
                  pl.BlockSpec((1, D, GW), lambda i, j: (layer, 0, 0))],
        out_specs=[pl.BlockSpec((tm, tn), lambda i, j: (i, j)),
                   pl.BlockSpec((tm, GW), lambda i, j: (i, 0))],
        out_shape=[jax.ShapeDtypeStruct((N, Z_WIDTH), F32),
                   jax.ShapeDtypeStruct((N, GW), F32)],
        scratch_shapes=[pltpu.VMEM((tm, D), BF16)],
        compiler_params=_cparams(("parallel", "arbitrary")),
        name="in_proj",
    )(h, mod, gain, w_mix, w_z, w_gate)


def _hgrn_kernel(q_ref, f_ref, i_ref, g_ref, lb_ref, on_ref, o_ref, st_ref, *, tb):
    C, H, DK = HG_CHUNK, HG_HEADS, HG_D
    nc = tb // C
    W = H * DK

    @pl.when(pl.program_id(1) == 0)
    def _():
        st_ref[...] = jnp.zeros_like(st_ref)

    q = q_ref[...]
    qs = q * _sigmoid(q)
    x = f_ref[...]
    iv = i_ref[...]
    lb = lb_ref[...]
    sp = jnp.log1p(jnp.exp(-jnp.abs(x)))
    lsig = jnp.minimum(x, 0.0) - sp
    a = jnp.log(jnp.maximum(lb, 1e-38))
    bterm = jnp.log1p(-lb) + lsig
    lae = jnp.maximum(a, bterm) + jnp.log1p(jnp.exp(-jnp.abs(a - bterm)))
    log_f = jnp.where(lb > 0.0, lae, bterm)
    k = (1.0 - lb) * jnp.exp(jnp.minimum(-x, 0.0) - sp)

    r = lax.broadcasted_iota(jnp.int32, (tb, tb), 0)
    cidx = lax.broadcasted_iota(jnp.int32, (tb, tb), 1)
    lmat = jnp.where(((r // C) == (cidx // C)) & (cidx <= r), 1.0, 0.0).astype(BF16)
    hi, mid, lo = _split3(log_f)
    b = _dot(lmat, hi) + _dot(lmat, mid) + _dot(lmat, lo)

    qe = (qs * jnp.exp(b)).astype(BF16)
    b3 = b.reshape(nc, C, W)
    bend = b3[:, C - 1:C, :]
    kdec = (k.reshape(nc, C, W) * jnp.exp(bend - b3)).reshape(tb, W).astype(BF16)
    ebend = jnp.exp(bend)
    ib = iv.astype(BF16)
    o_heads = []
    for h in range(H):
        ls = slice(h * DK, (h + 1) * DK)
        uts = [_dot_tn(ib[c * C:(c + 1) * C, ls], kdec[c * C:(c + 1) * C, ls]) for c in range(nc)]
        st = st_ref[h]
        sts = []
        for c in range(nc):
            sts.append(st.astype(BF16))
            st = st * ebend[c][:, ls] + uts[c]
        st_ref[h] = st
        o_heads.append(jnp.concatenate(
            [_dot_nt(qe[c * C:(c + 1) * C, ls], sts[c]) for c in range(nc)], axis=0))
    o = jnp.concatenate(o_heads, axis=1)

    b2 = b * LOG2E
    tmod = lax.broadcasted_iota(jnp.int32, (tb, W), 0) % C
    rr = lax.broadcasted_iota(jnp.int32, (W, W), 0) // DK
    cc = lax.broadcasted_iota(jnp.int32, (W, W), 1) // DK
    ones_bd = jnp.where(rr == cc, 1.0, 0.0).astype(BF16)
    for d in range(C):
        if d == 0:
            p = qs * k
            isd = iv
        else:
            kd = pltpu.roll(k, d, 0)
            bd = pltpu.roll(b2, d, 0)
            isd = pltpu.roll(iv, d, 0)
            p = jnp.where(tmod >= d, qs * kd * jnp.exp2(b2 - bd), 0.0)
        o = o + _dot(p.astype(BF16), ones_bd) * isd

    g = g_ref[...]
    gs = g * _sigmoid(g)
    onw = on_ref[...]
    outs = []
    for h in range(H):
        ls = slice(h * DK, (h + 1) * DK)
        oh = o[:, ls]
        outs.append(oh * lax.rsqrt(jnp.mean(oh * oh, axis=-1, keepdims=True) + EPS) * onw * gs[:, ls])
    o_ref[...] = jnp.concatenate(outs, axis=1).astype(o_ref.dtype)


def _hgrn(z, lb, onorm, batch, seq):
    N = z.shape[0]
    tb = 256
    nb = seq // tb
    W = HG_HEADS * HG_D

    def zspec(off):
        return pl.BlockSpec((tb, W), lambda b, j: (b * nb + j, off // W))

    return pl.pallas_call(
        functools.partial(_hgrn_kernel, tb=tb),
        grid=(batch, nb),
        in_specs=[zspec(OFF_HQ), zspec(OFF_HF), zspec(OFF_HI), zspec(OFF_HG),
                  pl.BlockSpec((1, W), lambda b, j: (0, 0)),
                  pl.BlockSpec((1, HG_D), lambda b, j: (0, 0))],
        out_specs=pl.BlockSpec((tb, W), lambda b, j: (b * nb + j, 0)),
        out_shape=jax.ShapeDtypeStruct((N, W), BF16),
        scratch_shapes=[pltpu.VMEM((HG_HEADS, HG_D, HG_D), F32)],
        compiler_params=_cparams(("parallel", "arbitrary")),
        name="hgrn2",
    )(z, z, z, z, lb, onorm)


def _ssm_kernel(u_ref, bm_ref, cr_ref, ci_ref, kt_ref, p1r_ref, p1i_ref, p2r_ref, p2i_ref,
                alr_ref, ali_ref, d_ref, o_ref, xr_ref, xi_ref, xpr_ref, xpi_ref, *, tt):
    LC = SSM_CHUNK
    nc = tt // LC
    W = SSM_GROUPS * SSM_N

    @pl.when(pl.program_id(1) == 0)
    def _():
        xr_ref[...] = jnp.zeros_like(xr_ref)
        xi_ref[...] = jnp.zeros_like(xi_ref)

    u = u_ref[...]
    ub = u.astype(BF16)
    bu = _dot(ub, bm_ref[...])
    bur = bu[:, :W].reshape(nc, LC, W)
    bui = bu[:, W:].reshape(nc, LC, W)
    p1r, p1i = p1r_ref[...], p1i_ref[...]
    vr = jnp.sum(bur * p1r - bui * p1i, axis=1)
    vi = jnp.sum(bur * p1i + bui * p1r, axis=1)

    alr, ali = alr_ref[...], ali_ref[...]
    xr, xi = xr_ref[...], xi_ref[...]
    for c in range(nc):
        xpr_ref[c:c + 1, :] = xr
        xpi_ref[c:c + 1, :] = xi
        nr = alr * xr - ali * xi + vr[c:c + 1, :]
        ni = alr * xi + ali * xr + vi[c:c + 1, :]
        xr, xi = nr, ni
    xr_ref[...] = xr
    xi_ref[...] = xi

    xpr = xpr_ref[...][:, None, :]
    xpi = xpi_ref[...][:, None, :]
    p2r, p2i = p2r_ref[...], p2i_ref[...]
    zr = (p2r * xpr - p2i * xpi).reshape(tt, W).astype(BF16)
    zi = (p2r * xpi + p2i * xpr).reshape(tt, W).astype(BF16)
    y = _dot(zr, cr_ref[...]) + _dot(zi, ci_ref[...])

    tmod = lax.broadcasted_iota(jnp.int32, u.shape, 0) % LC
    for tau in range(LC):
        if tau == 0:
            us = ub
        else:
            us = jnp.where(tmod >= tau, pltpu.roll(u, tau, 0), 0.0).astype(BF16)
        y = y + _dot(us, kt_ref[tau])
    y = y + d_ref[...] * u
    g = 0.5 * y * (1.0 + jnp.tanh(0.7978845608028654 * (y + 0.044715 * (y * y * y))))
    o_ref[...] = g.astype(o_ref.dtype)


def _ssm_params(a_re, a_im, log_dt, b_re, b_im, c_re, c_im):
    G, P, N, LC = SSM_GROUPS, SSM_P, SSM_N, SSM_CHUNK
    hp = lax.Precision.HIGHEST
    a_re = jnp.minimum(a_re.astype(F32), -1e-4)
    a_im = a_im.astype(F32)
    dt = jnp.exp(log_dt.astype(F32))[:, None]
    mag = jnp.exp(dt * a_re)
    ab_re, ab_im = mag * jnp.cos(dt * a_im), mag * jnp.sin(dt * a_im)
    den = a_re * a_re + a_im * a_im
    nr = ab_re - 1.0
    z_re = (nr * a_re + ab_im * a_im) / den
    z_im = (ab_im * a_re - nr * a_im) / den
    b_re, b_im = b_re.astype(F32), b_im.astype(F32)
    bb_re = z_re[..., None] * b_re - z_im[..., None] * b_im
    bb_im = z_re[..., None] * b_im + z_im[..., None] * b_re
    kk = jnp.arange(LC + 1, dtype=F32)[:, None, None]
    pm = jnp.exp(kk * dt * a_re)
    pw_re, pw_im = pm * jnp.cos(kk * dt * a_im), pm * jnp.sin(kk * dt * a_im)
    def bdiag(src):
        a, b = src.shape[1:]
        x = jnp.broadcast_to(src.reshape(G * a, 1, b), (G * a, G, b)).reshape(G * a, G * b)
        rg = jnp.arange(G * a)[:, None] // a
        cg = jnp.arange(G * b)[None, :] // b
        return jnp.where(rg == cg, x, 0.0)

    bmat = jnp.concatenate([bdiag(bb_re.transpose(0, 2, 1)), bdiag(bb_im.transpose(0, 2, 1))], axis=1)
    c_re, c_im = c_re.astype(F32), c_im.astype(F32)
    cr = bdiag(c_re.transpose(0, 2, 1))
    ci = -bdiag(c_im.transpose(0, 2, 1))
    t_re = pw_re[:LC, :, :, None] * bb_re[None] - pw_im[:LC, :, :, None] * bb_im[None]
    t_im = pw_re[:LC, :, :, None] * bb_im[None] + pw_im[:LC, :, :, None] * bb_re[None]
    kt = (jnp.einsum('gqn,tgnp->tgpq', c_re, t_re, precision=hp)
          - jnp.einsum('gqn,tgnp->tgpq', c_im, t_im, precision=hp))
    ktm = jax.vmap(bdiag)(kt)
    flat = lambda x: x.reshape(x.shape[0], G * N)
    p1r, p1i = flat(pw_re[LC - 1::-1][:LC]), flat(pw_im[LC - 1::-1][:LC])
    p2r, p2i = flat(pw_re[1:LC + 1]), flat(pw_im[1:LC + 1])
    alr, ali = flat(pw_re[LC:LC + 1]), flat(pw_im[LC:LC + 1])
    return (bmat.astype(BF16), cr.astype(BF16), ci.astype(BF16), ktm.astype(BF16),
            p1r, p1i, p2r, p2i, alr, ali)


def _ssm(z, params, d_skip, batch, seq):
    N = z.shape[0]
    tt = 256
    nb = seq // tt
    CW = SSM_GROUPS * SSM_P
    W = SSM_GROUPS * SSM_N
    LC = SSM_CHUNK
    bmat, cr, ci, ktm, p1r, p1i, p2r, p2i, alr, ali = params
    full = lambda a: pl.BlockSpec(a.shape, lambda b, j, _n=a.ndim: (0,) * _n)
    return pl.pallas_call(
        functools.partial(_ssm_kernel, tt=tt),
        grid=(batch, nb),
        in_specs=[pl.BlockSpec((tt, CW), lambda b, j: (b * nb + j, OFF_SU // CW)),
                  full(bmat), full(cr), full(ci), full(ktm), full(p1r), full(p1i),
                  full(p2r), full(p2i), full(alr), full(ali),
                  pl.BlockSpec((1, CW), lambda b, j: (0, 0))],
        out_specs=pl.BlockSpec((tt, CW), lambda b, j: (b * nb + j, 0)),
        out_shape=jax.ShapeDtypeStruct((N, CW), BF16),
        scratch_shapes=[pltpu.VMEM((1, W), F32), pltpu.VMEM((1, W), F32),
                        pltpu.VMEM((tt // LC, W), F32), pltpu.VMEM((tt // LC, W), F32)],
        compiler_params=_cparams(("parallel", "arbitrary")),
        name="s5_ssm",
    )(z, bmat, cr, ci, ktm, p1r, p1i, p2r, p2i, alr, ali, d_skip)


def _t5_bucket(dist):
    n = jnp.maximum(dist, 0)
    max_exact = REL_BUCKETS // 2
    nf = jnp.maximum(n, 1).astype(F32)
    large = max_exact + (jnp.log(nf / max_exact) / math.log(REL_MAX_DIST / max_exact)
                         * (REL_BUCKETS - max_exact)).astype(jnp.int32)
    large = jnp.minimum(large, REL_BUCKETS - 1)
    return jnp.where(n < max_exact, n, large)


def _bias_kernel(tab_ref, bk_ref, o_ref):
    h = pl.program_id(0)
    bk = bk_ref[...]
    base = tab_ref[REL_BUCKETS - 1, h]
    acc = jnp.full(bk.shape, NEG, F32)
    for k in range(REL_BUCKETS):
        acc = jnp.where(bk == k, (tab_ref[k, h] - base) * LOG2E, acc)
    o_ref[0] = acc


def _bias_tables(rel_table, seq):
    j = jnp.arange(QT)[:, None]
    i = jnp.arange(QT)[None, :]
    d0 = i - j
    b0 = jnp.where(d0 >= 0, _t5_bucket(d0), REL_BUCKETS)
    b1 = _t5_bucket(QT + i - j)
    n = jnp.arange(QT)[:, None]
    t = jnp.arange(seq)[None, :]
    n_cmp = (seq - CMP_LEN) // CMP_STRIDE + 1
    dc = t - (n * CMP_STRIDE + CMP_LEN - 1)
    bc = jnp.where((dc >= 0) & (n < n_cmp), _t5_bucket(dc), REL_BUCKETS)
    nh = rel_table.shape[1]
    tab = rel_table.astype(F32)

    def expand(bk):
        rows, lanes = bk.shape
        return pl.pallas_call(
            _bias_kernel,
            grid=(nh,),
            in_specs=[pl.BlockSpec(memory_space=pltpu.SMEM),
                      pl.BlockSpec((rows, lanes), lambda h: (0, 0))],
            out_specs=pl.BlockSpec((1, rows, lanes), lambda h: (h, 0, 0)),
            out_shape=jax.ShapeDtypeStruct((nh, rows, lanes), F32),
            compiler_params=_cparams(("arbitrary",)),
            name="nsa_bias_tables",
        )(tab, bk.astype(jnp.int32))

    near = expand(jnp.concatenate([b0, b1], axis=0)).reshape(nh, 2, QT, QT)
    return near, expand(bc)


def _head_rms(x, gain):
    return x * lax.rsqrt(jnp.mean(x * x, axis=-1, keepdims=True) + EPS) * gain


def _nsa_prep_kernel(kc_ref, vc_ref, ks_ref, vs_ref, kw_ref, vw_ref, kg_ref, pek_ref, pev_ref,
                     phik_ref, phiv_ref, oks_ref, ovs_ref, okw_ref, ovw_ref, okc_ref, ovc_ref,
                     xpad_ref, *, seq):
    DH = NSA_DH
    kg = kg_ref[...]
    srow = lax.broadcasted_iota(jnp.int32, (seq, DH), 0)
    lane = lax.broadcasted_iota(jnp.int32, (seq, DH), 1)
    onehot = jnp.where(lane == srow // SEL_LEN, 1.0, 0.0).astype(BF16)
    nkt = seq // QT
    npad = WIN // QT
    xpad_ref[seq:seq + CMP_LEN, :] = jnp.zeros((CMP_LEN, DH), F32)

    def value_tiles(v):
        return jnp.concatenate([v, jnp.ones_like(v)], axis=1).T[:NSA_VR].astype(BF16)

    for g in range(NSA_G):
        ls = slice(g * DH, (g + 1) * DH)
        ksn = _head_rms(ks_ref[:, ls], kg).astype(BF16)
        oks_ref[0, g] = jnp.concatenate([ksn, onehot], axis=1)
        okw_ref[0, g, 0:WIN, :] = jnp.zeros((WIN, DH), BF16)
        okw_ref[0, g, WIN:WIN + seq, :] = _head_rms(kw_ref[:, ls], kg).astype(BF16)
        vst = value_tiles(vs_ref[:, ls])
        vwt = value_tiles(vw_ref[:, ls])
        ovw_ref[0, g, 0:npad] = jnp.zeros((npad, NSA_VR, QT), BF16)
        for kt in range(nkt):
            ovs_ref[0, g, kt] = vst[:, kt * QT:(kt + 1) * QT]
            ovw_ref[0, g, npad + kt] = vwt[:, kt * QT:(kt + 1) * QT]
        for src_ref, pe_ref, phi_ref, is_k in ((kc_ref, pek_ref, phik_ref, True),
                                               (vc_ref, pev_ref, phiv_ref, False)):
            xpad_ref[0:seq, :] = src_ref[:, ls]
            acc = jnp.zeros((QT, DH), F32)
            for l in range(CMP_LEN):
                xl = xpad_ref[pl.ds(l, QT, stride=CMP_STRIDE), :] + pe_ref[l:l + 1, :]
                acc = acc + _dot(xl.astype(BF16), phi_ref[l])
            if is_k:
                okc_ref[0, g] = _head_rms(acc, kg).astype(BF16)
            else:
                ovc_ref[0, g] = value_tiles(acc)


def _nsa_prep(z, k_gain, pe_k, pe_v, phi_k, phi_v, batch, seq):
    G, DH = NSA_G, NSA_DH
    KW = G * DH

    def zspec(off):
        return pl.BlockSpec((seq, KW), lambda b: (b, off // KW))

    full = lambda a: pl.BlockSpec(a.shape, lambda b, _n=a.ndim: (0,) * _n)

    def ospec(*dims):
        return pl.BlockSpec((1, G) + dims, lambda b, _n=len(dims): (b, 0) + (0,) * _n)

    def oshape(*dims):
        return jax.ShapeDtypeStruct((batch, G) + dims, BF16)

    outs = [(seq, 2 * DH), (seq // QT, NSA_VR, QT), (seq + WIN, DH), ((seq + WIN) // QT, NSA_VR, QT),
            (QT, DH), (NSA_VR, QT)]
    return pl.pallas_call(
        functools.partial(_nsa_prep_kernel, seq=seq),
        grid=(batch,),
        in_specs=[zspec(OFF_KC), zspec(OFF_VC), zspec(OFF_KS), zspec(OFF_VS), zspec(OFF_KW),
                  zspec(OFF_VW), full(k_gain), full(pe_k), full(pe_v), full(phi_k), full(phi_v)],
        out_specs=[ospec(*d) for d in outs],
        out_shape=[oshape(*d) for d in outs],
        scratch_shapes=[pltpu.VMEM((seq + CMP_LEN, DH), F32)],
        compiler_params=_cparams(("parallel",)),
        name="nsa_kv_prep",
    )(z, z, z, z, z, z, k_gain, pe_k, pe_v, phi_k, phi_v)


def _nsa_kernel(q_ref, gt_ref, ks_ref, vs_ref, kw_ref, vw_ref, kc_ref, vc_ref, cb_ref, nb_ref,
                qg_ref, ov_ref, o_ref):
    DH, HPG, TQ = NSA_DH, NSA_HPG, NSA_TQ
    NR = TQ // QT
    RB = HPG * QT
    R = NR * RB
    NSEL = 32
    nwt = WIN // QT
    qi = pl.program_id(1)
    t0 = qi * TQ

    def cols(r):
        return slice(r * RB, (r + 1) * RB)

    def to_cols(per_head):
        return jnp.concatenate([per_head[h][:, r * QT:(r + 1) * QT] for r in range(NR) for h in range(HPG)],
                               axis=1)

    def krows(x, c):
        return x[c * QT:(c + 1) * QT]

    def vslab(ref, tile0, n):
        return jnp.concatenate([ref[0, 0, tile0 + c] for c in range(n)], axis=1)

    qt = q_ref[...].T
    gain = jnp.concatenate([qg_ref[...]] * NR, axis=1) * (DH ** -0.5 * LOG2E)
    qh = []
    for h in range(HPG):
        x = qt[h * DH:(h + 1) * DH]
        qh.append(x * lax.rsqrt(jnp.mean(x * x, axis=0, keepdims=True) + EPS) * gain)
    qb = to_cols(qh).astype(BF16)
    nb0 = jnp.concatenate([nb_ref[h, 0] for h in range(HPG)], axis=1)
    nb1 = jnp.concatenate([nb_ref[h, 1] for h in range(HPG)], axis=1)

    sc = _dot(kc_ref[0, 0], qb) + to_cols([cb_ref[h] for h in range(HPG)])
    mc = jnp.max(sc, axis=0, keepdims=True)
    ec = jnp.where(sc > 0.5 * NEG, jnp.exp2(sc - mc), 0.0)
    lc = jnp.sum(ec, axis=0, keepdims=True)
    pc = ec / jnp.where(lc > 0.0, lc, 1.0)
    o_cmp = _dot(vc_ref[0, 0], pc.astype(BF16))[:DH]

    psum = jnp.concatenate(
        [sum(pc[:, r * RB + h * QT:r * RB + (h + 1) * QT] for h in range(HPG)) for r in range(NR)], axis=1)
    ovm = ov_ref[...]
    hi, mid, lo = _split3(psum)
    imp = (_dot(ovm, hi) + _dot(ovm, mid) + _dot(ovm, lo))[0:NSEL]
    jrow = lax.broadcasted_iota(jnp.int32, (NSEL, TQ), 0)
    blk = (t0 + lax.broadcasted_iota(jnp.int32, (NSEL, TQ), 1)) // SEL_LEN
    forced = (jrow == 0) | (jrow == blk) | (jrow == blk - 1)
    imp = jnp.where(forced, FORCE_SCORE, imp)
    imp = jnp.where(jrow <= blk, imp, NEG)
    cnt = jnp.zeros((NSEL, TQ), F32)
    for jp in range(NSEL):
        rowv = imp[jp:jp + 1, :]
        beats = (rowv > imp) | ((rowv == imp) & (jrow > jp))
        cnt = cnt + jnp.where(beats, 1.0, 0.0)
    selb = jnp.where(cnt < float(SEL_TOPN), 0.0, NEG)
    qaug = jnp.concatenate([qb, to_cols([selb] * HPG).astype(BF16),
                            jnp.zeros((2 * DH - DH - NSEL, R), BF16)], axis=0)

    def ktile(ref, start, size):
        return ref[0, 0, pl.ds(pl.multiple_of(start, QT), size), :]

    def step(scores, vt, state):
        m_new = [jnp.maximum(state[r][0], jnp.max(scores[r], axis=0, keepdims=True)) for r in range(NR)]
        out = []
        for r in range(NR):
            p = jnp.exp2(scores[r] - m_new[r]).astype(BF16)
            acc = state[r][1] * jnp.exp2(state[r][0] - m_new[r]) + _dot(vt[r], p)
            out.append((m_new[r], acc))
        return out

    scores, vts = [], []
    for r in range(NR):
        s = _dot(ktile(ks_ref, t0, (r + 1) * QT), qaug[:, cols(r)])
        parts = [krows(s, c) for c in range(r + 1)]
        parts[r] = parts[r] + nb0
        if r >= 1:
            parts[r - 1] = parts[r - 1] + nb1
        scores.append(jnp.concatenate(parts, axis=0))
        vts.append(vslab(vs_ref, qi * NR, r + 1))
    state = step(scores, vts, [(jnp.full((1, RB), NEG, F32), jnp.zeros((NSA_VR, RB), F32))] * NR)

    def past_tile(kt, flat):
        kk = ktile(ks_ref, kt * TQ, TQ)
        vt = vslab(vs_ref, kt * NR, NR)
        scores = [_dot(kk, qaug[:, cols(r)]) for r in range(NR)]
        corner = scores[0][TQ - QT:] + jnp.where(kt == qi - 1, nb1, 0.0)
        scores[0] = jnp.concatenate([scores[0][:TQ - QT], corner], axis=0)
        new = step(scores, [vt] * NR, [(flat[2 * r], flat[2 * r + 1]) for r in range(NR)])
        return tuple(x for pair in new for x in pair)

    flat = lax.fori_loop(0, qi, past_tile, tuple(x for pair in state for x in pair))
    acc_s = jnp.concatenate([flat[2 * r + 1] for r in range(NR)], axis=1)

    jj = lax.broadcasted_iota(jnp.int32, (QT, RB), 0)
    ii = lax.broadcasted_iota(jnp.int32, (QT, RB), 1) % QT
    acc_w = []
    for r in range(NR):
        s = _dot(ktile(kw_ref, t0 + r * QT, WIN + QT), qb[:, cols(r)])
        parts = [krows(s, c) for c in range(nwt + 1)]
        parts[nwt] = parts[nwt] + nb0
        parts[nwt - 1] = parts[nwt - 1] + nb1
        parts[0] = jnp.where(jj > ii, parts[0], NEG)
        for c in range(nwt - r):
            parts[c] = jnp.where(qi > 0, parts[c], NEG)
        s = jnp.concatenate(parts, axis=0)
        p = jnp.exp2(s - jnp.max(s, axis=0, keepdims=True)).astype(BF16)
        acc_w.append(_dot(vslab(vw_ref, qi * NR + r, nwt + 1), p))
    acc_w = jnp.concatenate(acc_w, axis=1)

    gt = _sigmoid(gt_ref[...]).T

    def gate_row(br):
        return to_cols([gt[br * HPG + h:br * HPG + h + 1, :] for h in range(HPG)])

    def inv_l(acc):
        l = acc[DH:DH + 1, :]
        return 1.0 / jnp.where(l > 0.0, l, 1.0)

    o = (gate_row(0) * o_cmp + (gate_row(1) * inv_l(acc_s)) * acc_s[:DH]
         + (gate_row(2) * inv_l(acc_w)) * acc_w[:DH])
    o_hd = jnp.concatenate(
        [jnp.concatenate([o[:, r * RB + h * QT:r * RB + (h + 1) * QT] for r in range(NR)], axis=1)
         for h in range(HPG)], axis=0)
    o_ref[...] = o_hd.T.astype(o_ref.dtype)


def _nsa(z, zg, prep, bias, q_gain, ovm, batch, seq):
    N = z.shape[0]
    G, HPG, DH, TQ = NSA_G, NSA_HPG, NSA_DH, NSA_TQ
    nq = seq // TQ
    R = HPG * TQ
    oks, ovs, okw, ovw, okc, ovc = prep
    nbias, cbias = bias
    QW = HPG * DH
    qg = jnp.broadcast_to(q_gain.reshape(DH, 1), (DH, QT))

    def kvspec(a):
        return pl.BlockSpec((1, 1) + a.shape[2:], lambda p, i, _n=a.ndim - 2: (p // G, p % G) + (0,) * _n)

    return pl.pallas_call(
        _nsa_kernel,
        grid=(batch * G, nq),
        in_specs=[pl.BlockSpec((TQ, QW), lambda p, i: ((p // G) * nq + i, OFF_NQ // QW + p % G)),
                  pl.BlockSpec((TQ, 128), lambda p, i: ((p // G) * nq + i, p % G)),
                  kvspec(oks), kvspec(ovs), kvspec(okw), kvspec(ovw), kvspec(okc), kvspec(ovc),
                  pl.BlockSpec((HPG, QT, TQ), lambda p, i: (p % G, 0, i)),
                  pl.BlockSpec((HPG, 2, QT, QT), lambda p, i: (p % G, 0, 0, 0)),
                  pl.BlockSpec((DH, QT), lambda p, i: (0, 0)),
                  pl.BlockSpec((QT, QT), lambda p, i: (0, 0))],
        out_specs=pl.BlockSpec((TQ, QW), lambda p, i: ((p // G) * nq + i, p % G)),
        out_shape=jax.ShapeDtypeStruct((N, G * QW), BF16),
        compiler_params=_cparams(("parallel", "arbitrary")),
        name="nsa_attention",
    )(z, zg, oks, ovs, okw, ovw, okc, ovc, cbias, nbias, qg, ovm)


def _merge_kernel(h_ref, mod_ref, za_ref, zb_ref, zc_ref, oa_ref, ys_ref, oc_ref,
                  wa_ref, wb_ref, wc_ref, wo_ref, o_ref):
    D = h_ref.shape[1]
    ya = _dot(oa_ref[...], wa_ref[...])
    zz = _dot(ys_ref[...], wb_ref[...])
    yb = zz[:, :D] * _sigmoid(zz[:, D:])
    yc = _dot(oc_ref[...], wc_ref[...])
    merged = _sigmoid(za_ref[...]) * ya + _sigmoid(zb_ref[...]) * yb + _sigmoid(zc_ref[...]) * yc
    o_ref[...] = h_ref[...] + mod_ref[0, 5:6, :] * _dot(merged.astype(BF16), wo_ref[...])


def _merge(h, mod, z, oa, ys, oc, wa, wb, wc, wo, seq):
    N, D = h.shape
    tm = 256
    tpb = seq // tm
    row = lambda w: pl.BlockSpec((tm, w), lambda i: (i, 0))
    res = lambda a: pl.BlockSpec(a.shape, lambda i: (0, 0), pipeline_mode=pl.Buffered(1))
    return pl.pallas_call(
        _merge_kernel,
        grid=(N // tm,),
        in_specs=[row(D),
                  pl.BlockSpec((1, 9, D), lambda i: (i // tpb, 0, 0)),
                  pl.BlockSpec((tm, D), lambda i: (i, OFF_ZA // D)),
                  pl.BlockSpec((tm, D), lambda i: (i, OFF_ZB // D)),
                  pl.BlockSpec((tm, D), lambda i: (i, OFF_ZC // D)),
                  row(oa.shape[1]), row(ys.shape[1]), row(oc.shape[1]),
                  res(wa), res(wb), res(wc), res(wo)],
        out_specs=row(D),
        out_shape=jax.ShapeDtypeStruct((N, D), F32),
        compiler_params=_cparams(("parallel",)),
        name="mix_merge",
    )(h, mod, z, z, z, oa, ys, oc, wa, wb, wc, wo)


def _split_w_in(w):
    gate0 = 4 * 512 + 512 + 1024 + 6 * 256
    ngate = 3 * NSA_G * NSA_HPG
    L, D = w.shape[:2]
    gw = w[:, :, gate0:gate0 + ngate].reshape(L, D, 3, NSA_G, NSA_HPG)
    gw = gw.transpose(0, 1, 3, 2, 4).reshape(L, D, NSA_G, 3 * NSA_HPG)
    gw = jnp.pad(gw, ((0, 0), (0, 0), (0, 0), (0, 128 - 3 * NSA_HPG))).reshape(L, D, NSA_G * 128)
    return w[:, :, :gate0].astype(BF16), w[:, :, gate0 + ngate:].astype(BF16), gw.astype(BF16)


def _overlap_matrix():
    j = jnp.arange(QT)[:, None]
    n = jnp.arange(QT)[None, :]
    st = n * CMP_STRIDE
    ov = (st < j * SEL_LEN + SEL_LEN) & (st + CMP_LEN > j * SEL_LEN)
    return jnp.where(ov, 1.0, 0.0).astype(BF16)


def kernel(x, c, ada_w, ada_b, norm_g, ffn1_wi, ffn1_wo, ffn2_wi, ffn2_wo, w_in, hg_lb_logits,
           hg_onorm, hg_proj, ssm_a_re, ssm_a_im, ssm_log_dt, ssm_b_re, ssm_b_im, ssm_c_re,
           ssm_c_im, ssm_d, ssm_glu_w, nsa_q_gain, nsa_k_gain, nsa_pe_k, nsa_pe_v, nsa_phi_k,
           nsa_phi_v, nsa_proj, rel_table, w_out):
    B, S, D = x.shape
    L = ada_w.shape[0]
    N = B * S
    assert S % 512 == 0 and S // SEL_LEN == 32 and S // QT == 16
    lb_cum = jnp.cumsum(jax.nn.softmax(hg_lb_logits.astype(F32), axis=0), axis=0)
    lower_bounds = lb_cum - lb_cum[0:1]
    mods = _mods(c, ada_w, ada_b).reshape(L, B, 9, D)
    bias = _bias_tables(rel_table, S)
    ovm = _overlap_matrix()
    w_mix, w_z, w_gate = _split_w_in(w_in)
    h = x.reshape(N, D)
    for l in range(L):
        mod = mods[l]
        h = _ffn(h, mod, norm_g[l, 0:1], ffn1_wi, ffn1_wo, l, 0, S)
        z, zg = _win(h, mod, norm_g[l, 1:2], w_mix, w_z, w_gate, l, S)
        oa = _hgrn(z, lower_bounds[l:l + 1], hg_onorm[l:l + 1], B, S)
        sp = _ssm_params(ssm_a_re[l], ssm_a_im[l], ssm_log_dt[l], ssm_b_re[l], ssm_b_im[l],
                         ssm_c_re[l], ssm_c_im[l])
        ys = _ssm(z, sp, ssm_d[l:l + 1], B, S)
        prep = _nsa_prep(z, nsa_k_gain[l:l + 1], nsa_pe_k[l], nsa_pe_v[l],
                         nsa_phi_k[l].astype(BF16), nsa_phi_v[l].astype(BF16), B, S)
        oc = _nsa(z, zg, prep, bias, nsa_q_gain[l:l + 1], ovm, B, S)
        h = _merge(h, mod, z, oa, ys, oc, hg_proj[l].astype(BF16), ssm_glu_w[l].astype(BF16),
                   nsa_proj[l].astype(BF16), w_out[l].astype(BF16), S)
        h = _ffn(h, mod, norm_g[l, 2:3], ffn2_wi, ffn2_wo, l, 6, S)
    return h.reshape(B, S, D)
```

```python
import functools
import math

import jax
import jax.numpy as jnp
from jax import lax
from jax.experimental import pallas as pl
from jax.experimental.pallas import tpu as pltpu

F32 = jnp.float32
BF16 = jnp.bfloat16

EPS = 1e-6
NEG = -1e30
LOG2E = 1.4426950408889634
FORCE_SCORE = 1e4

HG_HEADS, HG_D, HG_CHUNK = 4, 128, 16
SSM_GROUPS, SSM_P, SSM_N, SSM_CHUNK = 32, 16, 64, 8
NSA_G, NSA_HPG, NSA_DH = 4, 4, 64
CMP_LEN, CMP_STRIDE, SEL_LEN, SEL_TOPN, WIN = 32, 16, 64, 16, 512
REL_BUCKETS, REL_MAX_DIST = 32, 128
QT = 128
NSA_TQ = 512
NSA_VR = NSA_DH + 16

VMEM_LIMIT = 60 * 1024 * 1024

OFF_ZA, OFF_ZB, OFF_ZC = 0, 2048, 4096
OFF_HQ, OFF_HF, OFF_HI, OFF_HG = 6144, 6656, 7168, 7680
OFF_SU, OFF_NQ = 8192, 8704
OFF_KC, OFF_VC, OFF_KS, OFF_VS, OFF_KW, OFF_VW = 9728, 9984, 10240, 10496, 10752, 11008
Z_WIDTH = 11264


def _cparams(sem):
    return pltpu.CompilerParams(dimension_semantics=sem, vmem_limit_bytes=VMEM_LIMIT)


def _dot(a, b):
    return jnp.dot(a, b, preferred_element_type=F32)


def _dot_nt(a, b):
    return lax.dot_general(a, b, (((1,), (1,)), ((), ())), preferred_element_type=F32)


def _dot_tn(a, b):
    return lax.dot_general(a, b, (((0,), (0,)), ((), ())), preferred_element_type=F32)


def _sigmoid(x):
    return 1.0 / (1.0 + jnp.exp(-x))


def _split3(x):
    hi = x.astype(BF16)
    r = x - hi.astype(F32)
    mid = r.astype(BF16)
    lo = (r - mid.astype(F32)).astype(BF16)
    return hi, mid, lo


def _norm_mod(x, gain, shift, scale):
    ms = jnp.mean(x * x, axis=-1, keepdims=True)
    y = x * lax.rsqrt(ms + EPS) * gain
    return y * (1.0 + scale) + shift


def _mod_kernel(c_ref, w_ref, b_ref, o_ref):
    c = c_ref[...]
    ca = (c * _sigmoid(c)).astype(BF16)
    o_ref[0] = _dot(ca, w_ref[0].astype(BF16)) + b_ref[0]


def _mods(c, ada_w, ada_b):
    L, D, W = ada_w.shape
    B = c.shape[0]
    tn = 1024
    return pl.pallas_call(
        _mod_kernel,
        grid=(L, W // tn),
        in_specs=[pl.BlockSpec((B, D), lambda l, j: (0, 0)),
                  pl.BlockSpec((1, D, tn), lambda l, j: (l, 0, j)),
                  pl.BlockSpec((1, 1, tn), lambda l, j: (l, 0, j))],
        out_specs=pl.BlockSpec((1, B, tn), lambda l, j: (l, 0, j)),
        out_shape=jax.ShapeDtypeStruct((L, B, W), F32),
        compiler_params=_cparams(("parallel", "parallel")),
        name="adaln_mod",
    )(c, ada_w, ada_b.reshape(L, 1, W))


def _ffn_kernel(h_ref, mod_ref, g_ref, wi1_ref, wi2_ref, wo_ref, o_ref, u_s, act_s, *, k0, nf):
    f = pl.program_id(1)

    def up(slot):
        u = u_s[...]
        a1 = _dot(u, wi1_ref[0].astype(BF16))
        a2 = _dot(u, wi2_ref[0].astype(BF16))
        act_s[slot] = (a1 * _sigmoid(a1) * a2).astype(BF16)

    def down(slot):
        o_ref[...] += _dot(act_s[slot], wo_ref[0].astype(BF16))

    @pl.when(f == 0)
    def _():
        u = _norm_mod(h_ref[...], g_ref[...], mod_ref[0, k0:k0 + 1, :], mod_ref[0, k0 + 1:k0 + 2, :])
        u_s[...] = u.astype(BF16)
        o_ref[...] = jnp.zeros_like(o_ref)
        up(0)

    @pl.when((f > 0) & (f < nf))
    def _():
        down((f - 1) % 2)
        up(f % 2)

    @pl.when(f == nf)
    def _():
        down((nf - 1) % 2)
        o_ref[...] = h_ref[...] + (0.5 * mod_ref[0, k0 + 2:k0 + 3, :]) * o_ref[...]


def _ffn(h, mod, gain, wi, wo, layer, k0, seq):
    N, D = h.shape
    dff = wo.shape[1]
    tm, tf = 1024, 256
    nf = dff // tf
    tpb = seq // tm
    return pl.pallas_call(
        functools.partial(_ffn_kernel, k0=k0, nf=nf),
        grid=(N // tm, nf + 1),
        in_specs=[pl.BlockSpec((tm, D), lambda i, f: (i, 0)),
                  pl.BlockSpec((1, 9, D), lambda i, f: (i // tpb, 0, 0)),
                  pl.BlockSpec((1, D), lambda i, f: (0, 0)),
                  pl.BlockSpec((1, D, tf), lambda i, f: (layer, 0, jnp.minimum(f, nf - 1))),
                  pl.BlockSpec((1, D, tf), lambda i, f: (layer, 0, jnp.minimum(f, nf - 1) + nf)),
                  pl.BlockSpec((1, tf, D), lambda i, f: (layer, jnp.maximum(f - 1, 0), 0))],
        out_specs=pl.BlockSpec((tm, D), lambda i, f: (i, 0)),
        out_shape=jax.ShapeDtypeStruct((N, D), F32),
        scratch_shapes=[pltpu.VMEM((tm, D), BF16), pltpu.VMEM((2, tm, tf), BF16)],
        compiler_params=_cparams(("parallel", "arbitrary")),
        name="ffn",
    )(h, mod, gain, wi, wi, wo)


def _win_kernel(h_ref, mod_ref, g_ref, wm_ref, wz_ref, wg_ref, z_ref, zg_ref, u_s, *, nz):
    j = pl.program_id(1)

    @pl.when(j == 0)
    def _():
        u = _norm_mod(h_ref[...], g_ref[...], mod_ref[0, 3:4, :], mod_ref[0, 4:5, :])
        ub = u.astype(BF16)
        u_s[...] = ub
        zg_ref[...] = _dot(ub, wg_ref[0])

    @pl.when(j < nz)
    def _():
        z_ref[...] = _dot(u_s[...], wz_ref[0])

    @pl.when(j >= nz)
    def _():
        z_ref[...] = _dot(u_s[...], wm_ref[0])


def _win(h, mod, gain, w_mix, w_z, w_gate, layer, seq):
    N, D = h.shape
    tm, tn = 1024, 1024
    tpb = seq // tm
    GW = w_gate.shape[2]
    nz = w_z.shape[2] // tn
    nmix = (Z_WIDTH - w_z.shape[2]) // tn
    return pl.pallas_call(
        functools.partial(_win_kernel, nz=nz),
        grid=(N // tm, nz + nmix),
        in_specs=[pl.BlockSpec((tm, D), lambda i, j: (i, 0)),
                  pl.BlockSpec((1, 9, D), lambda i, j: (i // tpb, 0, 0)),
                  pl.BlockSpec((1, D), lambda i, j: (0, 0)),
                  pl.BlockSpec((1, D, tn), lambda i, j: (layer, 0, jnp.maximum(j - nz, 0))),
                  pl.BlockSpec((1, D, tn), lambda i, j: (layer, 0, jnp.minimum(j, nz - 1))),
                  pl.BlockSpec((1, D, GW), lambda i, j: (layer, 0, 0))],
        out_specs=[pl.BlockSpec((tm, tn), lambda i, j: (i, j)),
                   pl.BlockSpec((tm, GW), lambda i, j: (i, 0))],
        out_shape=[jax.ShapeDtypeStruct((N, Z_WIDTH), F32),
                   jax.ShapeDtypeStruct((N, GW), F32)],
        scratch_shapes=[pltpu.VMEM((tm, D), BF16)],
        compiler_params=_cparams(("parallel", "arbitrary")),
        name="in_proj",
    )(h, mod, gain, w_mix, w_z, w_gate)


def _hgrn_kernel(q_ref, f_ref, i_ref, g_ref, lb_ref, on_ref, o_ref, st_ref, *, tb):
    C, H, DK = HG_CHUNK, HG_HEADS, HG_D
    nc = tb // C
    W = H * DK

    @pl.when(pl.program_id(1) == 0)
    def _():
        st_ref[...] = jnp.zeros_like(st_ref)

    q = q_ref[...]
    qs = q * _sigmoid(q)
    x = f_ref[...]
    iv = i_ref[...]
    lb = lb_ref[...]
    sp = jnp.log1p(jnp.exp(-jnp.abs(x)))
    lsig = jnp.minimum(x, 0.0) - sp
    a = jnp.log(jnp.maximum(lb, 1e-38))
    bterm = jnp.log1p(-lb) + lsig
    lae = jnp.maximum(a, bterm) + jnp.log1p(jnp.exp(-jnp.abs(a - bterm)))
    log_f = jnp.where(lb > 0.0, lae, bterm)
    k = (1.0 - lb) * jnp.exp(jnp.minimum(-x, 0.0) - sp)

    r = lax.broadcasted_iota(jnp.int32, (tb, tb), 0)
    cidx = lax.broadcasted_iota(jnp.int32, (tb, tb), 1)
    lmat = jnp.where(((r // C) == (cidx // C)) & (cidx <= r), 1.0, 0.0).astype(BF16)
    hi, mid, lo = _split3(log_f)
    b = _dot(lmat, hi) + _dot(lmat, mid) + _dot(lmat, lo)

    qe = (qs * jnp.exp(b)).astype(BF16)
    b3 = b.reshape(nc, C, W)
    bend = b3[:, C - 1:C, :]
    kdec = (k.reshape(nc, C, W) * jnp.exp(bend - b3)).reshape(tb, W).astype(BF16)
    ebend = jnp.exp(bend)
    ib = iv.astype(BF16)
    o_heads = []
    for h in range(H):
        ls = slice(h * DK, (h + 1) * DK)
        uts = [_dot_tn(ib[c * C:(c + 1) * C, ls], kdec[c * C:(c + 1) * C, ls]) for c in range(nc)]
        st = st_ref[h]
        sts = []
        for c in range(nc):
            sts.append(st.astype(BF16))
            st = st * ebend[c][:, ls] + uts[c]
        st_ref[h] = st
        o_heads.append(jnp.concatenate(
            [_dot_nt(qe[c * C:(c + 1) * C, ls], sts[c]) for c in range(nc)], axis=0))
    o = jnp.concatenate(o_heads, axis=1)

    b2 = b * LOG2E
    tmod = lax.broadcasted_iota(jnp.int32, (tb, W), 0) % C
    rr = lax.broadcasted_iota(jnp.int32, (W, W), 0) // DK
    cc = lax.broadcasted_iota(jnp.int32, (W, W), 1) // DK
    ones_bd = jnp.where(rr == cc, 1.0, 0.0).astype(BF16)
    for d in range(C):
        if d == 0:
            p = qs * k
            isd = iv
        else:
            kd = pltpu.roll(k, d, 0)
            bd = pltpu.roll(b2, d, 0)
            isd = pltpu.roll(iv, d, 0)
            p = jnp.where(tmod >= d, qs * kd * jnp.exp2(b2 - bd), 0.0)
        o = o + _dot(p.astype(BF16), ones_bd) * isd

    g = g_ref[...]
    gs = g * _sigmoid(g)
    onw = on_ref[...]
    outs = []
    for h in range(H):
        ls = slice(h * DK, (h + 1) * DK)
        oh = o[:, ls]
        outs.append(oh * lax.rsqrt(jnp.mean(oh * oh, axis=-1, keepdims=True) + EPS) * onw * gs[:, ls])
    o_ref[...] = jnp.concatenate(outs, axis=1).astype(o_ref.dtype)


def _hgrn(z, lb, onorm, batch, seq):
    N = z.shape[0]
    tb = 256
    nb = seq // tb
    W = HG_HEADS * HG_D

    def zspec(off):
        return pl.BlockSpec((tb, W), lambda b, j: (b * nb + j, off // W))

    return pl.pallas_call(
        functools.partial(_hgrn_kernel, tb=tb),
        grid=(batch, nb),
        in_specs=[zspec(OFF_HQ), zspec(OFF_HF), zspec(OFF_HI), zspec(OFF_HG),
                  pl.BlockSpec((1, W), lambda b, j: (0, 0)),
                  pl.BlockSpec((1, HG_D), lambda b, j: (0, 0))],
        out_specs=pl.BlockSpec((tb, W), lambda b, j: (b * nb + j, 0)),
        out_shape=jax.ShapeDtypeStruct((N, W), BF16),
        scratch_shapes=[pltpu.VMEM((HG_HEADS, HG_D, HG_D), F32)],
        compiler_params=_cparams(("parallel", "arbitrary")),
        name="hgrn2",
    )(z, z, z, z, lb, onorm)


def _ssm_kernel(u_ref, bm_ref, cr_ref, ci_ref, kt_ref, p1r_ref, p1i_ref, p2r_ref, p2i_ref,
                alr_ref, ali_ref, d_ref, o_ref, xr_ref, xi_ref, xpr_ref, xpi_ref, *, tt):
    LC = SSM_CHUNK
    nc = tt // LC
    W = SSM_GROUPS * SSM_N

    @pl.when(pl.program_id(1) == 0)
    def _():
        xr_ref[...] = jnp.zeros_like(xr_ref)
        xi_ref[...] = jnp.zeros_like(xi_ref)

    u = u_ref[...]
    ub = u.astype(BF16)
    bu = _dot(ub, bm_ref[...])
    bur = bu[:, :W].reshape(nc, LC, W)
    bui = bu[:, W:].reshape(nc, LC, W)
    p1r, p1i = p1r_ref[...], p1i_ref[...]
    vr = jnp.sum(bur * p1r - bui * p1i, axis=1)
    vi = jnp.sum(bur * p1i + bui * p1r, axis=1)

    alr, ali = alr_ref[...], ali_ref[...]
    xr, xi = xr_ref[...], xi_ref[...]
    for c in range(nc):
        xpr_ref[c:c + 1, :] = xr
        xpi_ref[c:c + 1, :] = xi
        nr = alr * xr - ali * xi + vr[c:c + 1, :]
        ni = alr * xi + ali * xr + vi[c:c + 1, :]
        xr, xi = nr, ni
    xr_ref[...] = xr
    xi_ref[...] = xi

    xpr = xpr_ref[...][:, None, :]
    xpi = xpi_ref[...][:, None, :]
    p2r, p2i = p2r_ref[...], p2i_ref[...]
    zr = (p2r * xpr - p2i * xpi).reshape(tt, W).astype(BF16)
    zi = (p2r * xpi + p2i * xpr).reshape(tt, W).astype(BF16)
    y = _dot(zr, cr_ref[...]) + _dot(zi, ci_ref[...])

    tmod = lax.broadcasted_iota(jnp.int32, u.shape, 0) % LC
    for tau in range(LC):
        if tau == 0:
            us = ub
        else:
            us = jnp.where(tmod >= tau, pltpu.roll(u, tau, 0), 0.0).astype(BF16)
        y = y + _dot(us, kt_ref[tau])
    y = y + d_ref[...] * u
    g = 0.5 * y * (1.0 + jnp.tanh(0.7978845608028654 * (y + 0.044715 * (y * y * y))))
    o_ref[...] = g.astype(o_ref.dtype)


def _ssm_params(a_re, a_im, log_dt, b_re, b_im, c_re, c_im):
    G, P, N, LC = SSM_GROUPS, SSM_P, SSM_N, SSM_CHUNK
    hp = lax.Precision.HIGHEST
    a_re = jnp.minimum(a_re.astype(F32), -1e-4)
    a_im = a_im.astype(F32)
    dt = jnp.exp(log_dt.astype(F32))[:, None]
    mag = jnp.exp(dt * a_re)
    ab_re, ab_im = mag * jnp.cos(dt * a_im), mag * jnp.sin(dt * a_im)
    den = a_re * a_re + a_im * a_im
    nr = ab_re - 1.0
    z_re = (nr * a_re + ab_im * a_im) / den
    z_im = (ab_im * a_re - nr * a_im) / den
    b_re, b_im = b_re.astype(F32), b_im.astype(F32)
    bb_re = z_re[..., None] * b_re - z_im[..., None] * b_im
    bb_im = z_re[..., None] * b_im + z_im[..., None] * b_re
    kk = jnp.arange(LC + 1, dtype=F32)[:, None, None]
    pm = jnp.exp(kk * dt * a_re)
    pw_re, pw_im = pm * jnp.cos(kk * dt * a_im), pm * jnp.sin(kk * dt * a_im)
    def bdiag(src):
        a, b = src.shape[1:]
        x = jnp.broadcast_to(src.reshape(G * a, 1, b), (G * a, G, b)).reshape(G * a, G * b)
        rg = jnp.arange(G * a)[:, None] // a
        cg = jnp.arange(G * b)[None, :] // b
        return jnp.where(rg == cg, x, 0.0)

    bmat = jnp.concatenate([bdiag(bb_re.transpose(0, 2, 1)), bdiag(bb_im.transpose(0, 2, 1))], axis=1)
    c_re, c_im = c_re.astype(F32), c_im.astype(F32)
    cr = bdiag(c_re.transpose(0, 2, 1))
    ci = -bdiag(c_im.transpose(0, 2, 1))
    t_re = pw_re[:LC, :, :, None] * bb_re[None] - pw_im[:LC, :, :, None] * bb_im[None]
    t_im = pw_re[:LC, :, :, None] * bb_im[None] + pw_im[:LC, :, :, None] * bb_re[None]
    kt = (jnp.einsum('gqn,tgnp->tgpq', c_re, t_re, precision=hp)
          - jnp.einsum('gqn,tgnp->tgpq', c_im, t_im, precision=hp))
    ktm = jax.vmap(bdiag)(kt)
    flat = lambda x: x.reshape(x.shape[0], G * N)
    p1r, p1i = flat(pw_re[LC - 1::-1][:LC]), flat(pw_im[LC - 1::-1][:LC])
    p2r, p2i = flat(pw_re[1:LC + 1]), flat(pw_im[1:LC + 1])
    alr, ali = flat(pw_re[LC:LC + 1]), flat(pw_im[LC:LC + 1])
    return (bmat.astype(BF16), cr.astype(BF16), ci.astype(BF16), ktm.astype(BF16),
            p1r, p1i, p2r, p2i, alr, ali)


def _ssm(z, params, d_skip, batch, seq):
    N = z.shape[0]
    tt = 256
    nb = seq // tt
    CW = SSM_GROUPS * SSM_P
    W = SSM_GROUPS * SSM_N
    LC = SSM_CHUNK
    bmat, cr, ci, ktm, p1r, p1i, p2r, p2i, alr, ali = params
    full = lambda a: pl.BlockSpec(a.shape, lambda b, j, _n=a.ndim: (0,) * _n)
    return pl.pallas_call(
        functools.partial(_ssm_kernel, tt=tt),
        grid=(batch, nb),
        in_specs=[pl.BlockSpec((tt, CW), lambda b, j: (b * nb + j, OFF_SU // CW)),
                  full(bmat), full(cr), full(ci), full(ktm), full(p1r), full(p1i),
                  full(p2r), full(p2i), full(alr), full(ali),
                  pl.BlockSpec((1, CW), lambda b, j: (0, 0))],
        out_specs=pl.BlockSpec((tt, CW), lambda b, j: (b * nb + j, 0)),
        out_shape=jax.ShapeDtypeStruct((N, CW), BF16),
        scratch_shapes=[pltpu.VMEM((1, W), F32), pltpu.VMEM((1, W), F32),
                        pltpu.VMEM((tt // LC, W), F32), pltpu.VMEM((tt // LC, W), F32)],
        compiler_params=_cparams(("parallel", "arbitrary")),
        name="s5_ssm",
    )(z, bmat, cr, ci, ktm, p1r, p1i, p2r, p2i, alr, ali, d_skip)


def _t5_bucket(dist):
    n = jnp.maximum(dist, 0)
    max_exact = REL_BUCKETS // 2
    nf = jnp.maximum(n, 1).astype(F32)
    large = max_exact + (jnp.log(nf / max_exact) / math.log(REL_MAX_DIST / max_exact)
                         * (REL_BUCKETS - max_exact)).astype(jnp.int32)
    large = jnp.minimum(large, REL_BUCKETS - 1)
    return jnp.where(n < max_exact, n, large)


def _bias_kernel(tab_ref, bk_ref, o_ref):
    h = pl.program_id(0)
    bk = bk_ref[...]
    base = tab_ref[REL_BUCKETS - 1, h]
    acc = jnp.full(bk.shape, NEG, F32)
    for k in range(REL_BUCKETS):
        acc = jnp.where(bk == k, (tab_ref[k, h] - base) * LOG2E, acc)
    o_ref[0] = acc


def _bias_tables(rel_table, seq):
    j = jnp.arange(QT)[:, None]
    i = jnp.arange(QT)[None, :]
    d0 = i - j
    b0 = jnp.where(d0 >= 0, _t5_bucket(d0), REL_BUCKETS)
    b1 = _t5_bucket(QT + i - j)
    n = jnp.arange(QT)[:, None]
    t = jnp.arange(seq)[None, :]
    n_cmp = (seq - CMP_LEN) // CMP_STRIDE + 1
    dc = t - (n * CMP_STRIDE + CMP_LEN - 1)
    bc = jnp.where((dc >= 0) & (n < n_cmp), _t5_bucket(dc), REL_BUCKETS)
    nh = rel_table.shape[1]
    tab = rel_table.astype(F32)

    def expand(bk):
        rows, lanes = bk.shape
        return pl.pallas_call(
            _bias_kernel,
            grid=(nh,),
            in_specs=[pl.BlockSpec(memory_space=pltpu.SMEM),
                      pl.BlockSpec((rows, lanes), lambda h: (0, 0))],
            out_specs=pl.BlockSpec((1, rows, lanes), lambda h: (h, 0, 0)),
            out_shape=jax.ShapeDtypeStruct((nh, rows, lanes), F32),
            compiler_params=_cparams(("arbitrary",)),
            name="nsa_bias_tables",
        )(tab, bk.astype(jnp.int32))

    near = expand(jnp.concatenate([b0, b1], axis=0)).reshape(nh, 2, QT, QT)
    return near, expand(bc)


def _head_rms(x, gain):
    return x * lax.rsqrt(jnp.mean(x * x, axis=-1, keepdims=True) + EPS) * gain


def _nsa_prep_kernel(kc_ref, vc_ref, ks_ref, vs_ref, kw_ref, vw_ref, kg_ref, pek_ref, pev_ref,
                     phik_ref, phiv_ref, oks_ref, ovs_ref, okw_ref, ovw_ref, okc_ref, ovc_ref,
                     xpad_ref, *, seq):
    DH = NSA_DH
    kg = kg_ref[...]
    srow = lax.broadcasted_iota(jnp.int32, (seq, DH), 0)
    lane = lax.broadcasted_iota(jnp.int32, (seq, DH), 1)
    onehot = jnp.where(lane == srow // SEL_LEN, 1.0, 0.0).astype(BF16)
    nkt = seq // QT
    npad = WIN // QT
    xpad_ref[seq:seq + CMP_LEN, :] = jnp.zeros((CMP_LEN, DH), F32)

    def value_tiles(v):
        return jnp.concatenate([v, jnp.ones_like(v)], axis=1).T[:NSA_VR].astype(BF16)

    for g in range(NSA_G):
        ls = slice(g * DH, (g + 1) * DH)
        ksn = _head_rms(ks_ref[:, ls], kg).astype(BF16)
        oks_ref[0, g] = jnp.concatenate([ksn, onehot], axis=1)
        okw_ref[0, g, 0:WIN, :] = jnp.zeros((WIN, DH), BF16)
        okw_ref[0, g, WIN:WIN + seq, :] = _head_rms(kw_ref[:, ls], kg).astype(BF16)
        vst = value_tiles(vs_ref[:, ls])
        vwt = value_tiles(vw_ref[:, ls])
        ovw_ref[0, g, 0:npad] = jnp.zeros((npad, NSA_VR, QT), BF16)
        for kt in range(nkt):
            ovs_ref[0, g, kt] = vst[:, kt * QT:(kt + 1) * QT]
            ovw_ref[0, g, npad + kt] = vwt[:, kt * QT:(kt + 1) * QT]
        for src_ref, pe_ref, phi_ref, is_k in ((kc_ref, pek_ref, phik_ref, True),
                                               (vc_ref, pev_ref, phiv_ref, False)):
            xpad_ref[0:seq, :] = src_ref[:, ls]
            acc = jnp.zeros((QT, DH), F32)
            for l in range(CMP_LEN):
                xl = xpad_ref[pl.ds(l, QT, stride=CMP_STRIDE), :] + pe_ref[l:l + 1, :]
                acc = acc + _dot(xl.astype(BF16), phi_ref[l])
            if is_k:
                okc_ref[0, g] = _head_rms(acc, kg).astype(BF16)
            else:
                ovc_ref[0, g] = value_tiles(acc)


def _nsa_prep(z, k_gain, pe_k, pe_v, phi_k, phi_v, batch, seq):
    G, DH = NSA_G, NSA_DH
    KW = G * DH

    def zspec(off):
        return pl.BlockSpec((seq, KW), lambda b: (b, off // KW))

    full = lambda a: pl.BlockSpec(a.shape, lambda b, _n=a.ndim: (0,) * _n)

    def ospec(*dims):
        return pl.BlockSpec((1, G) + dims, lambda b, _n=len(dims): (b, 0) + (0,) * _n)

    def oshape(*dims):
        return jax.ShapeDtypeStruct((batch, G) + dims, BF16)

    outs = [(seq, 2 * DH), (seq // QT, NSA_VR, QT), (seq + WIN, DH), ((seq + WIN) // QT, NSA_VR, QT),
            (QT, DH), (NSA_VR, QT)]
    return pl.pallas_call(
        functools.partial(_nsa_prep_kernel, seq=seq),
        grid=(batch,),
        in_specs=[zspec(OFF_KC), zspec(OFF_VC), zspec(OFF_KS), zspec(OFF_VS), zspec(OFF_KW),
                  zspec(OFF_VW), full(k_gain), full(pe_k), full(pe_v), full(phi_k), full(phi_v)],
        out_specs=[ospec(*d) for d in outs],
        out_shape=[oshape(*d) for d in outs],
        scratch_shapes=[pltpu.VMEM((seq + CMP_LEN, DH), F32)],
        compiler_params=_cparams(("parallel",)),
        name="nsa_kv_prep",
    )(z, z, z, z, z, z, k_gain, pe_k, pe_v, phi_k, phi_v)


def _nsa_kernel(q_ref, gt_ref, ks_ref, vs_ref, kw_ref, vw_ref, kc_ref, vc_ref, cb_ref, nb_ref,
                qg_ref, ov_ref, o_ref):
    DH, HPG, TQ = NSA_DH, NSA_HPG, NSA_TQ
    NR = TQ // QT
    RB = HPG * QT
    R = NR * RB
    NSEL = 32
    nwt = WIN // QT
    qi = pl.program_id(1)
    t0 = qi * TQ

    def cols(r):
        return slice(r * RB, (r + 1) * RB)

    def to_cols(per_head):
        return jnp.concatenate([per_head[h][:, r * QT:(r + 1) * QT] for r in range(NR) for h in range(HPG)],
                               axis=1)

    def krows(x, c):
        return x[c * QT:(c + 1) * QT]

    def vslab(ref, tile0, n):
        return jnp.concatenate([ref[0, 0, tile0 + c] for c in range(n)], axis=1)

    qt = q_ref[...].T
    gain = jnp.concatenate([qg_ref[...]] * NR, axis=1) * (DH ** -0.5 * LOG2E)
    qh = []
    for h in range(HPG):
        x = qt[h * DH:(h + 1) * DH]
        qh.append(x * lax.rsqrt(jnp.mean(x * x, axis=0, keepdims=True) + EPS) * gain)
    qb = to_cols(qh).astype(BF16)
    nb0 = jnp.concatenate([nb_ref[h, 0] for h in range(HPG)], axis=1)
    nb1 = jnp.concatenate([nb_ref[h, 1] for h in range(HPG)], axis=1)

    sc = _dot(kc_ref[0, 0], qb) + to_cols([cb_ref[h] for h in range(HPG)])
    mc = jnp.max(sc, axis=0, keepdims=True)
    ec = jnp.where(sc > 0.5 * NEG, jnp.exp2(sc - mc), 0.0)
    lc = jnp.sum(ec, axis=0, keepdims=True)
    pc = ec / jnp.where(lc > 0.0, lc, 1.0)
    o_cmp = _dot(vc_ref[0, 0], pc.astype(BF16))[:DH]

    psum = jnp.concatenate(
        [sum(pc[:, r * RB + h * QT:r * RB + (h + 1) * QT] for h in range(HPG)) for r in range(NR)], axis=1)
    ovm = ov_ref[...]
    hi, mid, lo = _split3(psum)
    imp = (_dot(ovm, hi) + _dot(ovm, mid) + _dot(ovm, lo))[0:NSEL]
    jrow = lax.broadcasted_iota(jnp.int32, (NSEL, TQ), 0)
    blk = (t0 + lax.broadcasted_iota(jnp.int32, (NSEL, TQ), 1)) // SEL_LEN
    forced = (jrow == 0) | (jrow == blk) | (jrow == blk - 1)
    imp = jnp.where(forced, FORCE_SCORE, imp)
    imp = jnp.where(jrow <= blk, imp, NEG)
    cnt = jnp.zeros((NSEL, TQ), F32)
    for jp in range(NSEL):
        rowv = imp[jp:jp + 1, :]
        beats = (rowv > imp) | ((rowv == imp) & (jrow > jp))
        cnt = cnt + jnp.where(beats, 1.0, 0.0)
    selb = jnp.where(cnt < float(SEL_TOPN), 0.0, NEG)
    qaug = jnp.concatenate([qb, to_cols([selb] * HPG).astype(BF16),
                            jnp.zeros((2 * DH - DH - NSEL, R), BF16)], axis=0)

    def ktile(ref, start, size):
        return ref[0, 0, pl.ds(pl.multiple_of(start, QT), size), :]

    def step(scores, vt, state):
        m_new = [jnp.maximum(state[r][0], jnp.max(scores[r], axis=0, keepdims=True)) for r in range(NR)]
        out = []
        for r in range(NR):
            p = jnp.exp2((scores[r] - m_new[r]).astype(BF16))
            acc = state[r][1] * jnp.exp2(state[r][0] - m_new[r]) + _dot(vt[r], p)
            out.append((m_new[r], acc))
        return out

    scores, vts = [], []
    for r in range(NR):
        s = _dot(ktile(ks_ref, t0, (r + 1) * QT), qaug[:, cols(r)])
        parts = [krows(s, c) for c in range(r + 1)]
        parts[r] = parts[r] + nb0
        if r >= 1:
            parts[r - 1] = parts[r - 1] + nb1
        scores.append(jnp.concatenate(parts, axis=0))
        vts.append(vslab(vs_ref, qi * NR, r + 1))
    state = step(scores, vts, [(jnp.full((1, RB), NEG, F32), jnp.zeros((NSA_VR, RB), F32))] * NR)

    def past_tile(kt, flat):
        kk = ktile(ks_ref, kt * TQ, TQ)
        vt = vslab(vs_ref, kt * NR, NR)
        scores = [_dot(kk, qaug[:, cols(r)]) for r in range(NR)]
        corner = scores[0][TQ - QT:] + jnp.where(kt == qi - 1, nb1, 0.0)
        scores[0] = jnp.concatenate([scores[0][:TQ - QT], corner], axis=0)
        new = step(scores, [vt] * NR, [(flat[2 * r], flat[2 * r + 1]) for r in range(NR)])
        return tuple(x for pair in new for x in pair)

    flat = lax.fori_loop(0, qi, past_tile, tuple(x for pair in state for x in pair))
    acc_s = jnp.concatenate([flat[2 * r + 1] for r in range(NR)], axis=1)

    jj = lax.broadcasted_iota(jnp.int32, (QT, RB), 0)
    ii = lax.broadcasted_iota(jnp.int32, (QT, RB), 1) % QT
    acc_w = []
    for r in range(NR):
        s = _dot(ktile(kw_ref, t0 + r * QT, WIN + QT), qb[:, cols(r)])
        parts = [krows(s, c) for c in range(nwt + 1)]
        parts[nwt] = parts[nwt] + nb0
        parts[nwt - 1] = parts[nwt - 1] + nb1
        parts[0] = jnp.where(jj > ii, parts[0], NEG)
        for c in range(nwt - r):
            parts[c] = jnp.where(qi > 0, parts[c], NEG)
        s = jnp.concatenate(parts, axis=0)
        p = jnp.exp2((s - jnp.max(s, axis=0, keepdims=True)).astype(BF16))
        acc_w.append(_dot(vslab(vw_ref, qi * NR + r, nwt + 1), p))
    acc_w = jnp.concatenate(acc_w, axis=1)

    gt = _sigmoid(gt_ref[...]).T

    def gate_row(br):
        return to_cols([gt[br * HPG + h:br * HPG + h + 1, :] for h in range(HPG)])

    def inv_l(acc):
        l = acc[DH:DH + 1, :]
        return 1.0 / jnp.where(l > 0.0, l, 1.0)

    o = (gate_row(0) * o_cmp + (gate_row(1) * inv_l(acc_s)) * acc_s[:DH]
         + (gate_row(2) * inv_l(acc_w)) * acc_w[:DH])
    o_hd = jnp.concatenate(
        [jnp.concatenate([o[:, r * RB + h * QT:r * RB + (h + 1) * QT] for r in range(NR)], axis=1)
         for h in range(HPG)], axis=0)
    o_ref[...] = o_hd.T.astype(o_ref.dtype)


def _nsa(z, zg, prep, bias, q_gain, ovm, batch, seq):
    N = z.shape[0]
    G, HPG, DH, TQ = NSA_G, NSA_HPG, NSA_DH, NSA_TQ
    nq = seq // TQ
    R = HPG * TQ
    oks, ovs, okw, ovw, okc, ovc = prep
    nbias, cbias = bias
    QW = HPG * DH
    qg = jnp.broadcast_to(q_gain.reshape(DH, 1), (DH, QT))

    def kvspec(a):
        return pl.BlockSpec((1, 1) + a.shape[2:], lambda p, i, _n=a.ndim - 2: (p // G, p % G) + (0,) * _n)

    return pl.pallas_call(
        _nsa_kernel,
        grid=(batch * G, nq),
        in_specs=[pl.BlockSpec((TQ, QW), lambda p, i: ((p // G) * nq + i, OFF_NQ // QW + p % G)),
                  pl.BlockSpec((TQ, 128), lambda p, i: ((p // G) * nq + i, p % G)),
                  kvspec(oks), kvspec(ovs), kvspec(okw), kvspec(ovw), kvspec(okc), kvspec(ovc),
                  pl.BlockSpec((HPG, QT, TQ), lambda p, i: (p % G, 0, i)),
                  pl.BlockSpec((HPG, 2, QT, QT), lambda p, i: (p % G, 0, 0, 0)),
                  pl.BlockSpec((DH, QT), lambda p, i: (0, 0)),
                  pl.BlockSpec((QT, QT), lambda p, i: (0, 0))],
        out_specs=pl.BlockSpec((TQ, QW), lambda p, i: ((p // G) * nq + i, p % G)),
        out_shape=jax.ShapeDtypeStruct((N, G * QW), BF16),
        compiler_params=_cparams(("parallel", "arbitrary")),
        name="nsa_attention",
    )(z, zg, oks, ovs, okw, ovw, okc, ovc, cbias, nbias, qg, ovm)


def _merge_kernel(h_ref, mod_ref, za_ref, zb_ref, zc_ref, oa_ref, ys_ref, oc_ref,
                  wa_ref, wb_ref, wc_ref, wo_ref, o_ref):
    D = h_ref.shape[1]
    ya = _dot(oa_ref[...], wa_ref[...])
    zz = _dot(ys_ref[...], wb_ref[...])
    yb = zz[:, :D] * _sigmoid(zz[:, D:])
    yc = _dot(oc_ref[...], wc_ref[...])
    merged = _sigmoid(za_ref[...]) * ya + _sigmoid(zb_ref[...]) * yb + _sigmoid(zc_ref[...]) * yc
    o_ref[...] = h_ref[...] + mod_ref[0, 5:6, :] * _dot(merged.astype(BF16), wo_ref[...])


def _merge(h, mod, z, oa, ys, oc, wa, wb, wc, wo, seq):
    N, D = h.shape
    tm = 256
    tpb = seq // tm
    row = lambda w: pl.BlockSpec((tm, w), lambda i: (i, 0))
    res = lambda a: pl.BlockSpec(a.shape, lambda i: (0, 0), pipeline_mode=pl.Buffered(1))
    return pl.pallas_call(
        _merge_kernel,
        grid=(N // tm,),
        in_specs=[row(D),
                  pl.BlockSpec((1, 9, D), lambda i: (i // tpb, 0, 0)),
                  pl.BlockSpec((tm, D), lambda i: (i, OFF_ZA // D)),
                  pl.BlockSpec((tm, D), lambda i: (i, OFF_ZB // D)),
                  pl.BlockSpec((tm, D), lambda i: (i, OFF_ZC // D)),
                  row(oa.shape[1]), row(ys.shape[1]), row(oc.shape[1]),
                  res(wa), res(wb), res(wc), res(wo)],
        out_specs=row(D),
        out_shape=jax.ShapeDtypeStruct((N, D), F32),
        compiler_params=_cparams(("parallel",)),
        name="mix_merge",
    )(h, mod, z, z, z, oa, ys, oc, wa, wb, wc, wo)


def _split_w_in(w):
    gate0 = 4 * 512 + 512 + 1024 + 6 * 256
    ngate = 3 * NSA_G * NSA_HPG
    L, D = w.shape[:2]
    wb = w.astype(BF16)
    gw = wb[:, :, gate0:gate0 + ngate].reshape(L, D, 3, NSA_G, NSA_HPG)
    gw = gw.transpose(0, 1, 3, 2, 4).reshape(L, D, NSA_G, 3 * NSA_HPG)
    gw = jnp.pad(gw, ((0, 0), (0, 0), (0, 0), (0, 128 - 3 * NSA_HPG))).reshape(L, D, NSA_G * 128)
    return wb, wb[:, :, gate0 + ngate:], gw


def _overlap_matrix():
    j = jnp.arange(QT)[:, None]
    n = jnp.arange(QT)[None, :]
    st = n * CMP_STRIDE
    ov = (st < j * SEL_LEN + SEL_LEN) & (st + CMP_LEN > j * SEL_LEN)
    return jnp.where(ov, 1.0, 0.0).astype(BF16)


def kernel(x, c, ada_w, ada_b, norm_g, ffn1_wi, ffn1_wo, ffn2_wi, ffn2_wo, w_in, hg_lb_logits,
           hg_onorm, hg_proj, ssm_a_re, ssm_a_im, ssm_log_dt, ssm_b_re, ssm_b_im, ssm_c_re,
           ssm_c_im, ssm_d, ssm_glu_w, nsa_q_gain, nsa_k_gain, nsa_pe_k, nsa_pe_v, nsa_phi_k,
           nsa_phi_v, nsa_proj, rel_table, w_out):
    B, S, D = x.shape
    L = ada_w.shape[0]
    N = B * S
    assert S % 512 == 0 and S // SEL_LEN == 32 and S // QT == 16
    lb_cum = jnp.cumsum(jax.nn.softmax(hg_lb_logits.astype(F32), axis=0), axis=0)
    lower_bounds = lb_cum - lb_cum[0:1]
    mods = _mods(c, ada_w, ada_b).reshape(L, B, 9, D)
    bias = _bias_tables(rel_table, S)
    ovm = _overlap_matrix()
    w_mix, w_z, w_gate = _split_w_in(w_in)
    h = x.reshape(N, D)
    for l in range(L):
        mod = mods[l]
        h = _ffn(h, mod, norm_g[l, 0:1], ffn1_wi, ffn1_wo, l, 0, S)
        z, zg = _win(h, mod, norm_g[l, 1:2], w_mix, w_z, w_gate, l, S)
        oa = _hgrn(z, lower_bounds[l:l + 1], hg_onorm[l:l + 1], B, S)
        sp = _ssm_params(ssm_a_re[l], ssm_a_im[l], ssm_log_dt[l], ssm_b_re[l], ssm_b_im[l],
                         ssm_c_re[l], ssm_c_im[l])
        ys = _ssm(z, sp, ssm_d[l:l + 1], B, S)
        prep = _nsa_prep(z, nsa_k_gain[l:l + 1], nsa_pe_k[l], nsa_pe_v[l],
                         nsa_phi_k[l].astype(BF16), nsa_phi_v[l].astype(BF16), B, S)
        oc = _nsa(z, zg, prep, bias, nsa_q_gain[l:l + 1], ovm, B, S)
        h = _merge(h, mod, z, oa, ys, oc, hg_proj[l].astype(BF16), ssm_glu_w[l].astype(BF16),
                   nsa_proj[l].astype(BF16), w_out[l].astype(BF16), S)
        h = _ffn(h, mod, norm_g[l, 2:3], ffn2_wi, ffn2_wo, l, 6, S)
    return h.reshape(B, S, D)
```

```python
import functools
import math

import jax
import jax.numpy as jnp
from jax import lax
from jax.experimental import pallas as pl
from jax.experimental.pallas import tpu as pltpu

F32 = jnp.float32
BF16 = jnp.bfloat16

EPS = 1e-6
NEG = -1e30
LOG2E = 1.4426950408889634
FORCE_SCORE = 1e4

HG_HEADS, HG_D, HG_CHUNK = 4, 128, 16
SSM_GROUPS, SSM_P, SSM_N, SSM_CHUNK = 32, 16, 64, 8
NSA_G, NSA_HPG, NSA_DH = 4, 4, 64
CMP_LEN, CMP_STRIDE, SEL_LEN, SEL_TOPN, WIN = 32, 16, 64, 16, 512
REL_BUCKETS, REL_MAX_DIST = 32, 128
QT = 128
NSA_TQ = 512
NSA_VR = NSA_DH + 16

VMEM_LIMIT = 60 * 1024 * 1024

OFF_ZA, OFF_ZB, OFF_ZC = 0, 2048, 4096
OFF_HQ, OFF_HF, OFF_HI, OFF_HG = 6144, 6656, 7168, 7680
OFF_SU, OFF_NQ = 8192, 8704
OFF_KC, OFF_VC, OFF_KS, OFF_VS, OFF_KW, OFF_VW = 9728, 9984, 10240, 10496, 10752, 11008
Z_WIDTH = 11264


def _cparams(sem):
    return pltpu.CompilerParams(dimension_semantics=sem, vmem_limit_bytes=VMEM_LIMIT)


def _dot(a, b):
    return jnp.dot(a, b, preferred_element_type=F32)


def _dot_nt(a, b):
    return lax.dot_general(a, b, (((1,), (1,)), ((), ())), preferred_element_type=F32)


def _dot_tn(a, b):
    return lax.dot_general(a, b, (((0,), (0,)), ((), ())), preferred_element_type=F32)


def _sigmoid(x):
    return 1.0 / (1.0 + jnp.exp(-x))


def _split3(x):
    hi = x.astype(BF16)
    r = x - hi.astype(F32)
    mid = r.astype(BF16)
    lo = (r - mid.astype(F32)).astype(BF16)
    return hi, mid, lo


def _norm_mod(x, gain, shift, scale):
    ms = jnp.mean(x * x, axis=-1, keepdims=True)
    y = x * lax.rsqrt(ms + EPS) * gain
    return y * (1.0 + scale) + shift


def _mod_kernel(c_ref, w_ref, b_ref, o_ref):
    c = c_ref[...]
    ca = (c * _sigmoid(c)).astype(BF16)
    o_ref[0] = _dot(ca, w_ref[0].astype(BF16)) + b_ref[0]


def _mods(c, ada_w, ada_b):
    L, D, W = ada_w.shape
    B = c.shape[0]
    tn = 1024
    return pl.pallas_call(
        _mod_kernel,
        grid=(L, W // tn),
        in_specs=[pl.BlockSpec((B, D), lambda l, j: (0, 0)),
                  pl.BlockSpec((1, D, tn), lambda l, j: (l, 0, j)),
                  pl.BlockSpec((1, 1, tn), lambda l, j: (l, 0, j))],
        out_specs=pl.BlockSpec((1, B, tn), lambda l, j: (l, 0, j)),
        out_shape=jax.ShapeDtypeStruct((L, B, W), F32),
        compiler_params=_cparams(("parallel", "parallel")),
        name="adaln_mod",
    )(c, ada_w, ada_b.reshape(L, 1, W))


def _ffn_kernel(h_ref, mod_ref, g_ref, wi1_ref, wi2_ref, wo_ref, o_ref, u_s, *, k0, nf):
    f = pl.program_id(1)

    @pl.when(f == 0)
    def _():
        u = _norm_mod(h_ref[...], g_ref[...], mod_ref[0, k0:k0 + 1, :], mod_ref[0, k0 + 1:k0 + 2, :])
        u_s[...] = u.astype(BF16)
        o_ref[...] = jnp.zeros_like(o_ref)

    u = u_s[...]
    a1 = _dot(u, wi1_ref[0].astype(BF16))
    a2 = _dot(u, wi2_ref[0].astype(BF16))
    act = (a1 * _sigmoid(a1) * a2).astype(BF16)
    o_ref[...] += _dot(act, wo_ref[0].astype(BF16))

    @pl.when(f == nf - 1)
    def _():
        o_ref[...] = h_ref[...] + (0.5 * mod_ref[0, k0 + 2:k0 + 3, :]) * o_ref[...]


def _ffn(h, mod, gain, wi, wo, layer, k0, seq):
    N, D = h.shape
    dff = wo.shape[1]
    tm, tf = 1024, 256
    nf = dff // tf
    tpb = seq // tm
    return pl.pallas_call(
        functools.partial(_ffn_kernel, k0=k0, nf=nf),
        grid=(N // tm, nf),
        in_specs=[pl.BlockSpec((tm, D), lambda i, f: (i, 0)),
                  pl.BlockSpec((1, 9, D), lambda i, f: (i // tpb, 0, 0)),
                  pl.BlockSpec((1, D), lambda i, f: (0, 0)),
                  pl.BlockSpec((1, D, tf), lambda i, f: (layer, 0, f)),
                  pl.BlockSpec((1, D, tf), lambda i, f: (layer, 0, f + nf)),
                  pl.BlockSpec((1, tf, D), lambda i, f: (layer, f, 0))],
        out_specs=pl.BlockSpec((tm, D), lambda i, f: (i, 0)),
        out_shape=jax.ShapeDtypeStruct((N, D), F32),
        scratch_shapes=[pltpu.VMEM((tm, D), BF16)],
        compiler_params=_cparams(("parallel", "arbitrary")),
        name="ffn",
    )(h, mod, gain, wi, wi, wo)


def _win_kernel(h_ref, mod_ref, g_ref, wm_ref, wz_ref, wg_ref, z_ref, zg_ref, u_s, *, nz):
    j = pl.program_id(1)

    @pl.when(j == 0)
    def _():
        u = _norm_mod(h_ref[...], g_ref[...], mod_ref[0, 3:4, :], mod_ref[0, 4:5, :])
        ub = u.astype(BF16)
        u_s[...] = ub
        zg_ref[...] = _dot(ub, wg_ref[0])

    @pl.when(j < nz)
    def _():
        z_ref[...] = _dot(u_s[...], wz_ref[0])

    @pl.when(j >= nz)
    def _():
        z_ref[...] = _dot(u_s[...], wm_ref[0])


def _win(h, mod, gain, w_mix, w_z, w_gate, layer, seq):
    N, D = h.shape
    tm, tn = 1024, 1024
    tpb = seq // tm
    GW = w_gate.shape[2]
    nz = w_z.shape[2] // tn
    nmix = w_mix.shape[2] // tn
    return pl.pallas_call(
        functools.partial(_win_kernel, nz=nz),
        grid=(N // tm, nz + nmix),
        in_specs=[pl.BlockSpec((tm, D), lambda i, j: (i, 0)),
                  pl.BlockSpec((1, 9, D), lambda i, j: (i // tpb, 0, 0)),
                  pl.BlockSpec((1, D), lambda i, j: (0, 0)),
                  pl.BlockSpec((1, D, tn), lambda i, j: (layer, 0, jnp.maximum(j - nz, 0))),
                  pl.BlockSpec((1, D, tn), lambda i, j: (layer, 0, jnp.minimum(j, nz - 1))),
                  pl.BlockSpec((1, D, GW), lambda i, j: (layer, 0, 0))],
        out_specs=[pl.BlockSpec((tm, tn), lambda i, j: (i, j)),
                   pl.BlockSpec((tm, GW), lambda i, j: (i, 0))],
        out_shape=[jax.ShapeDtypeStruct((N, Z_WIDTH), F32),
                   jax.ShapeDtypeStruct((N, GW), F32)],
        scratch_shapes=[pltpu.VMEM((tm, D), BF16)],
        compiler_params=_cparams(("parallel", "arbitrary")),
        name="in_proj",
    )(h, mod, gain, w_mix, w_z, w_gate)


def _hgrn_kernel(q_ref, f_ref, i_ref, g_ref, lb_ref, on_ref, o_ref, st_ref, *, tb):
    C, H, DK = HG_CHUNK, HG_HEADS, HG_D
    nc = tb // C
    W = H * DK

    @pl.when(pl.program_id(1) == 0)
    def _():
        st_ref[...] = jnp.zeros_like(st_ref)

    q = q_ref[...]
    qs = q * _sigmoid(q)
    x = f_ref[...]
    iv = i_ref[...]
    lb = lb_ref[...]
    sp = jnp.log1p(jnp.exp(-jnp.abs(x)))
    lsig = jnp.minimum(x, 0.0) - sp
    a = jnp.log(jnp.maximum(lb, 1e-38))
    bterm = jnp.log1p(-lb) + lsig
    lae = jnp.maximum(a, bterm) + jnp.log1p(jnp.exp(-jnp.abs(a - bterm)))
    log_f = jnp.where(lb > 0.0, lae, bterm)
    k = (1.0 - lb) * jnp.exp(jnp.minimum(-x, 0.0) - sp)

    r = lax.broadcasted_iota(jnp.int32, (tb, tb), 0)
    cidx = lax.broadcasted_iota(jnp.int32, (tb, tb), 1)
    lmat = jnp.where(((r // C) == (cidx // C)) & (cidx <= r), 1.0, 0.0).astype(BF16)
    hi, mid, lo = _split3(log_f)
    b = _dot(lmat, hi) + _dot(lmat, mid) + _dot(lmat, lo)

    qe = (qs * jnp.exp(b)).astype(BF16)
    b3 = b.reshape(nc, C, W)
    bend = b3[:, C - 1:C, :]
    kdec = (k.reshape(nc, C, W) * jnp.exp(bend - b3)).reshape(tb, W).astype(BF16)
    ebend = jnp.exp(bend)
    ib = iv.astype(BF16)
    o_heads = []
    for h in range(H):
        ls = slice(h * DK, (h + 1) * DK)
        uts = [_dot_tn(ib[c * C:(c + 1) * C, ls], kdec[c * C:(c + 1) * C, ls]) for c in range(nc)]
        st = st_ref[h]
        sts = []
        for c in range(nc):
            sts.append(st.astype(BF16))
            st = st * ebend[c][:, ls] + uts[c]
        st_ref[h] = st
        o_heads.append(jnp.concatenate(
            [_dot_nt(qe[c * C:(c + 1) * C, ls], sts[c]) for c in range(nc)], axis=0))
    o = jnp.concatenate(o_heads, axis=1)

    b2 = b * LOG2E
    tmod = lax.broadcasted_iota(jnp.int32, (tb, W), 0) % C
    rr = lax.broadcasted_iota(jnp.int32, (W, W), 0) // DK
    cc = lax.broadcasted_iota(jnp.int32, (W, W), 1) // DK
    ones_bd = jnp.where(rr == cc, 1.0, 0.0).astype(BF16)
    SUB = 8

    def shifted_terms(qx, bx, kx, bsrc, isrc, shift, rowmod):
        if shift == 0:
            p = qx * kx * jnp.exp2(bx - bsrc)
            isd = isrc
        else:
            kd = pltpu.roll(kx, shift, 0)
            bd = pltpu.roll(bsrc, shift, 0)
            isd = pltpu.roll(isrc, shift, 0)
            p = jnp.where(rowmod >= shift, qx * kd * jnp.exp2(bx - bd), 0.0)
        return _dot(p.astype(BF16), ones_bd) * isd

    for d in range(SUB):
        if d == 0:
            o = o + _dot((qs * k).astype(BF16), ones_bd) * iv
        else:
            o = o + shifted_terms(qs, b2, k, b2, iv, d, tmod)

    def group(x, i):
        return x.reshape(nc, C // SUB, SUB, W)[:, i].reshape(nc * SUB, W)

    rmod = lax.broadcasted_iota(jnp.int32, (nc * SUB, W), 0) % SUB
    q_hi, b_hi = group(qs, 1), group(b2, 1)
    k_lo, b_lo, i_lo = group(k, 0), group(b2, 0), group(iv, 0)
    o_hi = group(o, 1)
    for d in range(SUB, C):
        o_hi = o_hi + shifted_terms(q_hi, b_hi, k_lo, b_lo, i_lo, d - SUB, rmod)
    o = jnp.stack([group(o, 0).reshape(nc, SUB, W), o_hi.reshape(nc, SUB, W)], axis=1).reshape(tb, W)

    g = g_ref[...]
    gs = g * _sigmoid(g)
    onw = on_ref[...]
    outs = []
    for h in range(H):
        ls = slice(h * DK, (h + 1) * DK)
        oh = o[:, ls]
        outs.append(oh * lax.rsqrt(jnp.mean(oh * oh, axis=-1, keepdims=True) + EPS) * onw * gs[:, ls])
    o_ref[...] = jnp.concatenate(outs, axis=1).astype(o_ref.dtype)


def _hgrn(z, lb, onorm, batch, seq):
    N = z.shape[0]
    tb = 256
    nb = seq // tb
    W = HG_HEADS * HG_D

    def zspec(off):
        return pl.BlockSpec((tb, W), lambda b, j: (b * nb + j, off // W))

    return pl.pallas_call(
        functools.partial(_hgrn_kernel, tb=tb),
        grid=(batch, nb),
        in_specs=[zspec(OFF_HQ), zspec(OFF_HF), zspec(OFF_HI), zspec(OFF_HG),
                  pl.BlockSpec((1, W), lambda b, j: (0, 0)),
                  pl.BlockSpec((1, HG_D), lambda b, j: (0, 0))],
        out_specs=pl.BlockSpec((tb, W), lambda b, j: (b * nb + j, 0)),
        out_shape=jax.ShapeDtypeStruct((N, W), BF16),
        scratch_shapes=[pltpu.VMEM((HG_HEADS, HG_D, HG_D), F32)],
        compiler_params=_cparams(("parallel", "arbitrary")),
        name="hgrn2",
    )(z, z, z, z, lb, onorm)


def _ssm_kernel(u_ref, bm_ref, cr_ref, ci_ref, kt_ref, p1r_ref, p1i_ref, p2r_ref, p2i_ref,
                alr_ref, ali_ref, d_ref, o_ref, xr_ref, xi_ref, xpr_ref, xpi_ref, *, tt):
    LC = SSM_CHUNK
    nc = tt // LC
    W = SSM_GROUPS * SSM_N

    @pl.when(pl.program_id(1) == 0)
    def _():
        xr_ref[...] = jnp.zeros_like(xr_ref)
        xi_ref[...] = jnp.zeros_like(xi_ref)

    u = u_ref[...]
    ub = u.astype(BF16)
    bu = _dot(ub, bm_ref[...])
    bur = bu[:, :W].reshape(nc, LC, W)
    bui = bu[:, W:].reshape(nc, LC, W)
    p1r, p1i = p1r_ref[...], p1i_ref[...]
    vr = jnp.sum(bur * p1r - bui * p1i, axis=1)
    vi = jnp.sum(bur * p1i + bui * p1r, axis=1)

    alr, ali = alr_ref[...], ali_ref[...]
    xr, xi = xr_ref[...], xi_ref[...]
    for c in range(nc):
        xpr_ref[c:c + 1, :] = xr
        xpi_ref[c:c + 1, :] = xi
        nr = alr * xr - ali * xi + vr[c:c + 1, :]
        ni = alr * xi + ali * xr + vi[c:c + 1, :]
        xr, xi = nr, ni
    xr_ref[...] = xr
    xi_ref[...] = xi

    xpr = xpr_ref[...][:, None, :]
    xpi = xpi_ref[...][:, None, :]
    p2r, p2i = p2r_ref[...], p2i_ref[...]
    zr = (p2r * xpr - p2i * xpi).reshape(tt, W).astype(BF16)
    zi = (p2r * xpi + p2i * xpr).reshape(tt, W).astype(BF16)
    y = _dot(zr, cr_ref[...]) + _dot(zi, ci_ref[...])

    tmod = lax.broadcasted_iota(jnp.int32, u.shape, 0) % LC
    for tau in range(LC):
        if tau == 0:
            us = ub
        else:
            us = jnp.where(tmod >= tau, pltpu.roll(u, tau, 0), 0.0).astype(BF16)
        y = y + _dot(us, kt_ref[tau])
    y = y + d_ref[...] * u
    g = 0.5 * y * (1.0 + jnp.tanh(0.7978845608028654 * (y + 0.044715 * (y * y * y))))
    o_ref[...] = g.astype(o_ref.dtype)


def _ssm_params(a_re, a_im, log_dt, b_re, b_im, c_re, c_im):
    G, P, N, LC = SSM_GROUPS, SSM_P, SSM_N, SSM_CHUNK
    hp = lax.Precision.HIGHEST
    a_re = jnp.minimum(a_re.astype(F32), -1e-4)
    a_im = a_im.astype(F32)
    dt = jnp.exp(log_dt.astype(F32))[:, None]
    mag = jnp.exp(dt * a_re)
    ab_re, ab_im = mag * jnp.cos(dt * a_im), mag * jnp.sin(dt * a_im)
    den = a_re * a_re + a_im * a_im
    nr = ab_re - 1.0
    z_re = (nr * a_re + ab_im * a_im) / den
    z_im = (ab_im * a_re - nr * a_im) / den
    b_re, b_im = b_re.astype(F32), b_im.astype(F32)
    bb_re = z_re[..., None] * b_re - z_im[..., None] * b_im
    bb_im = z_re[..., None] * b_im + z_im[..., None] * b_re
    kk = jnp.arange(LC + 1, dtype=F32)[:, None, None]
    pm = jnp.exp(kk * dt * a_re)
    pw_re, pw_im = pm * jnp.cos(kk * dt * a_im), pm * jnp.sin(kk * dt * a_im)
    def bdiag(src):
        a, b = src.shape[1:]
        x = jnp.broadcast_to(src.reshape(G * a, 1, b), (G * a, G, b)).reshape(G * a, G * b)
        rg = jnp.arange(G * a)[:, None] // a
        cg = jnp.arange(G * b)[None, :] // b
        return jnp.where(rg == cg, x, 0.0)

    bmat = jnp.concatenate([bdiag(bb_re.transpose(0, 2, 1)), bdiag(bb_im.transpose(0, 2, 1))], axis=1)
    c_re, c_im = c_re.astype(F32), c_im.astype(F32)
    cr = bdiag(c_re.transpose(0, 2, 1))
    ci = -bdiag(c_im.transpose(0, 2, 1))
    t_re = pw_re[:LC, :, :, None] * bb_re[None] - pw_im[:LC, :, :, None] * bb_im[None]
    t_im = pw_re[:LC, :, :, None] * bb_im[None] + pw_im[:LC, :, :, None] * bb_re[None]
    kt = (jnp.einsum('gqn,tgnp->tgpq', c_re, t_re, precision=hp)
          - jnp.einsum('gqn,tgnp->tgpq', c_im, t_im, precision=hp))
    ktm = jax.vmap(bdiag)(kt)
    flat = lambda x: x.reshape(x.shape[0], G * N)
    p1r, p1i = flat(pw_re[LC - 1::-1][:LC]), flat(pw_im[LC - 1::-1][:LC])
    p2r, p2i = flat(pw_re[1:LC + 1]), flat(pw_im[1:LC + 1])
    alr, ali = flat(pw_re[LC:LC + 1]), flat(pw_im[LC:LC + 1])
    return (bmat.astype(BF16), cr.astype(BF16), ci.astype(BF16), ktm.astype(BF16),
            p1r, p1i, p2r, p2i, alr, ali)


def _ssm(z, params, d_skip, batch, seq):
    N = z.shape[0]
    tt = 256
    nb = seq // tt
    CW = SSM_GROUPS * SSM_P
    W = SSM_GROUPS * SSM_N
    LC = SSM_CHUNK
    bmat, cr, ci, ktm, p1r, p1i, p2r, p2i, alr, ali = params
    full = lambda a: pl.BlockSpec(a.shape, lambda b, j, _n=a.ndim: (0,) * _n)
    return pl.pallas_call(
        functools.partial(_ssm_kernel, tt=tt),
        grid=(batch, nb),
        in_specs=[pl.BlockSpec((tt, CW), lambda b, j: (b * nb + j, OFF_SU // CW)),
                  full(bmat), full(cr), full(ci), full(ktm), full(p1r), full(p1i),
                  full(p2r), full(p2i), full(alr), full(ali),
                  pl.BlockSpec((1, CW), lambda b, j: (0, 0))],
        out_specs=pl.BlockSpec((tt, CW), lambda b, j: (b * nb + j, 0)),
        out_shape=jax.ShapeDtypeStruct((N, CW), BF16),
        scratch_shapes=[pltpu.VMEM((1, W), F32), pltpu.VMEM((1, W), F32),
                        pltpu.VMEM((tt // LC, W), F32), pltpu.VMEM((tt // LC, W), F32)],
        compiler_params=_cparams(("parallel", "arbitrary")),
        name="s5_ssm",
    )(z, bmat, cr, ci, ktm, p1r, p1i, p2r, p2i, alr, ali, d_skip)


def _t5_bucket(dist):
    n = jnp.maximum(dist, 0)
    max_exact = REL_BUCKETS // 2
    nf = jnp.maximum(n, 1).astype(F32)
    large = max_exact + (jnp.log(nf / max_exact) / math.log(REL_MAX_DIST / max_exact)
                         * (REL_BUCKETS - max_exact)).astype(jnp.int32)
    large = jnp.minimum(large, REL_BUCKETS - 1)
    return jnp.where(n < max_exact, n, large)


def _bias_kernel(tab_ref, bk_ref, o_ref):
    h = pl.program_id(0)
    bk = bk_ref[...]
    base = tab_ref[REL_BUCKETS - 1, h]
    acc = jnp.full(bk.shape, NEG, F32)
    for k in range(REL_BUCKETS):
        acc = jnp.where(bk == k, (tab_ref[k, h] - base) * LOG2E, acc)
    o_ref[0] = acc


def _bias_tables(rel_table, seq):
    j = jnp.arange(QT)[:, None]
    i = jnp.arange(QT)[None, :]
    d0 = i - j
    b0 = jnp.where(d0 >= 0, _t5_bucket(d0), REL_BUCKETS)
    b1 = _t5_bucket(QT + i - j)
    n = jnp.arange(QT)[:, None]
    t = jnp.arange(seq)[None, :]
    n_cmp = (seq - CMP_LEN) // CMP_STRIDE + 1
    dc = t - (n * CMP_STRIDE + CMP_LEN - 1)
    bc = jnp.where((dc >= 0) & (n < n_cmp), _t5_bucket(dc), REL_BUCKETS)
    nh = rel_table.shape[1]
    tab = rel_table.astype(F32)

    def expand(bk):
        rows, lanes = bk.shape
        return pl.pallas_call(
            _bias_kernel,
            grid=(nh,),
            in_specs=[pl.BlockSpec(memory_space=pltpu.SMEM),
                      pl.BlockSpec((rows, lanes), lambda h: (0, 0))],
            out_specs=pl.BlockSpec((1, rows, lanes), lambda h: (h, 0, 0)),
            out_shape=jax.ShapeDtypeStruct((nh, rows, lanes), F32),
            compiler_params=_cparams(("arbitrary",)),
            name="nsa_bias_tables",
        )(tab, bk.astype(jnp.int32))

    near = expand(jnp.concatenate([b0, b1], axis=0)).reshape(nh, 2, QT, QT)
    return near, expand(bc)


def _head_rms(x, gain):
    return x * lax.rsqrt(jnp.mean(x * x, axis=-1, keepdims=True) + EPS) * gain


def _nsa_prep_kernel(kc_ref, vc_ref, ks_ref, vs_ref, kw_ref, vw_ref, kg_ref, pek_ref, pev_ref,
                     phik_ref, phiv_ref, oks_ref, ovs_ref, okw_ref, ovw_ref, okc_ref, ovc_ref,
                     xpad_ref, *, seq):
    DH = NSA_DH
    kg = kg_ref[...]
    srow = lax.broadcasted_iota(jnp.int32, (seq, DH), 0)
    lane = lax.broadcasted_iota(jnp.int32, (seq, DH), 1)
    onehot = jnp.where(lane == srow // SEL_LEN, 1.0, 0.0).astype(BF16)
    nkt = seq // QT
    npad = WIN // QT
    xpad_ref[seq:seq + CMP_LEN, :] = jnp.zeros((CMP_LEN, DH), F32)

    def value_tiles(v):
        return jnp.concatenate([v, jnp.ones_like(v)], axis=1).T[:NSA_VR].astype(BF16)

    for g in range(NSA_G):
        ls = slice(g * DH, (g + 1) * DH)
        ksn = _head_rms(ks_ref[:, ls], kg).astype(BF16)
        oks_ref[0, g] = jnp.concatenate([ksn, onehot], axis=1)
        okw_ref[0, g, 0:WIN, :] = jnp.zeros((WIN, DH), BF16)
        okw_ref[0, g, WIN:WIN + seq, :] = _head_rms(kw_ref[:, ls], kg).astype(BF16)
        vst = value_tiles(vs_ref[:, ls])
        vwt = value_tiles(vw_ref[:, ls])
        ovw_ref[0, g, 0:npad] = jnp.zeros((npad, NSA_VR, QT), BF16)
        for kt in range(nkt):
            ovs_ref[0, g, kt] = vst[:, kt * QT:(kt + 1) * QT]
            ovw_ref[0, g, npad + kt] = vwt[:, kt * QT:(kt + 1) * QT]
        for src_ref, pe_ref, phi_ref, is_k in ((kc_ref, pek_ref, phik_ref, True),
                                               (vc_ref, pev_ref, phiv_ref, False)):
            xpad_ref[0:seq, :] = src_ref[:, ls]
            acc = jnp.zeros((QT, DH), F32)
            for l in range(CMP_LEN):
                xl = xpad_ref[pl.ds(l, QT, stride=CMP_STRIDE), :] + pe_ref[l:l + 1, :]
                acc = acc + _dot(xl.astype(BF16), phi_ref[l])
            if is_k:
                okc_ref[0, g] = _head_rms(acc, kg).astype(BF16)
            else:
                ovc_ref[0, g] = value_tiles(acc)


def _nsa_prep(z, k_gain, pe_k, pe_v, phi_k, phi_v, batch, seq):
    G, DH = NSA_G, NSA_DH
    KW = G * DH

    def zspec(off):
        return pl.BlockSpec((seq, KW), lambda b: (b, off // KW))

    full = lambda a: pl.BlockSpec(a.shape, lambda b, _n=a.ndim: (0,) * _n)

    def ospec(*dims):
        return pl.BlockSpec((1, G) + dims, lambda b, _n=len(dims): (b, 0) + (0,) * _n)

    def oshape(*dims):
        return jax.ShapeDtypeStruct((batch, G) + dims, BF16)

    outs = [(seq, 2 * DH), (seq // QT, NSA_VR, QT), (seq + WIN, DH), ((seq + WIN) // QT, NSA_VR, QT),
            (QT, DH), (NSA_VR, QT)]
    return pl.pallas_call(
        functools.partial(_nsa_prep_kernel, seq=seq),
        grid=(batch,),
        in_specs=[zspec(OFF_KC), zspec(OFF_VC), zspec(OFF_KS), zspec(OFF_VS), zspec(OFF_KW),
                  zspec(OFF_VW), full(k_gain), full(pe_k), full(pe_v), full(phi_k), full(phi_v)],
        out_specs=[ospec(*d) for d in outs],
        out_shape=[oshape(*d) for d in outs],
        scratch_shapes=[pltpu.VMEM((seq + CMP_LEN, DH), F32)],
        compiler_params=_cparams(("parallel",)),
        name="nsa_kv_prep",
    )(z, z, z, z, z, z, k_gain, pe_k, pe_v, phi_k, phi_v)


def _nsa_kernel(q_ref, gt_ref, ks_ref, vs_ref, kw_ref, vw_ref, kc_ref, vc_ref, cb_ref, nb_ref,
                qg_ref, ov_ref, o_ref):
    DH, HPG, TQ = NSA_DH, NSA_HPG, NSA_TQ
    NR = TQ // QT
    RB = HPG * QT
    R = NR * RB
    NSEL = 32
    nwt = WIN // QT
    qi = pl.program_id(1)
    t0 = qi * TQ

    def cols(r):
        return slice(r * RB, (r + 1) * RB)

    def to_cols(per_head):
        return jnp.concatenate([per_head[h][:, r * QT:(r + 1) * QT] for r in range(NR) for h in range(HPG)],
                               axis=1)

    def krows(x, c):
        return x[c * QT:(c + 1) * QT]

    def vslab(ref, tile0, n):
        return jnp.concatenate([ref[0, 0, tile0 + c] for c in range(n)], axis=1)

    def ktile(ref, start, size):
        return ref[0, 0, pl.ds(pl.multiple_of(start, QT), size), :]

    qt = q_ref[...].T
    gain = jnp.concatenate([qg_ref[...]] * NR, axis=1) * (DH ** -0.5 * LOG2E)
    qh = []
    for h in range(HPG):
        x = qt[h * DH:(h + 1) * DH]
        qh.append(x * lax.rsqrt(jnp.mean(x * x, axis=0, keepdims=True) + EPS) * gain)
    qb = to_cols(qh).astype(BF16)
    nb0 = jnp.concatenate([nb_ref[h, 0] for h in range(HPG)], axis=1)
    nb1 = jnp.concatenate([nb_ref[h, 1] for h in range(HPG)], axis=1)

    jj = lax.broadcasted_iota(jnp.int32, (QT, RB), 0)
    ii = lax.broadcasted_iota(jnp.int32, (QT, RB), 1) % QT
    acc_w = []
    for r in range(NR):
        s = _dot(ktile(kw_ref, t0 + r * QT, WIN + QT), qb[:, cols(r)])
        parts = [krows(s, c) for c in range(nwt + 1)]
        parts[nwt] = parts[nwt] + nb0
        parts[nwt - 1] = parts[nwt - 1] + nb1
        parts[0] = jnp.where(jj > ii, parts[0], NEG)
        for c in range(nwt - r):
            parts[c] = jnp.where(qi > 0, parts[c], NEG)
        s = jnp.concatenate(parts, axis=0)
        p = jnp.exp2(s - jnp.max(s, axis=0, keepdims=True)).astype(BF16)
        acc_w.append(_dot(vslab(vw_ref, qi * NR + r, nwt + 1), p))
    acc_w = jnp.concatenate(acc_w, axis=1)

    sc = _dot(kc_ref[0, 0], qb) + to_cols([cb_ref[h] for h in range(HPG)])
    mc = jnp.max(sc, axis=0, keepdims=True)
    ec = jnp.where(sc > 0.5 * NEG, jnp.exp2(sc - mc), 0.0)
    lc = jnp.sum(ec, axis=0, keepdims=True)
    pc = ec / jnp.where(lc > 0.0, lc, 1.0)
    o_cmp = _dot(vc_ref[0, 0], pc.astype(BF16))[:DH]

    psum = jnp.concatenate(
        [sum(pc[:, r * RB + h * QT:r * RB + (h + 1) * QT] for h in range(HPG)) for r in range(NR)], axis=1)
    ovm = ov_ref[...]
    hi, mid, lo = _split3(psum)
    imp = (_dot(ovm, hi) + _dot(ovm, mid) + _dot(ovm, lo))[0:NSEL]
    jrow = lax.broadcasted_iota(jnp.int32, (NSEL, TQ), 0)
    blk = (t0 + lax.broadcasted_iota(jnp.int32, (NSEL, TQ), 1)) // SEL_LEN
    forced = (jrow == 0) | (jrow == blk) | (jrow == blk - 1)
    imp = jnp.where(forced, FORCE_SCORE, imp)
    imp = jnp.where(jrow <= blk, imp, NEG)
    cnt = jnp.zeros((NSEL, TQ), F32)
    for jp in range(NSEL):
        rowv = imp[jp:jp + 1, :]
        beats = (rowv > imp) | ((rowv == imp) & (jrow > jp))
        cnt = cnt + jnp.where(beats, 1.0, 0.0)
    selb = jnp.where(cnt < float(SEL_TOPN), 0.0, NEG)
    qaug = jnp.concatenate([qb, to_cols([selb] * HPG).astype(BF16),
                            jnp.zeros((2 * DH - DH - NSEL, R), BF16)], axis=0)

    def step(scores, vt, state):
        m_new = [jnp.maximum(state[r][0], jnp.max(scores[r], axis=0, keepdims=True)) for r in range(NR)]
        out = []
        for r in range(NR):
            p = jnp.exp2(scores[r] - m_new[r]).astype(BF16)
            acc = state[r][1] * jnp.exp2(state[r][0] - m_new[r]) + _dot(vt[r], p)
            out.append((m_new[r], acc))
        return out

    scores, vts = [], []
    for r in range(NR):
        s = _dot(ktile(ks_ref, t0, (r + 1) * QT), qaug[:, cols(r)])
        parts = [krows(s, c) for c in range(r + 1)]
        parts[r] = parts[r] + nb0
        if r >= 1:
            parts[r - 1] = parts[r - 1] + nb1
        scores.append(jnp.concatenate(parts, axis=0))
        vts.append(vslab(vs_ref, qi * NR, r + 1))
    state = step(scores, vts, [(jnp.full((1, RB), NEG, F32), jnp.zeros((NSA_VR, RB), F32))] * NR)

    def past_tile(kt, flat):
        kk = ktile(ks_ref, kt * TQ, TQ)
        vt = vslab(vs_ref, kt * NR, NR)
        scores = [_dot(kk, qaug[:, cols(r)]) for r in range(NR)]
        corner = scores[0][TQ - QT:] + jnp.where(kt == qi - 1, nb1, 0.0)
        scores[0] = jnp.concatenate([scores[0][:TQ - QT], corner], axis=0)
        new = step(scores, [vt] * NR, [(flat[2 * r], flat[2 * r + 1]) for r in range(NR)])
        return tuple(x for pair in new for x in pair)

    flat = lax.fori_loop(0, qi, past_tile, tuple(x for pair in state for x in pair))
    acc_s = jnp.concatenate([flat[2 * r + 1] for r in range(NR)], axis=1)

    gt = _sigmoid(gt_ref[...]).T

    def gate_row(br):
        return to_cols([gt[br * HPG + h:br * HPG + h + 1, :] for h in range(HPG)])

    def inv_l(acc):
        l = acc[DH:DH + 1, :]
        return 1.0 / jnp.where(l > 0.0, l, 1.0)

    o = (gate_row(0) * o_cmp + (gate_row(1) * inv_l(acc_s)) * acc_s[:DH]
         + (gate_row(2) * inv_l(acc_w)) * acc_w[:DH])
    o_hd = jnp.concatenate(
        [jnp.concatenate([o[:, r * RB + h * QT:r * RB + (h + 1) * QT] for r in range(NR)], axis=1)
         for h in range(HPG)], axis=0)
    o_ref[...] = o_hd.T.astype(o_ref.dtype)


def _nsa(z, zg, prep, bias, q_gain, ovm, batch, seq):
    N = z.shape[0]
    G, HPG, DH, TQ = NSA_G, NSA_HPG, NSA_DH, NSA_TQ
    nq = seq // TQ
    R = HPG * TQ
    oks, ovs, okw, ovw, okc, ovc = prep
    nbias, cbias = bias
    QW = HPG * DH
    qg = jnp.broadcast_to(q_gain.reshape(DH, 1), (DH, QT))

    def kvspec(a):
        return pl.BlockSpec((1, 1) + a.shape[2:], lambda p, i, _n=a.ndim - 2: (p // G, p % G) + (0,) * _n)

    return pl.pallas_call(
        _nsa_kernel,
        grid=(batch * G, nq),
        in_specs=[pl.BlockSpec((TQ, QW), lambda p, i: ((p // G) * nq + i, OFF_NQ // QW + p % G)),
                  pl.BlockSpec((TQ, 128), lambda p, i: ((p // G) * nq + i, p % G)),
                  kvspec(oks), kvspec(ovs), kvspec(okw), kvspec(ovw), kvspec(okc), kvspec(ovc),
                  pl.BlockSpec((HPG, QT, TQ), lambda p, i: (p % G, 0, i)),
                  pl.BlockSpec((HPG, 2, QT, QT), lambda p, i: (p % G, 0, 0, 0)),
                  pl.BlockSpec((DH, QT), lambda p, i: (0, 0)),
                  pl.BlockSpec((QT, QT), lambda p, i: (0, 0))],
        out_specs=pl.BlockSpec((TQ, QW), lambda p, i: ((p // G) * nq + i, p % G)),
        out_shape=jax.ShapeDtypeStruct((N, G * QW), BF16),
        compiler_params=_cparams(("parallel", "arbitrary")),
        name="nsa_attention",
    )(z, zg, oks, ovs, okw, ovw, okc, ovc, cbias, nbias, qg, ovm)


def _merge_kernel(h_ref, mod_ref, za_ref, zb_ref, zc_ref, oa_ref, ys_ref, oc_ref,
                  wa_ref, wb_ref, wc_ref, wo_ref, o_ref):
    D = h_ref.shape[1]
    ya = _dot(oa_ref[...], wa_ref[...])
    zz = _dot(ys_ref[...], wb_ref[...])
    yb = zz[:, :D] * _sigmoid(zz[:, D:])
    yc = _dot(oc_ref[...], wc_ref[...])
    merged = _sigmoid(za_ref[...]) * ya + _sigmoid(zb_ref[...]) * yb + _sigmoid(zc_ref[...]) * yc
    o_ref[...] = h_ref[...] + mod_ref[0, 5:6, :] * _dot(merged.astype(BF16), wo_ref[...])


def _merge(h, mod, z, oa, ys, oc, wa, wb, wc, wo, seq):
    N, D = h.shape
    tm = 256
    tpb = seq // tm
    row = lambda w: pl.BlockSpec((tm, w), lambda i: (i, 0))
    res = lambda a: pl.BlockSpec(a.shape, lambda i: (0, 0), pipeline_mode=pl.Buffered(1))
    return pl.pallas_call(
        _merge_kernel,
        grid=(N // tm,),
        in_specs=[row(D),
                  pl.BlockSpec((1, 9, D), lambda i: (i // tpb, 0, 0)),
                  pl.BlockSpec((tm, D), lambda i: (i, OFF_ZA // D)),
                  pl.BlockSpec((tm, D), lambda i: (i, OFF_ZB // D)),
                  pl.BlockSpec((tm, D), lambda i: (i, OFF_ZC // D)),
                  row(oa.shape[1]), row(ys.shape[1]), row(oc.shape[1]),
                  res(wa), res(wb), res(wc), res(wo)],
        out_specs=row(D),
        out_shape=jax.ShapeDtypeStruct((N, D), F32),
        compiler_params=_cparams(("parallel",)),
        name="mix_merge",
    )(h, mod, z, z, z, oa, ys, oc, wa, wb, wc, wo)


def _split_w_in(w):
    gate0 = 4 * 512 + 512 + 1024 + 6 * 256
    ngate = 3 * NSA_G * NSA_HPG
    L, D = w.shape[:2]
    gw = w[:, :, gate0:gate0 + ngate].reshape(L, D, 3, NSA_G, NSA_HPG)
    gw = gw.transpose(0, 1, 3, 2, 4).reshape(L, D, NSA_G, 3 * NSA_HPG)
    gw = jnp.pad(gw, ((0, 0), (0, 0), (0, 0), (0, 128 - 3 * NSA_HPG))).reshape(L, D, NSA_G * 128)
    return w[:, :, :gate0].astype(BF16), w[:, :, gate0 + ngate:].astype(BF16), gw.astype(BF16)


def _overlap_matrix():
    j = jnp.arange(QT)[:, None]
    n = jnp.arange(QT)[None, :]
    st = n * CMP_STRIDE
    ov = (st < j * SEL_LEN + SEL_LEN) & (st + CMP_LEN > j * SEL_LEN)
    return jnp.where(ov, 1.0, 0.0).astype(BF16)


def kernel(x, c, ada_w, ada_b, norm_g, ffn1_wi, ffn1_wo, ffn2_wi, ffn2_wo, w_in, hg_lb_logits,
           hg_onorm, hg_proj, ssm_a_re, ssm_a_im, ssm_log_dt, ssm_b_re, ssm_b_im, ssm_c_re,
           ssm_c_im, ssm_d, ssm_glu_w, nsa_q_gain, nsa_k_gain, nsa_pe_k, nsa_pe_v, nsa_phi_k,
           nsa_phi_v, nsa_proj, rel_table, w_out):
    B, S, D = x.shape
    L = ada_w.shape[0]
    N = B * S
    assert S % 512 == 0 and S // SEL_LEN == 32 and S // QT == 16
    lb_cum = jnp.cumsum(jax.nn.softmax(hg_lb_logits.astype(F32), axis=0), axis=0)
    lower_bounds = lb_cum - lb_cum[0:1]
    mods = _mods(c, ada_w, ada_b).reshape(L, B, 9, D)
    bias = _bias_tables(rel_table, S)
    ovm = _overlap_matrix()
    w_mix, w_z, w_gate = _split_w_in(w_in)
    h = x.reshape(N, D)
    for l in range(L):
        mod = mods[l]
        h = _ffn(h, mod, norm_g[l, 0:1], ffn1_wi, ffn1_wo, l, 0, S)
        z, zg = _win(h, mod, norm_g[l, 1:2], w_mix, w_z, w_gate, l, S)
        oa = _hgrn(z, lower_bounds[l:l + 1], hg_onorm[l:l + 1], B, S)
        sp = _ssm_params(ssm_a_re[l], ssm_a_im[l], ssm_log_dt[l], ssm_b_re[l], ssm_b_im[l],
                         ssm_c_re[l], ssm_c_im[l])
        ys = _ssm(z, sp, ssm_d[l:l + 1], B, S)
        prep = _nsa_prep(z, nsa_k_gain[l:l + 1], nsa_pe_k[l], nsa_pe_v[l],
                         nsa_phi_k[l].astype(BF16), nsa_phi_v[l].astype(BF16), B, S)
        oc = _nsa(z, zg, prep, bias, nsa_q_gain[l:l + 1], ovm, B, S)
        h = _merge(h, mod, z, oa, ys, oc, hg_proj[l].astype(BF16), ssm_glu_w[l].astype(BF16),
                   nsa_proj[l].astype(BF16), w_out[l].astype(BF16), S)
        h = _ffn(h, mod, norm_g[l, 2:3], ffn2_wi, ffn2_wo, l, 6, S)
    return h.reshape(B, S, D)
```

```python
import functools
import math

import jax
import jax.numpy as jnp
from jax import lax
from jax.experimental import pallas as pl
from jax.experimental.pallas import tpu as pltpu

F32 = jnp.float32
BF16 = jnp.bfloat16

EPS = 1e-6
NEG = -1e30
LOG2E = 1.4426950408889634
FORCE_SCORE = 1e4

HG_HEADS, HG_D, HG_CHUNK = 4, 128, 16
SSM_GROUPS, SSM_P, SSM_N, SSM_CHUNK = 32, 16, 64, 8
SSM_BUNDLE = 8
SSM_TAP_BUNDLE = 16
NSA_G, NSA_HPG, NSA_DH = 4, 4, 64
CMP_LEN, CMP_STRIDE, SEL_LEN, SEL_TOPN, WIN = 32, 16, 64, 16, 512
REL_BUCKETS, REL_MAX_DIST = 32, 128
QT = 128
NSA_TQ = 512
NSA_VR = NSA_DH + 16

VMEM_LIMIT = 60 * 1024 * 1024

OFF_ZA, OFF_ZB, OFF_ZC = 0, 2048, 4096
OFF_HQ, OFF_HF, OFF_HI, OFF_HG = 6144, 6656, 7168, 7680
OFF_SU, OFF_NQ = 8192, 8704
OFF_KC, OFF_VC, OFF_KS, OFF_VS, OFF_KW, OFF_VW = 9728, 9984, 10240, 10496, 10752, 11008
Z_WIDTH = 11264


def _cparams(sem):
    return pltpu.CompilerParams(dimension_semantics=sem, vmem_limit_bytes=VMEM_LIMIT)


def _dot(a, b):
    return jnp.dot(a, b, preferred_element_type=F32)


def _dot_nt(a, b):
    return lax.dot_general(a, b, (((1,), (1,)), ((), ())), preferred_element_type=F32)


def _dot_tn(a, b):
    return lax.dot_general(a, b, (((0,), (0,)), ((), ())), preferred_element_type=F32)


def _sigmoid(x):
    return 1.0 / (1.0 + jnp.exp(-x))


def _split3(x):
    hi = x.astype(BF16)
    r = x - hi.astype(F32)
    mid = r.astype(BF16)
    lo = (r - mid.astype(F32)).astype(BF16)
    return hi, mid, lo


def _norm_mod(x, gain, shift, scale):
    ms = jnp.mean(x * x, axis=-1, keepdims=True)
    y = x * lax.rsqrt(ms + EPS) * gain
    return y * (1.0 + scale) + shift


def _mod_kernel(c_ref, w_ref, b_ref, o_ref):
    c = c_ref[...]
    ca = (c * _sigmoid(c)).astype(BF16)
    o_ref[0] = _dot(ca, w_ref[0].astype(BF16)) + b_ref[0]


def _mods(c, ada_w, ada_b):
    L, D, W = ada_w.shape
    B = c.shape[0]
    tn = 1024
    return pl.pallas_call(
        _mod_kernel,
        grid=(L, W // tn),
        in_specs=[pl.BlockSpec((B, D), lambda l, j: (0, 0)),
                  pl.BlockSpec((1, D, tn), lambda l, j: (l, 0, j)),
                  pl.BlockSpec((1, 1, tn), lambda l, j: (l, 0, j))],
        out_specs=pl.BlockSpec((1, B, tn), lambda l, j: (l, 0, j)),
        out_shape=jax.ShapeDtypeStruct((L, B, W), F32),
        compiler_params=_cparams(("parallel", "parallel")),
        name="adaln_mod",
    )(c, ada_w, ada_b.reshape(L, 1, W))


def _ffn_kernel(h_ref, mod_ref, g_ref, wi1_ref, wi2_ref, wo_ref, o_ref, u_s, *, k0, nf):
    f = pl.program_id(1)

    @pl.when(f == 0)
    def _():
        u = _norm_mod(h_ref[...], g_ref[...], mod_ref[0, k0:k0 + 1, :], mod_ref[0, k0 + 1:k0 + 2, :])
        u_s[...] = u.astype(BF16)
        o_ref[...] = jnp.zeros_like(o_ref)

    u = u_s[...]
    a1 = _dot(u, wi1_ref[0].astype(BF16))
    a2 = _dot(u, wi2_ref[0].astype(BF16))
    act = (a1 * _sigmoid(a1) * a2).astype(BF16)
    o_ref[...] += _dot(act, wo_ref[0].astype(BF16))

    @pl.when(f == nf - 1)
    def _():
        o_ref[...] = h_ref[...] + (0.5 * mod_ref[0, k0 + 2:k0 + 3, :]) * o_ref[...]


def _ffn(h, mod, gain, wi, wo, layer, k0, seq):
    N, D = h.shape
    dff = wo.shape[1]
    tm, tf = 1024, 256
    nf = dff // tf
    tpb = seq // tm
    return pl.pallas_call(
        functools.partial(_ffn_kernel, k0=k0, nf=nf),
        grid=(N // tm, nf),
        in_specs=[pl.BlockSpec((tm, D), lambda i, f: (i, 0)),
                  pl.BlockSpec((1, 9, D), lambda i, f: (i // tpb, 0, 0)),
                  pl.BlockSpec((1, D), lambda i, f: (0, 0)),
                  pl.BlockSpec((1, D, tf), lambda i, f: (layer, 0, f)),
                  pl.BlockSpec((1, D, tf), lambda i, f: (layer, 0, f + nf)),
                  pl.BlockSpec((1, tf, D), lambda i, f: (layer, f, 0))],
        out_specs=pl.BlockSpec((tm, D), lambda i, f: (i, 0)),
        out_shape=jax.ShapeDtypeStruct((N, D), F32),
        scratch_shapes=[pltpu.VMEM((tm, D), BF16)],
        compiler_params=_cparams(("parallel", "arbitrary")),
        name="ffn",
    )(h, mod, gain, wi, wi, wo)


def _win_kernel(h_ref, mod_ref, g_ref, wm_ref, wz_ref, wg_ref, z_ref, zg_ref, u_s, *, nz):
    j = pl.program_id(1)

    @pl.when(j == 0)
    def _():
        u = _norm_mod(h_ref[...], g_ref[...], mod_ref[0, 3:4, :], mod_ref[0, 4:5, :])
        ub = u.astype(BF16)
        u_s[...] = ub
        zg_ref[...] = _dot(ub, wg_ref[0])

    @pl.when(j < nz)
    def _():
        z_ref[...] = _dot(u_s[...], wz_ref[0])

    @pl.when(j >= nz)
    def _():
        z_ref[...] = _dot(u_s[...], wm_ref[0])


def _win(h, mod, gain, w_mix, w_z, w_gate, layer, seq):
    N, D = h.shape
    tm, tn = 1024, 1024
    tpb = seq // tm
    GW = w_gate.shape[2]
    nz = w_z.shape[2] // tn
    nmix = w_mix.shape[2] // tn
    return pl.pallas_call(
        functools.partial(_win_kernel, nz=nz),
        grid=(N // tm, nz + nmix),
        in_specs=[pl.BlockSpec((tm, D), lambda i, j: (i, 0)),
                  pl.BlockSpec((1, 9, D), lambda i, j: (i // tpb, 0, 0)),
                  pl.BlockSpec((1, D), lambda i, j: (0, 0)),
                  pl.BlockSpec((1, D, tn), lambda i, j: (layer, 0, jnp.maximum(j - nz, 0))),
                  pl.BlockSpec((1, D, tn), lambda i, j: (layer, 0, jnp.minimum(j, nz - 1))),
                  pl.BlockSpec((1, D, GW), lambda i, j: (layer, 0, 0))],
        out_specs=[pl.BlockSpec((tm, tn), lambda i, j: (i, j)),
                   pl.BlockSpec((tm, GW), lambda i, j: (i, 0))],
        out_shape=[jax.ShapeDtypeStruct((N, Z_WIDTH), F32),
                   jax.ShapeDtypeStruct((N, GW), F32)],
        scratch_shapes=[pltpu.VMEM((tm, D), BF16)],
        compiler_params=_cparams(("parallel", "arbitrary")),
        name="in_proj",
    )(h, mod, gain, w_mix, w_z, w_gate)


def _hgrn_kernel(q_ref, f_ref, i_ref, g_ref, lb_ref, on_ref, o_ref, st_ref, *, tb):
    C, H, DK = HG_CHUNK, HG_HEADS, HG_D
    nc = tb // C
    W = H * DK

    @pl.when(pl.program_id(1) == 0)
    def _():
        st_ref[...] = jnp.zeros_like(st_ref)

    q = q_ref[...]
    qs = q * _sigmoid(q)
    x = f_ref[...]
    iv = i_ref[...]
    lb = lb_ref[...]
    sp = jnp.log1p(jnp.exp(-jnp.abs(x)))
    lsig = jnp.minimum(x, 0.0) - sp
    a = jnp.log(jnp.maximum(lb, 1e-38))
    bterm = jnp.log1p(-lb) + lsig
    lae = jnp.maximum(a, bterm) + jnp.log1p(jnp.exp(-jnp.abs(a - bterm)))
    log_f = jnp.where(lb > 0.0, lae, bterm)
    k = (1.0 - lb) * jnp.exp(jnp.minimum(-x, 0.0) - sp)

    r = lax.broadcasted_iota(jnp.int32, (tb, tb), 0)
    cidx = lax.broadcasted_iota(jnp.int32, (tb, tb), 1)
    lmat = jnp.where(((r // C) == (cidx // C)) & (cidx <= r), 1.0, 0.0).astype(BF16)
    hi, mid, lo = _split3(log_f)
    b = _dot(lmat, hi) + _dot(lmat, mid) + _dot(lmat, lo)

    qe = (qs * jnp.exp(b)).astype(BF16)
    b3 = b.reshape(nc, C, W)
    bend = b3[:, C - 1:C, :]
    kdec = (k.reshape(nc, C, W) * jnp.exp(bend - b3)).reshape(tb, W).astype(BF16)
    ebend = jnp.exp(bend)
    ib = iv.astype(BF16)
    o_heads = []
    for h in range(H):
        ls = slice(h * DK, (h + 1) * DK)
        uts = [_dot_tn(ib[c * C:(c + 1) * C, ls], kdec[c * C:(c + 1) * C, ls]) for c in range(nc)]
        st = st_ref[h]
        sts = []
        for c in range(nc):
            sts.append(st.astype(BF16))
            st = st * ebend[c][:, ls] + uts[c]
        st_ref[h] = st
        o_heads.append(jnp.concatenate(
            [_dot_nt(qe[c * C:(c + 1) * C, ls], sts[c]) for c in range(nc)], axis=0))
    o = jnp.concatenate(o_heads, axis=1)

    b2 = b * LOG2E
    tmod = lax.broadcasted_iota(jnp.int32, (tb, W), 0) % C
    rr = lax.broadcasted_iota(jnp.int32, (W, W), 0) // DK
    cc = lax.broadcasted_iota(jnp.int32, (W, W), 1) // DK
    ones_bd = jnp.where(rr == cc, 1.0, 0.0).astype(BF16)
    SUB = 8

    def shifted_terms(qx, bx, kx, bsrc, isrc, shift, rowmod):
        if shift == 0:
            p = qx * kx * jnp.exp2(bx - bsrc)
            isd = isrc
        else:
            kd = pltpu.roll(kx, shift, 0)
            bd = pltpu.roll(bsrc, shift, 0)
            isd = pltpu.roll(isrc, shift, 0)
            p = jnp.where(rowmod >= shift, qx * kd * jnp.exp2(bx - bd), 0.0)
        return _dot(p.astype(BF16), ones_bd) * isd

    for d in range(SUB):
        if d == 0:
            o = o + _dot((qs * k).astype(BF16), ones_bd) * iv
        else:
            o = o + shifted_terms(qs, b2, k, b2, iv, d, tmod)

    def group(x, i):
        return x.reshape(nc, C // SUB, SUB, W)[:, i].reshape(nc * SUB, W)

    rmod = lax.broadcasted_iota(jnp.int32, (nc * SUB, W), 0) % SUB
    q_hi, b_hi = group(qs, 1), group(b2, 1)
    k_lo, b_lo, i_lo = group(k, 0), group(b2, 0), group(iv, 0)
    o_hi = group(o, 1)
    for d in range(SUB, C):
        o_hi = o_hi + shifted_terms(q_hi, b_hi, k_lo, b_lo, i_lo, d - SUB, rmod)
    o = jnp.stack([group(o, 0).reshape(nc, SUB, W), o_hi.reshape(nc, SUB, W)], axis=1).reshape(tb, W)

    g = g_ref[...]
    gs = g * _sigmoid(g)
    onw = on_ref[...]
    outs = []
    for h in range(H):
        ls = slice(h * DK, (h + 1) * DK)
        oh = o[:, ls]
        outs.append(oh * lax.rsqrt(jnp.mean(oh * oh, axis=-1, keepdims=True) + EPS) * onw * gs[:, ls])
    o_ref[...] = jnp.concatenate(outs, axis=1).astype(o_ref.dtype)


def _hgrn(z, lb, onorm, batch, seq):
    N = z.shape[0]
    tb = 256
    nb = seq // tb
    W = HG_HEADS * HG_D

    def zspec(off):
        return pl.BlockSpec((tb, W), lambda b, j: (b * nb + j, off // W))

    return pl.pallas_call(
        functools.partial(_hgrn_kernel, tb=tb),
        grid=(batch, nb),
        in_specs=[zspec(OFF_HQ), zspec(OFF_HF), zspec(OFF_HI), zspec(OFF_HG),
                  pl.BlockSpec((1, W), lambda b, j: (0, 0)),
                  pl.BlockSpec((1, HG_D), lambda b, j: (0, 0))],
        out_specs=pl.BlockSpec((tb, W), lambda b, j: (b * nb + j, 0)),
        out_shape=jax.ShapeDtypeStruct((N, W), BF16),
        scratch_shapes=[pltpu.VMEM((HG_HEADS, HG_D, HG_D), F32)],
        compiler_params=_cparams(("parallel", "arbitrary")),
        name="hgrn2",
    )(z, z, z, z, lb, onorm)


def _ssm_kernel(u_ref, bm_ref, c_ref, kt_ref, p1r_ref, p1i_ref, p2r_ref, p2i_ref,
                alr_ref, ali_ref, d_ref, o_ref, xr_ref, xi_ref, xpr_ref, xpi_ref, *, tt):
    LC = SSM_CHUNK
    nc = tt // LC
    W = SSM_GROUPS * SSM_N

    @pl.when(pl.program_id(1) == 0)
    def _():
        xr_ref[...] = jnp.zeros_like(xr_ref)
        xi_ref[...] = jnp.zeros_like(xi_ref)

    u = u_ref[...]
    ub = u.astype(BF16)
    nb, cw, sw = bm_ref.shape[0], bm_ref.shape[1], bm_ref.shape[2] // 2
    bu = [_dot(ub[:, s * cw:(s + 1) * cw], bm_ref[s]) for s in range(nb)]
    bur = jnp.concatenate([x[:, :sw] for x in bu], axis=1).reshape(nc, LC, W)
    bui = jnp.concatenate([x[:, sw:] for x in bu], axis=1).reshape(nc, LC, W)
    p1r, p1i = p1r_ref[...], p1i_ref[...]
    vr = jnp.sum(bur * p1r - bui * p1i, axis=1)
    vi = jnp.sum(bur * p1i + bui * p1r, axis=1)

    alr, ali = alr_ref[...], ali_ref[...]
    xr, xi = xr_ref[...], xi_ref[...]
    for c in range(nc):
        xpr_ref[c:c + 1, :] = xr
        xpi_ref[c:c + 1, :] = xi
        nr = alr * xr - ali * xi + vr[c:c + 1, :]
        ni = alr * xi + ali * xr + vi[c:c + 1, :]
        xr, xi = nr, ni
    xr_ref[...] = xr
    xi_ref[...] = xi

    xpr = xpr_ref[...][:, None, :]
    xpi = xpi_ref[...][:, None, :]
    p2r, p2i = p2r_ref[...], p2i_ref[...]
    zr = (p2r * xpr - p2i * xpi).reshape(tt, W).astype(BF16)
    zi = (p2r * xpi + p2i * xpr).reshape(tt, W).astype(BF16)
    y = jnp.concatenate(
        [_dot(jnp.concatenate([zr[:, s * sw:(s + 1) * sw], zi[:, s * sw:(s + 1) * sw]], axis=1), c_ref[s])
         for s in range(nb)], axis=1)

    tmod = lax.broadcasted_iota(jnp.int32, u.shape, 0) % LC
    nkb, kw = kt_ref.shape[1], kt_ref.shape[2]
    taps = [jnp.zeros((tt, kw), F32)] * nkb
    for tau in range(LC):
        if tau == 0:
            us = ub
        else:
            us = jnp.where(tmod >= tau, pltpu.roll(u, tau, 0), 0.0).astype(BF16)
        taps = [taps[s] + _dot(us[:, s * kw:(s + 1) * kw], kt_ref[tau, s]) for s in range(nkb)]
    y = y + jnp.concatenate(taps, axis=1) + d_ref[...] * u
    g = 0.5 * y * (1.0 + jnp.tanh(0.7978845608028654 * (y + 0.044715 * (y * y * y))))
    o_ref[...] = g.astype(o_ref.dtype)


def _ssm_params(a_re, a_im, log_dt, b_re, b_im, c_re, c_im):
    G, P, N, LC = SSM_GROUPS, SSM_P, SSM_N, SSM_CHUNK
    hp = lax.Precision.HIGHEST
    a_re = jnp.minimum(a_re.astype(F32), -1e-4)
    a_im = a_im.astype(F32)
    dt = jnp.exp(log_dt.astype(F32))[:, None]
    mag = jnp.exp(dt * a_re)
    ab_re, ab_im = mag * jnp.cos(dt * a_im), mag * jnp.sin(dt * a_im)
    den = a_re * a_re + a_im * a_im
    nr = ab_re - 1.0
    z_re = (nr * a_re + ab_im * a_im) / den
    z_im = (ab_im * a_re - nr * a_im) / den
    b_re, b_im = b_re.astype(F32), b_im.astype(F32)
    bb_re = z_re[..., None] * b_re - z_im[..., None] * b_im
    bb_im = z_re[..., None] * b_im + z_im[..., None] * b_re
    kk = jnp.arange(LC + 1, dtype=F32)[:, None, None]
    pm = jnp.exp(kk * dt * a_re)
    pw_re, pw_im = pm * jnp.cos(kk * dt * a_im), pm * jnp.sin(kk * dt * a_im)
    def bdiag(src):
        n, a, b = src.shape
        x = jnp.broadcast_to(src.reshape(n * a, 1, b), (n * a, n, b)).reshape(n * a, n * b)
        rg = jnp.arange(n * a)[:, None] // a
        cg = jnp.arange(n * b)[None, :] // b
        return jnp.where(rg == cg, x, 0.0)

    def bundles(src, per):
        return jax.vmap(bdiag)(src.reshape(G // per, per, *src.shape[1:]))

    bmat = jnp.concatenate([bundles(bb_re.transpose(0, 2, 1), SSM_BUNDLE),
                            bundles(bb_im.transpose(0, 2, 1), SSM_BUNDLE)], axis=2)
    c_re, c_im = c_re.astype(F32), c_im.astype(F32)
    cmat = jnp.concatenate([bundles(c_re.transpose(0, 2, 1), SSM_BUNDLE),
                            -bundles(c_im.transpose(0, 2, 1), SSM_BUNDLE)], axis=1)
    t_re = pw_re[:LC, :, :, None] * bb_re[None] - pw_im[:LC, :, :, None] * bb_im[None]
    t_im = pw_re[:LC, :, :, None] * bb_im[None] + pw_im[:LC, :, :, None] * bb_re[None]
    kt = (jnp.einsum('gqn,tgnp->tgpq', c_re, t_re, precision=hp)
          - jnp.einsum('gqn,tgnp->tgpq', c_im, t_im, precision=hp))
    ktm = jax.vmap(lambda k: bundles(k, SSM_TAP_BUNDLE))(kt)
    flat = lambda x: x.reshape(x.shape[0], G * N)
    p1r, p1i = flat(pw_re[LC - 1::-1][:LC]), flat(pw_im[LC - 1::-1][:LC])
    p2r, p2i = flat(pw_re[1:LC + 1]), flat(pw_im[1:LC + 1])
    alr, ali = flat(pw_re[LC:LC + 1]), flat(pw_im[LC:LC + 1])
    return (bmat.astype(BF16), cmat.astype(BF16), ktm.astype(BF16), p1r, p1i, p2r, p2i, alr, ali)


def _ssm(z, params, d_skip, batch, seq):
    N = z.shape[0]
    tt = 256
    nb = seq // tt
    CW = SSM_GROUPS * SSM_P
    W = SSM_GROUPS * SSM_N
    LC = SSM_CHUNK
    bmat, cmat, ktm, p1r, p1i, p2r, p2i, alr, ali = params
    full = lambda a: pl.BlockSpec(a.shape, lambda b, j, _n=a.ndim: (0,) * _n)
    return pl.pallas_call(
        functools.partial(_ssm_kernel, tt=tt),
        grid=(batch, nb),
        in_specs=[pl.BlockSpec((tt, CW), lambda b, j: (b * nb + j, OFF_SU // CW)),
                  full(bmat), full(cmat), full(ktm), full(p1r), full(p1i),
                  full(p2r), full(p2i), full(alr), full(ali),
                  pl.BlockSpec((1, CW), lambda b, j: (0, 0))],
        out_specs=pl.BlockSpec((tt, CW), lambda b, j: (b * nb + j, 0)),
        out_shape=jax.ShapeDtypeStruct((N, CW), BF16),
        scratch_shapes=[pltpu.VMEM((1, W), F32), pltpu.VMEM((1, W), F32),
                        pltpu.VMEM((tt // LC, W), F32), pltpu.VMEM((tt // LC, W), F32)],
        compiler_params=_cparams(("parallel", "arbitrary")),
        name="s5_ssm",
    )(z, bmat, cmat, ktm, p1r, p1i, p2r, p2i, alr, ali, d_skip)


def _t5_bucket(dist):
    n = jnp.maximum(dist, 0)
    max_exact = REL_BUCKETS // 2
    nf = jnp.maximum(n, 1).astype(F32)
    large = max_exact + (jnp.log(nf / max_exact) / math.log(REL_MAX_DIST / max_exact)
                         * (REL_BUCKETS - max_exact)).astype(jnp.int32)
    large = jnp.minimum(large, REL_BUCKETS - 1)
    return jnp.where(n < max_exact, n, large)


def _bias_kernel(tab_ref, bk_ref, o_ref):
    h = pl.program_id(0)
    bk = bk_ref[...]
    base = tab_ref[REL_BUCKETS - 1, h]
    acc = jnp.full(bk.shape, NEG, F32)
    for k in range(REL_BUCKETS):
        acc = jnp.where(bk == k, (tab_ref[k, h] - base) * LOG2E, acc)
    o_ref[0] = acc


def _bias_tables(rel_table, seq):
    j = jnp.arange(QT)[:, None]
    i = jnp.arange(QT)[None, :]
    d0 = i - j
    b0 = jnp.where(d0 >= 0, _t5_bucket(d0), REL_BUCKETS)
    b1 = _t5_bucket(QT + i - j)
    n = jnp.arange(QT)[:, None]
    t = jnp.arange(seq)[None, :]
    n_cmp = (seq - CMP_LEN) // CMP_STRIDE + 1
    dc = t - (n * CMP_STRIDE + CMP_LEN - 1)
    bc = jnp.where((dc >= 0) & (n < n_cmp), _t5_bucket(dc), REL_BUCKETS)
    nh = rel_table.shape[1]
    tab = rel_table.astype(F32)

    def expand(bk):
        rows, lanes = bk.shape
        return pl.pallas_call(
            _bias_kernel,
            grid=(nh,),
            in_specs=[pl.BlockSpec(memory_space=pltpu.SMEM),
                      pl.BlockSpec((rows, lanes), lambda h: (0, 0))],
            out_specs=pl.BlockSpec((1, rows, lanes), lambda h: (h, 0, 0)),
            out_shape=jax.ShapeDtypeStruct((nh, rows, lanes), F32),
            compiler_params=_cparams(("arbitrary",)),
            name="nsa_bias_tables",
        )(tab, bk.astype(jnp.int32))

    near = expand(jnp.concatenate([b0, b1], axis=0)).reshape(nh, 2, QT, QT)
    return near, expand(bc)


def _head_rms(x, gain):
    return x * lax.rsqrt(jnp.mean(x * x, axis=-1, keepdims=True) + EPS) * gain


def _nsa_prep_kernel(kc_ref, vc_ref, ks_ref, vs_ref, kw_ref, vw_ref, kg_ref, pek_ref, pev_ref,
                     phik_ref, phiv_ref, oks_ref, ovs_ref, okw_ref, ovw_ref, okc_ref, ovc_ref,
                     xpad_ref, *, seq):
    DH = NSA_DH
    kg = kg_ref[...]
    srow = lax.broadcasted_iota(jnp.int32, (seq, DH), 0)
    lane = lax.broadcasted_iota(jnp.int32, (seq, DH), 1)
    onehot = jnp.where(lane == srow // SEL_LEN, 1.0, 0.0).astype(BF16)
    nkt = seq // QT
    npad = WIN // QT
    xpad_ref[seq:seq + CMP_LEN, :] = jnp.zeros((CMP_LEN, DH), F32)

    def value_tiles(v):
        return jnp.concatenate([v, jnp.ones_like(v)], axis=1).T[:NSA_VR].astype(BF16)

    for g in range(NSA_G):
        ls = slice(g * DH, (g + 1) * DH)
        ksn = _head_rms(ks_ref[:, ls], kg).astype(BF16)
        oks_ref[0, g] = jnp.concatenate([ksn, onehot], axis=1)
        okw_ref[0, g, 0:WIN, :] = jnp.zeros((WIN, DH), BF16)
        okw_ref[0, g, WIN:WIN + seq, :] = _head_rms(kw_ref[:, ls], kg).astype(BF16)
        vst = value_tiles(vs_ref[:, ls])
        vwt = value_tiles(vw_ref[:, ls])
        ovw_ref[0, g, 0:npad] = jnp.zeros((npad, NSA_VR, QT), BF16)
        for kt in range(nkt):
            ovs_ref[0, g, kt] = vst[:, kt * QT:(kt + 1) * QT]
            ovw_ref[0, g, npad + kt] = vwt[:, kt * QT:(kt + 1) * QT]
        for src_ref, pe_ref, phi_ref, is_k in ((kc_ref, pek_ref, phik_ref, True),
                                               (vc_ref, pev_ref, phiv_ref, False)):
            xpad_ref[0:seq, :] = src_ref[:, ls]
            acc = jnp.zeros((QT, DH), F32)
            for l in range(CMP_LEN):
                xl = xpad_ref[pl.ds(l, QT, stride=CMP_STRIDE), :] + pe_ref[l:l + 1, :]
                acc = acc + _dot(xl.astype(BF16), phi_ref[l])
            if is_k:
                okc_ref[0, g] = _head_rms(acc, kg).astype(BF16)
            else:
                ovc_ref[0, g] = value_tiles(acc)


def _nsa_prep(z, k_gain, pe_k, pe_v, phi_k, phi_v, batch, seq):
    G, DH = NSA_G, NSA_DH
    KW = G * DH

    def zspec(off):
        return pl.BlockSpec((seq, KW), lambda b: (b, off // KW))

    full = lambda a: pl.BlockSpec(a.shape, lambda b, _n=a.ndim: (0,) * _n)

    def ospec(*dims):
        return pl.BlockSpec((1, G) + dims, lambda b, _n=len(dims): (b, 0) + (0,) * _n)

    def oshape(*dims):
        return jax.ShapeDtypeStruct((batch, G) + dims, BF16)

    outs = [(seq, 2 * DH), (seq // QT, NSA_VR, QT), (seq + WIN, DH), ((seq + WIN) // QT, NSA_VR, QT),
            (QT, DH), (NSA_VR, QT)]
    return pl.pallas_call(
        functools.partial(_nsa_prep_kernel, seq=seq),
        grid=(batch,),
        in_specs=[zspec(OFF_KC), zspec(OFF_VC), zspec(OFF_KS), zspec(OFF_VS), zspec(OFF_KW),
                  zspec(OFF_VW), full(k_gain), full(pe_k), full(pe_v), full(phi_k), full(phi_v)],
        out_specs=[ospec(*d) for d in outs],
        out_shape=[oshape(*d) for d in outs],
        scratch_shapes=[pltpu.VMEM((seq + CMP_LEN, DH), F32)],
        compiler_params=_cparams(("parallel",)),
        name="nsa_kv_prep",
    )(z, z, z, z, z, z, k_gain, pe_k, pe_v, phi_k, phi_v)


def _nsa_kernel(q_ref, gt_ref, ks_ref, vs_ref, kw_ref, vw_ref, kc_ref, vc_ref, cb_ref, nb_ref,
                qg_ref, ov_ref, o_ref):
    DH, HPG, TQ = NSA_DH, NSA_HPG, NSA_TQ
    NR = TQ // QT
    RB = HPG * QT
    R = NR * RB
    NSEL = 32
    nwt = WIN // QT
    qi = pl.program_id(1)
    t0 = qi * TQ

    def cols(r):
        return slice(r * RB, (r + 1) * RB)

    def to_cols(per_head):
        return jnp.concatenate([per_head[h][:, r * QT:(r + 1) * QT] for r in range(NR) for h in range(HPG)],
                               axis=1)

    def krows(x, c):
        return x[c * QT:(c + 1) * QT]

    def vslab(ref, tile0, n):
        return jnp.concatenate([ref[0, 0, tile0 + c] for c in range(n)], axis=1)

    def ktile(ref, start, size):
        return ref[0, 0, pl.ds(pl.multiple_of(start, QT), size), :]

    qt = q_ref[...].T
    gain = jnp.concatenate([qg_ref[...]] * NR, axis=1) * (DH ** -0.5 * LOG2E)
    qh = []
    for h in range(HPG):
        x = qt[h * DH:(h + 1) * DH]
        qh.append(x * lax.rsqrt(jnp.mean(x * x, axis=0, keepdims=True) + EPS) * gain)
    qb = to_cols(qh).astype(BF16)
    nb0 = jnp.concatenate([nb_ref[h, 0] for h in range(HPG)], axis=1)
    nb1 = jnp.concatenate([nb_ref[h, 1] for h in range(HPG)], axis=1)

    jj = lax.broadcasted_iota(jnp.int32, (QT, RB), 0)
    ii = lax.broadcasted_iota(jnp.int32, (QT, RB), 1) % QT
    acc_w = []
    for r in range(NR):
        s = _dot(ktile(kw_ref, t0 + r * QT, WIN + QT), qb[:, cols(r)])
        parts = [krows(s, c) for c in range(nwt + 1)]
        parts[nwt] = parts[nwt] + nb0
        parts[nwt - 1] = parts[nwt - 1] + nb1
        parts[0] = jnp.where(jj > ii, parts[0], NEG)
        for c in range(nwt - r):
            parts[c] = jnp.where(qi > 0, parts[c], NEG)
        s = jnp.concatenate(parts, axis=0)
        p = jnp.exp2(s - jnp.max(s, axis=0, keepdims=True)).astype(BF16)
        acc_w.append(_dot(vslab(vw_ref, qi * NR + r, nwt + 1), p))
    acc_w = jnp.concatenate(acc_w, axis=1)

    sc = _dot(kc_ref[0, 0], qb) + to_cols([cb_ref[h] for h in range(HPG)])
    mc = jnp.max(sc, axis=0, keepdims=True)
    ec = jnp.where(sc > 0.5 * NEG, jnp.exp2(sc - mc), 0.0)
    lc = jnp.sum(ec, axis=0, keepdims=True)
    pc = ec / jnp.where(lc > 0.0, lc, 1.0)
    o_cmp = _dot(vc_ref[0, 0], pc.astype(BF16))[:DH]

    psum = jnp.concatenate(
        [sum(pc[:, r * RB + h * QT:r * RB + (h + 1) * QT] for h in range(HPG)) for r in range(NR)], axis=1)
    ovm = ov_ref[...]
    hi, mid, lo = _split3(psum)
    imp = (_dot(ovm, hi) + _dot(ovm, mid) + _dot(ovm, lo))[0:NSEL]
    jrow = lax.broadcasted_iota(jnp.int32, (NSEL, TQ), 0)
    blk = (t0 + lax.broadcasted_iota(jnp.int32, (NSEL, TQ), 1)) // SEL_LEN
    forced = (jrow == 0) | (jrow == blk) | (jrow == blk - 1)
    imp = jnp.where(forced, FORCE_SCORE, imp)
    imp = jnp.where(jrow <= blk, imp, NEG)
    cnt = jnp.zeros((NSEL, TQ), F32)
    for jp in range(NSEL):
        rowv = imp[jp:jp + 1, :]
        beats = (rowv > imp) | ((rowv == imp) & (jrow > jp))
        cnt = cnt + jnp.where(beats, 1.0, 0.0)
    selb = jnp.where(cnt < float(SEL_TOPN), 0.0, NEG)
    qaug = jnp.concatenate([qb, to_cols([selb] * HPG).astype(BF16),
                            jnp.zeros((2 * DH - DH - NSEL, R), BF16)], axis=0)

    def step(scores, vt, state):
        m_new = [jnp.maximum(state[r][0], jnp.max(scores[r], axis=0, keepdims=True)) for r in range(NR)]
        out = []
        for r in range(NR):
            p = jnp.exp2(scores[r] - m_new[r]).astype(BF16)
            acc = state[r][1] * jnp.exp2(state[r][0] - m_new[r]) + _dot(vt[r], p)
            out.append((m_new[r], acc))
        return out

    scores, vts = [], []
    for r in range(NR):
        s = _dot(ktile(ks_ref, t0, (r + 1) * QT), qaug[:, cols(r)])
        parts = [krows(s, c) for c in range(r + 1)]
        parts[r] = parts[r] + nb0
        if r >= 1:
            parts[r - 1] = parts[r - 1] + nb1
        scores.append(jnp.concatenate(parts, axis=0))
        vts.append(vslab(vs_ref, qi * NR, r + 1))
    state = step(scores, vts, [(jnp.full((1, RB), NEG, F32), jnp.zeros((NSA_VR, RB), F32))] * NR)

    def past_tile(kt, flat):
        kk = ktile(ks_ref, kt * TQ, TQ)
        vt = vslab(vs_ref, kt * NR, NR)
        scores = [_dot(kk, qaug[:, cols(r)]) for r in range(NR)]
        corner = scores[0][TQ - QT:] + jnp.where(kt == qi - 1, nb1, 0.0)
        scores[0] = jnp.concatenate([scores[0][:TQ - QT], corner], axis=0)
        new = step(scores, [vt] * NR, [(flat[2 * r], flat[2 * r + 1]) for r in range(NR)])
        return tuple(x for pair in new for x in pair)

    flat = lax.fori_loop(0, qi, past_tile, tuple(x for pair in state for x in pair))
    acc_s = jnp.concatenate([flat[2 * r + 1] for r in range(NR)], axis=1)

    gt = _sigmoid(gt_ref[...]).T

    def gate_row(br):
        return to_cols([gt[br * HPG + h:br * HPG + h + 1, :] for h in range(HPG)])

    def inv_l(acc):
        l = acc[DH:DH + 1, :]
        return 1.0 / jnp.where(l > 0.0, l, 1.0)

    o = (gate_row(0) * o_cmp + (gate_row(1) * inv_l(acc_s)) * acc_s[:DH]
         + (gate_row(2) * inv_l(acc_w)) * acc_w[:DH])
    o_hd = jnp.concatenate(
        [jnp.concatenate([o[:, r * RB + h * QT:r * RB + (h + 1) * QT] for r in range(NR)], axis=1)
         for h in range(HPG)], axis=0)
    o_ref[...] = o_hd.T.astype(o_ref.dtype)


def _nsa(z, zg, prep, bias, q_gain, ovm, batch, seq):
    N = z.shape[0]
    G, HPG, DH, TQ = NSA_G, NSA_HPG, NSA_DH, NSA_TQ
    nq = seq // TQ
    oks, ovs, okw, ovw, okc, ovc = prep
    nbias, cbias = bias
    QW = HPG * DH
    qg = jnp.broadcast_to(q_gain.reshape(DH, 1), (DH, QT))

    def kvspec(a):
        return pl.BlockSpec((1, 1) + a.shape[2:], lambda p, i, _n=a.ndim - 2: (p // G, p % G) + (0,) * _n)

    return pl.pallas_call(
        _nsa_kernel,
        grid=(batch * G, nq),
        in_specs=[pl.BlockSpec((TQ, QW), lambda p, i: ((p // G) * nq + i, OFF_NQ // QW + p % G)),
                  pl.BlockSpec((TQ, 128), lambda p, i: ((p // G) * nq + i, p % G)),
                  kvspec(oks), kvspec(ovs), kvspec(okw), kvspec(ovw), kvspec(okc), kvspec(ovc),
                  pl.BlockSpec((HPG, QT, TQ), lambda p, i: (p % G, 0, i)),
                  pl.BlockSpec((HPG, 2, QT, QT), lambda p, i: (p % G, 0, 0, 0)),
                  pl.BlockSpec((DH, QT), lambda p, i: (0, 0)),
                  pl.BlockSpec((QT, QT), lambda p, i: (0, 0))],
        out_specs=pl.BlockSpec((TQ, QW), lambda p, i: ((p // G) * nq + i, p % G)),
        out_shape=jax.ShapeDtypeStruct((N, G * QW), BF16),
        compiler_params=_cparams(("parallel", "arbitrary")),
        name="nsa_attention",
    )(z, zg, oks, ovs, okw, ovw, okc, ovc, cbias, nbias, qg, ovm)


def _merge_kernel(h_ref, mod_ref, za_ref, zb_ref, zc_ref, oa_ref, ys_ref, oc_ref,
                  wa_ref, wb_ref, wc_ref, wo_ref, o_ref):
    D = h_ref.shape[1]
    ya = _dot(oa_ref[...], wa_ref[...])
    zz = _dot(ys_ref[...], wb_ref[...])
    yb = zz[:, :D] * _sigmoid(zz[:, D:])
    yc = _dot(oc_ref[...], wc_ref[...])
    merged = _sigmoid(za_ref[...]) * ya + _sigmoid(zb_ref[...]) * yb + _sigmoid(zc_ref[...]) * yc
    o_ref[...] = h_ref[...] + mod_ref[0, 5:6, :] * _dot(merged.astype(BF16), wo_ref[...])


def _merge(h, mod, z, oa, ys, oc, wa, wb, wc, wo, seq):
    N, D = h.shape
    tm = 256
    tpb = seq // tm
    row = lambda w: pl.BlockSpec((tm, w), lambda i: (i, 0))
    res = lambda a: pl.BlockSpec(a.shape, lambda i: (0, 0), pipeline_mode=pl.Buffered(1))
    return pl.pallas_call(
        _merge_kernel,
        grid=(N // tm,),
        in_specs=[row(D),
                  pl.BlockSpec((1, 9, D), lambda i: (i // tpb, 0, 0)),
                  pl.BlockSpec((tm, D), lambda i: (i, OFF_ZA // D)),
                  pl.BlockSpec((tm, D), lambda i: (i, OFF_ZB // D)),
                  pl.BlockSpec((tm, D), lambda i: (i, OFF_ZC // D)),
                  row(oa.shape[1]), row(ys.shape[1]), row(oc.shape[1]),
                  res(wa), res(wb), res(wc), res(wo)],
        out_specs=row(D),
        out_shape=jax.ShapeDtypeStruct((N, D), F32),
        compiler_params=_cparams(("parallel",)),
        name="mix_merge",
    )(h, mod, z, z, z, oa, ys, oc, wa, wb, wc, wo)


def _split_w_in(w):
    gate0 = 4 * 512 + 512 + 1024 + 6 * 256
    ngate = 3 * NSA_G * NSA_HPG
    L, D = w.shape[:2]
    gw = w[:, :, gate0:gate0 + ngate].reshape(L, D, 3, NSA_G, NSA_HPG)
    gw = gw.transpose(0, 1, 3, 2, 4).reshape(L, D, NSA_G, 3 * NSA_HPG)
    gw = jnp.pad(gw, ((0, 0), (0, 0), (0, 0), (0, 128 - 3 * NSA_HPG))).reshape(L, D, NSA_G * 128)
    return w[:, :, :gate0].astype(BF16), w[:, :, gate0 + ngate:].astype(BF16), gw.astype(BF16)


def _overlap_matrix():
    j = jnp.arange(QT)[:, None]
    n = jnp.arange(QT)[None, :]
    st = n * CMP_STRIDE
    ov = (st < j * SEL_LEN + SEL_LEN) & (st + CMP_LEN > j * SEL_LEN)
    return jnp.where(ov, 1.0, 0.0).astype(BF16)


def kernel(x, c, ada_w, ada_b, norm_g, ffn1_wi, ffn1_wo, ffn2_wi, ffn2_wo, w_in, hg_lb_logits,
           hg_onorm, hg_proj, ssm_a_re, ssm_a_im, ssm_log_dt, ssm_b_re, ssm_b_im, ssm_c_re,
           ssm_c_im, ssm_d, ssm_glu_w, nsa_q_gain, nsa_k_gain, nsa_pe_k, nsa_pe_v, nsa_phi_k,
           nsa_phi_v, nsa_proj, rel_table, w_out):
    B, S, D = x.shape
    L = ada_w.shape[0]
    N = B * S
    assert S % 512 == 0 and S // SEL_LEN == 32 and S // QT == 16
    lb_cum = jnp.cumsum(jax.nn.softmax(hg_lb_logits.astype(F32), axis=0), axis=0)
    lower_bounds = lb_cum - lb_cum[0:1]
    mods = _mods(c, ada_w, ada_b).reshape(L, B, 9, D)
    bias = _bias_tables(rel_table, S)
    ovm = _overlap_matrix()
    w_mix, w_z, w_gate = _split_w_in(w_in)
    h = x.reshape(N, D)
    for l in range(L):
        mod = mods[l]
        h = _ffn(h, mod, norm_g[l, 0:1], ffn1_wi, ffn1_wo, l, 0, S)
        z, zg = _win(h, mod, norm_g[l, 1:2], w_mix, w_z, w_gate, l, S)
        oa = _hgrn(z, lower_bounds[l:l + 1], hg_onorm[l:l + 1], B, S)
        sp = _ssm_params(ssm_a_re[l], ssm_a_im[l], ssm_log_dt[l], ssm_b_re[l], ssm_b_im[l],
                         ssm_c_re[l], ssm_c_im[l])
        ys = _ssm(z, sp, ssm_d[l:l + 1], B, S)
        prep = _nsa_prep(z, nsa_k_gain[l:l + 1], nsa_pe_k[l], nsa_pe_v[l],
                         nsa_phi_k[l].astype(BF16), nsa_phi_v[l].astype(BF16), B, S)
        oc = _nsa(z, zg, prep, bias, nsa_q_gain[l:l + 1], ovm, B, S)
        h = _merge(h, mod, z, oa, ys, oc, hg_proj[l].astype(BF16), ssm_glu_w[l].astype(BF16),
                   nsa_proj[l].astype(BF16), w_out[l].astype(BF16), S)
        h = _ffn(h, mod, norm_g[l, 2:3], ffn2_wi, ffn2_wo, l, 6, S)
    return h.reshape(B, S, D)
```

```python
import functools
import math

import jax
import jax.numpy as jnp
from jax import lax
from jax.experimental import pallas as pl
from jax.experimental.pallas import tpu as pltpu

F32 = jnp.float32
BF16 = jnp.bfloat16

EPS = 1e-6
NEG = -1e30
LOG2E = 1.4426950408889634
FORCE_SCORE = 1e4

HG_HEADS, HG_D, HG_CHUNK = 4, 128, 16
SSM_GROUPS, SSM_P, SSM_N, SSM_CHUNK = 32, 16, 64, 8
SSM_BUNDLE = 8
SSM_TAP_BUNDLE = 16
NSA_G, NSA_HPG, NSA_DH = 4, 4, 64
CMP_LEN, CMP_STRIDE, SEL_LEN, SEL_TOPN, WIN = 32, 16, 64, 16, 512
REL_BUCKETS, REL_MAX_DIST = 32, 128
QT = 128
NSA_TQ = 512
NSA_VR = NSA_DH + 16

VMEM_LIMIT = 60 * 1024 * 1024

OFF_ZA, OFF_ZB, OFF_ZC = 0, 2048, 4096
OFF_HQ, OFF_HF, OFF_HI, OFF_HG = 6144, 6656, 7168, 7680
OFF_SU, OFF_NQ = 8192, 8704
OFF_KC, OFF_VC, OFF_KS, OFF_VS, OFF_KW, OFF_VW = 9728, 9984, 10240, 10496, 10752, 11008
Z_WIDTH = 11264


def _cparams(sem):
    return pltpu.CompilerParams(dimension_semantics=sem, vmem_limit_bytes=VMEM_LIMIT)


def _dot(a, b):
    return jnp.dot(a, b, preferred_element_type=F32)


def _dot_nt(a, b):
    return lax.dot_general(a, b, (((1,), (1,)), ((), ())), preferred_element_type=F32)


def _dot_tn(a, b):
    return lax.dot_general(a, b, (((0,), (0,)), ((), ())), preferred_element_type=F32)


def _sigmoid(x):
    return 1.0 / (1.0 + jnp.exp(-x))


def _split3(x):
    hi = x.astype(BF16)
    r = x - hi.astype(F32)
    mid = r.astype(BF16)
    lo = (r - mid.astype(F32)).astype(BF16)
    return hi, mid, lo


def _norm_mod(x, gain, shift, scale):
    ms = jnp.mean(x * x, axis=-1, keepdims=True)
    y = x * lax.rsqrt(ms + EPS) * gain
    return y * (1.0 + scale) + shift


def _mod_kernel(c_ref, w_ref, b_ref, o_ref):
    c = c_ref[...]
    ca = (c * _sigmoid(c)).astype(BF16)
    o_ref[0] = _dot(ca, w_ref[0].astype(BF16)) + b_ref[0]


def _mods(c, ada_w, ada_b):
    L, D, W = ada_w.shape
    B = c.shape[0]
    tn = 1024
    return pl.pallas_call(
        _mod_kernel,
        grid=(L, W // tn),
        in_specs=[pl.BlockSpec((B, D), lambda l, j: (0, 0)),
                  pl.BlockSpec((1, D, tn), lambda l, j: (l, 0, j)),
                  pl.BlockSpec((1, 1, tn), lambda l, j: (l, 0, j))],
        out_specs=pl.BlockSpec((1, B, tn), lambda l, j: (l, 0, j)),
        out_shape=jax.ShapeDtypeStruct((L, B, W), F32),
        compiler_params=_cparams(("parallel", "parallel")),
        name="adaln_mod",
    )(c, ada_w, ada_b.reshape(L, 1, W))


def _ffn_kernel(h_ref, mod_ref, g_ref, wi1_ref, wi2_ref, wo_ref, o_ref, u_s, *, k0, nf):
    f = pl.program_id(1)

    @pl.when(f == 0)
    def _():
        u = _norm_mod(h_ref[...], g_ref[...], mod_ref[0, k0:k0 + 1, :], mod_ref[0, k0 + 1:k0 + 2, :])
        u_s[...] = u.astype(BF16)
        o_ref[...] = jnp.zeros_like(o_ref)

    u = u_s[...]
    a1 = _dot(u, wi1_ref[0].astype(BF16))
    a2 = _dot(u, wi2_ref[0].astype(BF16))
    act = (a1 * _sigmoid(a1) * a2).astype(BF16)
    o_ref[...] += _dot(act, wo_ref[0].astype(BF16))

    @pl.when(f == nf - 1)
    def _():
        o_ref[...] = h_ref[...] + (0.5 * mod_ref[0, k0 + 2:k0 + 3, :]) * o_ref[...]


def _ffn(h, mod, gain, wi, wo, layer, k0, seq):
    N, D = h.shape
    dff = wo.shape[1]
    tm, tf = 1024, 256
    nf = dff // tf
    tpb = seq // tm
    return pl.pallas_call(
        functools.partial(_ffn_kernel, k0=k0, nf=nf),
        grid=(N // tm, nf),
        in_specs=[pl.BlockSpec((tm, D), lambda i, f: (i, 0)),
                  pl.BlockSpec((1, 9, D), lambda i, f: (i // tpb, 0, 0)),
                  pl.BlockSpec((1, D), lambda i, f: (0, 0)),
                  pl.BlockSpec((1, D, tf), lambda i, f: (layer, 0, f)),
                  pl.BlockSpec((1, D, tf), lambda i, f: (layer, 0, f + nf)),
                  pl.BlockSpec((1, tf, D), lambda i, f: (layer, f, 0))],
        out_specs=pl.BlockSpec((tm, D), lambda i, f: (i, 0)),
        out_shape=jax.ShapeDtypeStruct((N, D), F32),
        scratch_shapes=[pltpu.VMEM((tm, D), BF16)],
        compiler_params=_cparams(("parallel", "arbitrary")),
        name="ffn",
    )(h, mod, gain, wi, wi, wo)


def _win_kernel(h_ref, mod_ref, g_ref, w_ref, wg_ref, z_ref, zg_ref, u_s):
    j = pl.program_id(1)

    @pl.when(j == 0)
    def _():
        u = _norm_mod(h_ref[...], g_ref[...], mod_ref[0, 3:4, :], mod_ref[0, 4:5, :])
        ub = u.astype(BF16)
        u_s[...] = ub
        zg_ref[...] = _dot(ub, wg_ref[0])

    z_ref[...] = _dot(u_s[...], w_ref[0])


def _win(h, mod, gain, w_main, w_gate, layer, seq):
    N, D = h.shape
    tm, tn = 1024, 1024
    tpb = seq // tm
    GW = w_gate.shape[2]
    return pl.pallas_call(
        _win_kernel,
        grid=(N // tm, Z_WIDTH // tn),
        in_specs=[pl.BlockSpec((tm, D), lambda i, j: (i, 0)),
                  pl.BlockSpec((1, 9, D), lambda i, j: (i // tpb, 0, 0)),
                  pl.BlockSpec((1, D), lambda i, j: (0, 0)),
                  pl.BlockSpec((1, D, tn), lambda i, j: (layer, 0, j)),
                  pl.BlockSpec((1, D, GW), lambda i, j: (layer, 0, 0))],
        out_specs=[pl.BlockSpec((tm, tn), lambda i, j: (i, j)),
                   pl.BlockSpec((tm, GW), lambda i, j: (i, 0))],
        out_shape=[jax.ShapeDtypeStruct((N, Z_WIDTH), F32),
                   jax.ShapeDtypeStruct((N, GW), F32)],
        scratch_shapes=[pltpu.VMEM((tm, D), BF16)],
        compiler_params=_cparams(("parallel", "arbitrary")),
        name="in_proj",
    )(h, mod, gain, w_main, w_gate)


def _hgrn_kernel(q_ref, f_ref, i_ref, g_ref, lb_ref, on_ref, o_ref, st_ref, *, tb):
    C, H, DK = HG_CHUNK, HG_HEADS, HG_D
    nc = tb // C
    W = H * DK

    @pl.when(pl.program_id(1) == 0)
    def _():
        st_ref[...] = jnp.zeros_like(st_ref)

    q = q_ref[...]
    qs = q * _sigmoid(q)
    x = f_ref[...]
    iv = i_ref[...]
    lb = lb_ref[...]
    sp = jnp.log1p(jnp.exp(-jnp.abs(x)))
    lsig = jnp.minimum(x, 0.0) - sp
    a = jnp.log(jnp.maximum(lb, 1e-38))
    bterm = jnp.log1p(-lb) + lsig
    lae = jnp.maximum(a, bterm) + jnp.log1p(jnp.exp(-jnp.abs(a - bterm)))
    log_f = jnp.where(lb > 0.0, lae, bterm)
    k = (1.0 - lb) * jnp.exp(jnp.minimum(-x, 0.0) - sp)

    r = lax.broadcasted_iota(jnp.int32, (tb, tb), 0)
    cidx = lax.broadcasted_iota(jnp.int32, (tb, tb), 1)
    lmat = jnp.where(((r // C) == (cidx // C)) & (cidx <= r), 1.0, 0.0).astype(BF16)
    hi, mid, lo = _split3(log_f)
    b = _dot(lmat, hi) + _dot(lmat, mid) + _dot(lmat, lo)

    qe = (qs * jnp.exp(b)).astype(BF16)
    b3 = b.reshape(nc, C, W)
    bend = b3[:, C - 1:C, :]
    kdec = (k.reshape(nc, C, W) * jnp.exp(bend - b3)).reshape(tb, W).astype(BF16)
    ebend = jnp.exp(bend)
    ib = iv.astype(BF16)
    o_heads = []
    for h in range(H):
        ls = slice(h * DK, (h + 1) * DK)
        uts = [_dot_tn(ib[c * C:(c + 1) * C, ls], kdec[c * C:(c + 1) * C, ls]) for c in range(nc)]
        st = st_ref[h]
        sts = []
        for c in range(nc):
            sts.append(st.astype(BF16))
            st = st * ebend[c][:, ls] + uts[c]
        st_ref[h] = st
        o_heads.append(jnp.concatenate(
            [_dot_nt(qe[c * C:(c + 1) * C, ls], sts[c]) for c in range(nc)], axis=0))
    o = jnp.concatenate(o_heads, axis=1)

    b2 = b * LOG2E
    tmod = lax.broadcasted_iota(jnp.int32, (tb, W), 0) % C
    rr = lax.broadcasted_iota(jnp.int32, (W, W), 0) // DK
    cc = lax.broadcasted_iota(jnp.int32, (W, W), 1) // DK
    ones_bd = jnp.where(rr == cc, 1.0, 0.0).astype(BF16)
    SUB = 8

    def shifted_terms(qx, bx, kx, bsrc, isrc, shift, rowmod):
        if shift == 0:
            p = qx * kx * jnp.exp2(bx - bsrc)
            isd = isrc
        else:
            kd = pltpu.roll(kx, shift, 0)
            bd = pltpu.roll(bsrc, shift, 0)
            isd = pltpu.roll(isrc, shift, 0)
            p = jnp.where(rowmod >= shift, qx * kd * jnp.exp2(bx - bd), 0.0)
        return _dot(p.astype(BF16), ones_bd) * isd

    for d in range(SUB):
        if d == 0:
            o = o + _dot((qs * k).astype(BF16), ones_bd) * iv
        else:
            o = o + shifted_terms(qs, b2, k, b2, iv, d, tmod)

    def group(x, i):
        return x.reshape(nc, C // SUB, SUB, W)[:, i].reshape(nc * SUB, W)

    rmod = lax.broadcasted_iota(jnp.int32, (nc * SUB, W), 0) % SUB
    q_hi, b_hi = group(qs, 1), group(b2, 1)
    k_lo, b_lo, i_lo = group(k, 0), group(b2, 0), group(iv, 0)
    o_hi = group(o, 1)
    for d in range(SUB, C):
        o_hi = o_hi + shifted_terms(q_hi, b_hi, k_lo, b_lo, i_lo, d - SUB, rmod)
    o = jnp.stack([group(o, 0).reshape(nc, SUB, W), o_hi.reshape(nc, SUB, W)], axis=1).reshape(tb, W)

    g = g_ref[...]
    gs = g * _sigmoid(g)
    onw = on_ref[...]
    outs = []
    for h in range(H):
        ls = slice(h * DK, (h + 1) * DK)
        oh = o[:, ls]
        outs.append(oh * lax.rsqrt(jnp.mean(oh * oh, axis=-1, keepdims=True) + EPS) * onw * gs[:, ls])
    o_ref[...] = jnp.concatenate(outs, axis=1).astype(o_ref.dtype)


def _hgrn(z, lb, onorm, batch, seq):
    N = z.shape[0]
    tb = 256
    nb = seq // tb
    W = HG_HEADS * HG_D

    def zspec(off):
        return pl.BlockSpec((tb, W), lambda b, j: (b * nb + j, off // W))

    return pl.pallas_call(
        functools.partial(_hgrn_kernel, tb=tb),
        grid=(batch, nb),
        in_specs=[zspec(OFF_HQ), zspec(OFF_HF), zspec(OFF_HI), zspec(OFF_HG),
                  pl.BlockSpec((1, W), lambda b, j: (0, 0)),
                  pl.BlockSpec((1, HG_D), lambda b, j: (0, 0))],
        out_specs=pl.BlockSpec((tb, W), lambda b, j: (b * nb + j, 0)),
        out_shape=jax.ShapeDtypeStruct((N, W), BF16),
        scratch_shapes=[pltpu.VMEM((HG_HEADS, HG_D, HG_D), F32)],
        compiler_params=_cparams(("parallel", "arbitrary")),
        name="hgrn2",
    )(z, z, z, z, lb, onorm)


def _ssm_kernel(u_ref, bm_ref, c_ref, kt_ref, p1r_ref, p1i_ref, p2r_ref, p2i_ref,
                alr_ref, ali_ref, d_ref, o_ref, xr_ref, xi_ref, xpr_ref, xpi_ref, *, tt):
    LC = SSM_CHUNK
    nc = tt // LC
    W = SSM_GROUPS * SSM_N

    @pl.when(pl.program_id(1) == 0)
    def _():
        xr_ref[...] = jnp.zeros_like(xr_ref)
        xi_ref[...] = jnp.zeros_like(xi_ref)

    u = u_ref[...]
    ub = u.astype(BF16)
    nb, cw, sw = bm_ref.shape[0], bm_ref.shape[1], bm_ref.shape[2] // 2
    bu = [_dot(ub[:, s * cw:(s + 1) * cw], bm_ref[s]) for s in range(nb)]
    bur = jnp.concatenate([x[:, :sw] for x in bu], axis=1).reshape(nc, LC, W)
    bui = jnp.concatenate([x[:, sw:] for x in bu], axis=1).reshape(nc, LC, W)
    p1r, p1i = p1r_ref[...], p1i_ref[...]
    vr = jnp.sum(bur * p1r - bui * p1i, axis=1)
    vi = jnp.sum(bur * p1i + bui * p1r, axis=1)

    alr, ali = alr_ref[...], ali_ref[...]
    xr, xi = xr_ref[...], xi_ref[...]
    for c in range(nc):
        xpr_ref[c:c + 1, :] = xr
        xpi_ref[c:c + 1, :] = xi
        nr = alr * xr - ali * xi + vr[c:c + 1, :]
        ni = alr * xi + ali * xr + vi[c:c + 1, :]
        xr, xi = nr, ni
    xr_ref[...] = xr
    xi_ref[...] = xi

    xpr = xpr_ref[...][:, None, :]
    xpi = xpi_ref[...][:, None, :]
    p2r, p2i = p2r_ref[...], p2i_ref[...]
    zr = (p2r * xpr - p2i * xpi).reshape(tt, W).astype(BF16)
    zi = (p2r * xpi + p2i * xpr).reshape(tt, W).astype(BF16)
    y = jnp.concatenate(
        [_dot(jnp.concatenate([zr[:, s * sw:(s + 1) * sw], zi[:, s * sw:(s + 1) * sw]], axis=1), c_ref[s])
         for s in range(nb)], axis=1)

    tmod = lax.broadcasted_iota(jnp.int32, u.shape, 0) % LC
    nkb, kw = kt_ref.shape[1], kt_ref.shape[2]
    taps = [jnp.zeros((tt, kw), F32)] * nkb
    for tau in range(LC):
        if tau == 0:
            us = ub
        else:
            us = jnp.where(tmod >= tau, pltpu.roll(u, tau, 0), 0.0).astype(BF16)
        taps = [taps[s] + _dot(us[:, s * kw:(s + 1) * kw], kt_ref[tau, s]) for s in range(nkb)]
    y = y + jnp.concatenate(taps, axis=1) + d_ref[...] * u
    g = 0.5 * y * (1.0 + jnp.tanh(0.7978845608028654 * (y + 0.044715 * (y * y * y))))
    o_ref[...] = g.astype(o_ref.dtype)


def _ssm_params(a_re, a_im, log_dt, b_re, b_im, c_re, c_im):
    G, P, N, LC = SSM_GROUPS, SSM_P, SSM_N, SSM_CHUNK
    hp = lax.Precision.HIGHEST
    a_re = jnp.minimum(a_re.astype(F32), -1e-4)
    a_im = a_im.astype(F32)
    dt = jnp.exp(log_dt.astype(F32))[:, None]
    mag = jnp.exp(dt * a_re)
    ab_re, ab_im = mag * jnp.cos(dt * a_im), mag * jnp.sin(dt * a_im)
    den = a_re * a_re + a_im * a_im
    nr = ab_re - 1.0
    z_re = (nr * a_re + ab_im * a_im) / den
    z_im = (ab_im * a_re - nr * a_im) / den
    b_re, b_im = b_re.astype(F32), b_im.astype(F32)
    bb_re = z_re[..., None] * b_re - z_im[..., None] * b_im
    bb_im = z_re[..., None] * b_im + z_im[..., None] * b_re
    kk = jnp.arange(LC + 1, dtype=F32)[:, None, None]
    pm = jnp.exp(kk * dt * a_re)
    pw_re, pw_im = pm * jnp.cos(kk * dt * a_im), pm * jnp.sin(kk * dt * a_im)
    def bdiag(src):
        n, a, b = src.shape
        x = jnp.broadcast_to(src.reshape(n * a, 1, b), (n * a, n, b)).reshape(n * a, n * b)
        rg = jnp.arange(n * a)[:, None] // a
        cg = jnp.arange(n * b)[None, :] // b
        return jnp.where(rg == cg, x, 0.0)

    def bundles(src, per):
        return jax.vmap(bdiag)(src.reshape(G // per, per, *src.shape[1:]))

    bmat = jnp.concatenate([bundles(bb_re.transpose(0, 2, 1), SSM_BUNDLE),
                            bundles(bb_im.transpose(0, 2, 1), SSM_BUNDLE)], axis=2)
    c_re, c_im = c_re.astype(F32), c_im.astype(F32)
    cmat = jnp.concatenate([bundles(c_re.transpose(0, 2, 1), SSM_BUNDLE),
                            -bundles(c_im.transpose(0, 2, 1), SSM_BUNDLE)], axis=1)
    t_re = pw_re[:LC, :, :, None] * bb_re[None] - pw_im[:LC, :, :, None] * bb_im[None]
    t_im = pw_re[:LC, :, :, None] * bb_im[None] + pw_im[:LC, :, :, None] * bb_re[None]
    kt = (jnp.einsum('gqn,tgnp->tgpq', c_re, t_re, precision=hp)
          - jnp.einsum('gqn,tgnp->tgpq', c_im, t_im, precision=hp))
    ktm = jax.vmap(lambda k: bundles(k, SSM_TAP_BUNDLE))(kt)
    flat = lambda x: x.reshape(x.shape[0], G * N)
    p1r, p1i = flat(pw_re[LC - 1::-1][:LC]), flat(pw_im[LC - 1::-1][:LC])
    p2r, p2i = flat(pw_re[1:LC + 1]), flat(pw_im[1:LC + 1])
    alr, ali = flat(pw_re[LC:LC + 1]), flat(pw_im[LC:LC + 1])
    return (bmat.astype(BF16), cmat.astype(BF16), ktm.astype(BF16), p1r, p1i, p2r, p2i, alr, ali)


def _ssm(z, params, d_skip, batch, seq):
    N = z.shape[0]
    tt = 256
    nb = seq // tt
    CW = SSM_GROUPS * SSM_P
    W = SSM_GROUPS * SSM_N
    LC = SSM_CHUNK
    bmat, cmat, ktm, p1r, p1i, p2r, p2i, alr, ali = params
    full = lambda a: pl.BlockSpec(a.shape, lambda b, j, _n=a.ndim: (0,) * _n)
    return pl.pallas_call(
        functools.partial(_ssm_kernel, tt=tt),
        grid=(batch, nb),
        in_specs=[pl.BlockSpec((tt, CW), lambda b, j: (b * nb + j, OFF_SU // CW)),
                  full(bmat), full(cmat), full(ktm), full(p1r), full(p1i),
                  full(p2r), full(p2i), full(alr), full(ali),
                  pl.BlockSpec((1, CW), lambda b, j: (0, 0))],
        out_specs=pl.BlockSpec((tt, CW), lambda b, j: (b * nb + j, 0)),
        out_shape=jax.ShapeDtypeStruct((N, CW), BF16),
        scratch_shapes=[pltpu.VMEM((1, W), F32), pltpu.VMEM((1, W), F32),
                        pltpu.VMEM((tt // LC, W), F32), pltpu.VMEM((tt // LC, W), F32)],
        compiler_params=_cparams(("parallel", "arbitrary")),
        name="s5_ssm",
    )(z, bmat, cmat, ktm, p1r, p1i, p2r, p2i, alr, ali, d_skip)


def _t5_bucket(dist):
    n = jnp.maximum(dist, 0)
    max_exact = REL_BUCKETS // 2
    nf = jnp.maximum(n, 1).astype(F32)
    large = max_exact + (jnp.log(nf / max_exact) / math.log(REL_MAX_DIST / max_exact)
                         * (REL_BUCKETS - max_exact)).astype(jnp.int32)
    large = jnp.minimum(large, REL_BUCKETS - 1)
    return jnp.where(n < max_exact, n, large)


def _bias_kernel(tab_ref, bk_ref, o_ref):
    h = pl.program_id(0)
    bk = bk_ref[...]
    base = tab_ref[REL_BUCKETS - 1, h]
    acc = jnp.full(bk.shape, NEG, F32)
    for k in range(REL_BUCKETS):
        acc = jnp.where(bk == k, (tab_ref[k, h] - base) * LOG2E, acc)
    o_ref[0] = acc


def _bias_tables(rel_table, seq):
    j = jnp.arange(QT)[:, None]
    i = jnp.arange(QT)[None, :]
    d0 = i - j
    b0 = jnp.where(d0 >= 0, _t5_bucket(d0), REL_BUCKETS)
    b1 = _t5_bucket(QT + i - j)
    n = jnp.arange(QT)[:, None]
    t = jnp.arange(seq)[None, :]
    n_cmp = (seq - CMP_LEN) // CMP_STRIDE + 1
    dc = t - (n * CMP_STRIDE + CMP_LEN - 1)
    bc = jnp.where((dc >= 0) & (n < n_cmp), _t5_bucket(dc), REL_BUCKETS)
    nh = rel_table.shape[1]
    tab = rel_table.astype(F32)

    def expand(bk):
        rows, lanes = bk.shape
        return pl.pallas_call(
            _bias_kernel,
            grid=(nh,),
            in_specs=[pl.BlockSpec(memory_space=pltpu.SMEM),
                      pl.BlockSpec((rows, lanes), lambda h: (0, 0))],
            out_specs=pl.BlockSpec((1, rows, lanes), lambda h: (h, 0, 0)),
            out_shape=jax.ShapeDtypeStruct((nh, rows, lanes), F32),
            compiler_params=_cparams(("arbitrary",)),
            name="nsa_bias_tables",
        )(tab, bk.astype(jnp.int32))

    near = expand(jnp.concatenate([b0, b1], axis=0)).reshape(nh, 2, QT, QT)
    return near, expand(bc)


def _head_rms(x, gain):
    return x * lax.rsqrt(jnp.mean(x * x, axis=-1, keepdims=True) + EPS) * gain


def _nsa_prep_kernel(kc_ref, vc_ref, ks_ref, vs_ref, kw_ref, vw_ref, kg_ref, pek_ref, pev_ref,
                     phik_ref, phiv_ref, oks_ref, ovs_ref, okw_ref, ovw_ref, okc_ref, ovc_ref,
                     x_ref, *, seq):
    DH = NSA_DH
    kg = kg_ref[...]
    srow = lax.broadcasted_iota(jnp.int32, (seq, DH), 0)
    lane = lax.broadcasted_iota(jnp.int32, (seq, DH), 1)
    onehot = jnp.where(lane == srow // SEL_LEN, 1.0, 0.0).astype(BF16)
    nkt = seq // QT
    npad = WIN // QT

    def value_tiles(v):
        return jnp.concatenate([v, jnp.ones_like(v)], axis=1).T[:NSA_VR].astype(BF16)

    for g in range(NSA_G):
        ls = slice(g * DH, (g + 1) * DH)
        ksn = _head_rms(ks_ref[:, ls], kg).astype(BF16)
        oks_ref[0, g] = jnp.concatenate([ksn, onehot], axis=1)
        okw_ref[0, g, 0:WIN, :] = jnp.zeros((WIN, DH), BF16)
        okw_ref[0, g, WIN:WIN + seq, :] = _head_rms(kw_ref[:, ls], kg).astype(BF16)
        vst = value_tiles(vs_ref[:, ls])
        vwt = value_tiles(vw_ref[:, ls])
        ovw_ref[0, g, 0:npad] = jnp.zeros((npad, NSA_VR, QT), BF16)
        for kt in range(nkt):
            ovs_ref[0, g, kt] = vst[:, kt * QT:(kt + 1) * QT]
            ovw_ref[0, g, npad + kt] = vwt[:, kt * QT:(kt + 1) * QT]
        for src_ref, pe_ref, phi_ref, is_k in ((kc_ref, pek_ref, phik_ref, True),
                                               (vc_ref, pev_ref, phiv_ref, False)):
            x_ref[...] = src_ref[:, ls]
            lo = jnp.zeros((QT, DH), F32)
            hi = jnp.zeros((QT, DH), F32)
            for r in range(CMP_STRIDE):
                xr = x_ref[pl.ds(r, QT, stride=CMP_STRIDE), :]
                lo = lo + _dot((xr + pe_ref[r:r + 1, :]).astype(BF16), phi_ref[r])
                r2 = r + CMP_STRIDE
                hi = hi + _dot((xr + pe_ref[r2:r2 + 1, :]).astype(BF16), phi_ref[r2])
            acc = lo + pltpu.roll(hi, QT - 1, 0)
            if is_k:
                okc_ref[0, g] = _head_rms(acc, kg).astype(BF16)
            else:
                ovc_ref[0, g] = value_tiles(acc)


def _nsa_prep(z, k_gain, pe_k, pe_v, phi_k, phi_v, batch, seq):
    G, DH = NSA_G, NSA_DH
    KW = G * DH

    def zspec(off):
        return pl.BlockSpec((seq, KW), lambda b: (b, off // KW))

    full = lambda a: pl.BlockSpec(a.shape, lambda b, _n=a.ndim: (0,) * _n)

    def ospec(*dims):
        return pl.BlockSpec((1, G) + dims, lambda b, _n=len(dims): (b, 0) + (0,) * _n)

    def oshape(*dims):
        return jax.ShapeDtypeStruct((batch, G) + dims, BF16)

    outs = [(seq, 2 * DH), (seq // QT, NSA_VR, QT), (seq + WIN, DH), ((seq + WIN) // QT, NSA_VR, QT),
            (QT, DH), (NSA_VR, QT)]
    return pl.pallas_call(
        functools.partial(_nsa_prep_kernel, seq=seq),
        grid=(batch,),
        in_specs=[zspec(OFF_KC), zspec(OFF_VC), zspec(OFF_KS), zspec(OFF_VS), zspec(OFF_KW),
                  zspec(OFF_VW), full(k_gain), full(pe_k), full(pe_v), full(phi_k), full(phi_v)],
        out_specs=[ospec(*d) for d in outs],
        out_shape=[oshape(*d) for d in outs],
        scratch_shapes=[pltpu.VMEM((seq, DH), F32)],
        compiler_params=_cparams(("parallel",)),
        name="nsa_kv_prep",
    )(z, z, z, z, z, z, k_gain, pe_k, pe_v, phi_k, phi_v)


def _nsa_kernel(q_ref, gt_ref, ks_ref, vs_ref, kw_ref, vw_ref, kc_ref, vc_ref, cb_ref, nb_ref,
                qg_ref, ov_ref, o_ref):
    DH, HPG, TQ = NSA_DH, NSA_HPG, NSA_TQ
    NR = TQ // QT
    RB = HPG * QT
    R = NR * RB
    NSEL = 32
    nwt = WIN // QT
    qi = pl.program_id(1)
    t0 = qi * TQ

    def cols(r):
        return slice(r * RB, (r + 1) * RB)

    def to_cols(per_head):
        return jnp.concatenate([per_head[h][:, r * QT:(r + 1) * QT] for r in range(NR) for h in range(HPG)],
                               axis=1)

    def krows(x, c):
        return x[c * QT:(c + 1) * QT]

    def vslab(ref, tile0, n):
        return jnp.concatenate([ref[0, 0, tile0 + c] for c in range(n)], axis=1)

    def ktile(ref, start, size):
        return ref[0, 0, pl.ds(pl.multiple_of(start, QT), size), :]

    qt = q_ref[...].T
    gain = jnp.concatenate([qg_ref[...]] * NR, axis=1) * (DH ** -0.5 * LOG2E)
    qh = []
    for h in range(HPG):
        x = qt[h * DH:(h + 1) * DH]
        qh.append(x * lax.rsqrt(jnp.mean(x * x, axis=0, keepdims=True) + EPS) * gain)
    qb = to_cols(qh).astype(BF16)
    nb0 = jnp.concatenate([nb_ref[h, 0] for h in range(HPG)], axis=1)
    nb1 = jnp.concatenate([nb_ref[h, 1] for h in range(HPG)], axis=1)

    jj = lax.broadcasted_iota(jnp.int32, (QT, RB), 0)
    ii = lax.broadcasted_iota(jnp.int32, (QT, RB), 1) % QT
    acc_w = []
    for r in range(NR):
        s = _dot(ktile(kw_ref, t0 + r * QT, WIN + QT), qb[:, cols(r)])
        parts = [krows(s, c) for c in range(nwt + 1)]
        parts[nwt] = parts[nwt] + nb0
        parts[nwt - 1] = parts[nwt - 1] + nb1
        parts[0] = jnp.where(jj > ii, parts[0], NEG)
        for c in range(nwt - r):
            parts[c] = jnp.where(qi > 0, parts[c], NEG)
        s = jnp.concatenate(parts, axis=0)
        p = jnp.exp2(s - jnp.max(s, axis=0, keepdims=True)).astype(BF16)
        acc_w.append(_dot(vslab(vw_ref, qi * NR + r, nwt + 1), p))
    acc_w = jnp.concatenate(acc_w, axis=1)

    sc = _dot(kc_ref[0, 0], qb) + to_cols([cb_ref[h] for h in range(HPG)])
    mc = jnp.max(sc, axis=0, keepdims=True)
    ec = jnp.where(sc > 0.5 * NEG, jnp.exp2(sc - mc), 0.0)
    lc = jnp.sum(ec, axis=0, keepdims=True)
    pc = ec / jnp.where(lc > 0.0, lc, 1.0)
    o_cmp = _dot(vc_ref[0, 0], pc.astype(BF16))[:DH]

    psum = jnp.concatenate(
        [sum(pc[:, r * RB + h * QT:r * RB + (h + 1) * QT] for h in range(HPG)) for r in range(NR)], axis=1)
    ovm = ov_ref[...]
    hi, mid, lo = _split3(psum)
    imp = (_dot(ovm, hi) + _dot(ovm, mid) + _dot(ovm, lo))[0:NSEL]
    jrow = lax.broadcasted_iota(jnp.int32, (NSEL, TQ), 0)
    blk = (t0 + lax.broadcasted_iota(jnp.int32, (NSEL, TQ), 1)) // SEL_LEN
    forced = (jrow == 0) | (jrow == blk) | (jrow == blk - 1)
    imp = jnp.where(forced, FORCE_SCORE, imp)
    imp = jnp.where(jrow <= blk, imp, NEG)
    cnt = jnp.zeros((NSEL, TQ), F32)
    for jp in range(NSEL):
        rowv = imp[jp:jp + 1, :]
        beats = (rowv > imp) | ((rowv == imp) & (jrow > jp))
        cnt = cnt + jnp.where(beats, 1.0, 0.0)
    selb = jnp.where(cnt < float(SEL_TOPN), 0.0, NEG)
    qaug = jnp.concatenate([qb, to_cols([selb] * HPG).astype(BF16),
                            jnp.zeros((2 * DH - DH - NSEL, R), BF16)], axis=0)

    def step(scores, vt, state):
        m_new = [jnp.maximum(state[r][0], jnp.max(scores[r], axis=0, keepdims=True)) for r in range(NR)]
        out = []
        for r in range(NR):
            p = jnp.exp2(scores[r] - m_new[r]).astype(BF16)
            acc = state[r][1] * jnp.exp2(state[r][0] - m_new[r]) + _dot(vt[r], p)
            out.append((m_new[r], acc))
        return out

    scores, vts = [], []
    for r in range(NR):
        s = _dot(ktile(ks_ref, t0, (r + 1) * QT), qaug[:, cols(r)])
        parts = [krows(s, c) for c in range(r + 1)]
        parts[r] = parts[r] + nb0
        if r >= 1:
            parts[r - 1] = parts[r - 1] + nb1
        scores.append(jnp.concatenate(parts, axis=0))
        vts.append(vslab(vs_ref, qi * NR, r + 1))
    state = step(scores, vts, [(jnp.full((1, RB), NEG, F32), jnp.zeros((NSA_VR, RB), F32))] * NR)

    def past_tile(kt, flat):
        kk = ktile(ks_ref, kt * TQ, TQ)
        vt = vslab(vs_ref, kt * NR, NR)
        scores = [_dot(kk, qaug[:, cols(r)]) for r in range(NR)]
        corner = scores[0][TQ - QT:] + jnp.where(kt == qi - 1, nb1, 0.0)
        scores[0] = jnp.concatenate([scores[0][:TQ - QT], corner], axis=0)
        new = step(scores, [vt] * NR, [(flat[2 * r], flat[2 * r + 1]) for r in range(NR)])
        return tuple(x for pair in new for x in pair)

    flat = lax.fori_loop(0, qi, past_tile, tuple(x for pair in state for x in pair))
    acc_s = jnp.concatenate([flat[2 * r + 1] for r in range(NR)], axis=1)

    gt = _sigmoid(gt_ref[...]).T

    def gate_row(br):
        return to_cols([gt[br * HPG + h:br * HPG + h + 1, :] for h in range(HPG)])

    def inv_l(acc):
        l = acc[DH:DH + 1, :]
        return 1.0 / jnp.where(l > 0.0, l, 1.0)

    o = (gate_row(0) * o_cmp + (gate_row(1) * inv_l(acc_s)) * acc_s[:DH]
         + (gate_row(2) * inv_l(acc_w)) * acc_w[:DH])
    o_hd = jnp.concatenate(
        [jnp.concatenate([o[:, r * RB + h * QT:r * RB + (h + 1) * QT] for r in range(NR)], axis=1)
         for h in range(HPG)], axis=0)
    o_ref[...] = o_hd.T.astype(o_ref.dtype)


def _nsa(z, zg, prep, bias, q_gain, ovm, batch, seq):
    N = z.shape[0]
    G, HPG, DH, TQ = NSA_G, NSA_HPG, NSA_DH, NSA_TQ
    nq = seq // TQ
    oks, ovs, okw, ovw, okc, ovc = prep
    nbias, cbias = bias
    QW = HPG * DH
    qg = jnp.broadcast_to(q_gain.reshape(DH, 1), (DH, QT))

    def kvspec(a):
        return pl.BlockSpec((1, 1) + a.shape[2:], lambda p, i, _n=a.ndim - 2: (p // G, p % G) + (0,) * _n)

    return pl.pallas_call(
        _nsa_kernel,
        grid=(batch * G, nq),
        in_specs=[pl.BlockSpec((TQ, QW), lambda p, i: ((p // G) * nq + i, OFF_NQ // QW + p % G)),
                  pl.BlockSpec((TQ, 128), lambda p, i: ((p // G) * nq + i, p % G)),
                  kvspec(oks), kvspec(ovs), kvspec(okw), kvspec(ovw), kvspec(okc), kvspec(ovc),
                  pl.BlockSpec((HPG, QT, TQ), lambda p, i: (p % G, 0, i)),
                  pl.BlockSpec((HPG, 2, QT, QT), lambda p, i: (p % G, 0, 0, 0)),
                  pl.BlockSpec((DH, QT), lambda p, i: (0, 0)),
                  pl.BlockSpec((QT, QT), lambda p, i: (0, 0))],
        out_specs=pl.BlockSpec((TQ, QW), lambda p, i: ((p // G) * nq + i, p % G)),
        out_shape=jax.ShapeDtypeStruct((N, G * QW), BF16),
        compiler_params=_cparams(("parallel", "arbitrary")),
        name="nsa_attention",
    )(z, zg, oks, ovs, okw, ovw, okc, ovc, cbias, nbias, qg, ovm)


def _merge_kernel(h_ref, mod_ref, za_ref, zb_ref, zc_ref, oa_ref, ys_ref, oc_ref,
                  wa_ref, wb_ref, wc_ref, wo_ref, o_ref):
    D = h_ref.shape[1]
    ya = _dot(oa_ref[...], wa_ref[...])
    zz = _dot(ys_ref[...], wb_ref[...])
    yb = zz[:, :D] * _sigmoid(zz[:, D:])
    yc = _dot(oc_ref[...], wc_ref[...])
    merged = _sigmoid(za_ref[...]) * ya + _sigmoid(zb_ref[...]) * yb + _sigmoid(zc_ref[...]) * yc
    o_ref[...] = h_ref[...] + mod_ref[0, 5:6, :] * _dot(merged.astype(BF16), wo_ref[...])


def _merge(h, mod, z, oa, ys, oc, wa, wb, wc, wo, seq):
    N, D = h.shape
    tm = 256
    tpb = seq // tm
    row = lambda w: pl.BlockSpec((tm, w), lambda i: (i, 0))
    res = lambda a: pl.BlockSpec(a.shape, lambda i: (0, 0), pipeline_mode=pl.Buffered(1))
    return pl.pallas_call(
        _merge_kernel,
        grid=(N // tm,),
        in_specs=[row(D),
                  pl.BlockSpec((1, 9, D), lambda i: (i // tpb, 0, 0)),
                  pl.BlockSpec((tm, D), lambda i: (i, OFF_ZA // D)),
                  pl.BlockSpec((tm, D), lambda i: (i, OFF_ZB // D)),
                  pl.BlockSpec((tm, D), lambda i: (i, OFF_ZC // D)),
                  row(oa.shape[1]), row(ys.shape[1]), row(oc.shape[1]),
                  res(wa), res(wb), res(wc), res(wo)],
        out_specs=row(D),
        out_shape=jax.ShapeDtypeStruct((N, D), F32),
        compiler_params=_cparams(("parallel",)),
        name="mix_merge",
    )(h, mod, z, z, z, oa, ys, oc, wa, wb, wc, wo)


def _permute_w_in(w):
    gate0 = 4 * 512 + 512 + 1024 + 6 * 256
    ngate = 3 * NSA_G * NSA_HPG
    L, D = w.shape[:2]
    wb = w.astype(BF16)
    main = jnp.concatenate([wb[:, :, gate0 + ngate:], wb[:, :, :gate0]], axis=2)
    gw = wb[:, :, gate0:gate0 + ngate].reshape(L, D, 3, NSA_G, NSA_HPG)
    gw = gw.transpose(0, 1, 3, 2, 4).reshape(L, D, NSA_G, 3 * NSA_HPG)
    gw = jnp.pad(gw, ((0, 0), (0, 0), (0, 0), (0, 128 - 3 * NSA_HPG))).reshape(L, D, NSA_G * 128)
    return main, gw


def _overlap_matrix():
    j = jnp.arange(QT)[:, None]
    n = jnp.arange(QT)[None, :]
    st = n * CMP_STRIDE
    ov = (st < j * SEL_LEN + SEL_LEN) & (st + CMP_LEN > j * SEL_LEN)
    return jnp.where(ov, 1.0, 0.0).astype(BF16)


def kernel(x, c, ada_w, ada_b, norm_g, ffn1_wi, ffn1_wo, ffn2_wi, ffn2_wo, w_in, hg_lb_logits,
           hg_onorm, hg_proj, ssm_a_re, ssm_a_im, ssm_log_dt, ssm_b_re, ssm_b_im, ssm_c_re,
           ssm_c_im, ssm_d, ssm_glu_w, nsa_q_gain, nsa_k_gain, nsa_pe_k, nsa_pe_v, nsa_phi_k,
           nsa_phi_v, nsa_proj, rel_table, w_out):
    B, S, D = x.shape
    L = ada_w.shape[0]
    N = B * S
    assert S % 512 == 0 and S // SEL_LEN == 32 and S // QT == 16
    lb_cum = jnp.cumsum(jax.nn.softmax(hg_lb_logits.astype(F32), axis=0), axis=0)
    lower_bounds = lb_cum - lb_cum[0:1]
    mods = _mods(c, ada_w, ada_b).reshape(L, B, 9, D)
    bias = _bias_tables(rel_table, S)
    ovm = _overlap_matrix()
    w_main, w_gate = _permute_w_in(w_in)
    h = x.reshape(N, D)
    for l in range(L):
        mod = mods[l]
        h = _ffn(h, mod, norm_g[l, 0:1], ffn1_wi, ffn1_wo, l, 0, S)
        z, zg = _win(h, mod, norm_g[l, 1:2], w_main, w_gate, l, S)
        oa = _hgrn(z, lower_bounds[l:l + 1], hg_onorm[l:l + 1], B, S)
        sp = _ssm_params(ssm_a_re[l], ssm_a_im[l], ssm_log_dt[l], ssm_b_re[l], ssm_b_im[l],
                         ssm_c_re[l], ssm_c_im[l])
        ys = _ssm(z, sp, ssm_d[l:l + 1], B, S)
        prep = _nsa_prep(z, nsa_k_gain[l:l + 1], nsa_pe_k[l], nsa_pe_v[l],
                         nsa_phi_k[l].astype(BF16), nsa_phi_v[l].astype(BF16), B, S)
        oc = _nsa(z, zg, prep, bias, nsa_q_gain[l:l + 1], ovm, B, S)
        h = _merge(h, mod, z, oa, ys, oc, hg_proj[l].astype(BF16), ssm_glu_w[l].astype(BF16),
                   nsa_proj[l].astype(BF16), w_out[l].astype(BF16), S)
        h = _ffn(h, mod, norm_g[l, 2:3], ffn2_wi, ffn2_wo, l, 6, S)
    return h.reshape(B, S, D)
```

```python
import functools
import math

import jax
import jax.numpy as jnp
from jax import lax
from jax.experimental import pallas as pl
from jax.experimental.pallas import tpu as pltpu

F32 = jnp.float32
BF16 = jnp.bfloat16

EPS = 1e-6
NEG = -1e30
LOG2E = 1.4426950408889634
FORCE_SCORE = 1e4

HG_HEADS, HG_D, HG_CHUNK = 4, 128, 16
SSM_GROUPS, SSM_P, SSM_N, SSM_CHUNK = 32, 16, 64, 8
SSM_BUNDLE = 8
SSM_TAP_BUNDLE = 16
NSA_G, NSA_HPG, NSA_DH = 4, 4, 64
CMP_LEN, CMP_STRIDE, SEL_LEN, SEL_TOPN, WIN = 32, 16, 64, 16, 512
REL_BUCKETS, REL_MAX_DIST = 32, 128
QT = 128
NSA_TQ = 512
NSA_VR = NSA_DH + 16

VMEM_LIMIT = 60 * 1024 * 1024

OFF_ZA, OFF_ZB, OFF_ZC = 0, 2048, 4096
OFF_HQ, OFF_HF, OFF_HI, OFF_HG = 6144, 6656, 7168, 7680
OFF_SU, OFF_NQ = 8192, 8704
OFF_KC, OFF_VC, OFF_KS, OFF_VS, OFF_KW, OFF_VW = 9728, 9984, 10240, 10496, 10752, 11008
Z_WIDTH = 11264


def _cparams(sem):
    return pltpu.CompilerParams(dimension_semantics=sem, vmem_limit_bytes=VMEM_LIMIT)


def _dot(a, b):
    return jnp.dot(a, b, preferred_element_type=F32)


def _dot_nt(a, b):
    return lax.dot_general(a, b, (((1,), (1,)), ((), ())), preferred_element_type=F32)


def _dot_tn(a, b):
    return lax.dot_general(a, b, (((0,), (0,)), ((), ())), preferred_element_type=F32)


def _sigmoid(x):
    return 1.0 / (1.0 + jnp.exp(-x))


def _split3(x):
    hi = x.astype(BF16)
    r = x - hi.astype(F32)
    mid = r.astype(BF16)
    lo = (r - mid.astype(F32)).astype(BF16)
    return hi, mid, lo


NORM_ROWS = 32


def _norm_mod_store(dst_ref, h_ref, gain, shift, scale):
    g2 = gain * (1.0 + scale)

    def body(c, carry):
        rows = pl.ds(pl.multiple_of(c * NORM_ROWS, NORM_ROWS), NORM_ROWS)
        x = h_ref[rows, :]
        ms = jnp.mean(x * x, axis=-1, keepdims=True)
        dst_ref[rows, :] = (x * lax.rsqrt(ms + EPS) * g2 + shift).astype(dst_ref.dtype)
        return carry

    lax.fori_loop(0, h_ref.shape[0] // NORM_ROWS, body, 0, unroll=8)


def _mod_kernel(c_ref, w_ref, b_ref, o_ref):
    c = c_ref[...]
    ca = (c * _sigmoid(c)).astype(BF16)
    o_ref[0] = _dot(ca, w_ref[0].astype(BF16)) + b_ref[0]


def _mods(c, ada_w, ada_b):
    L, D, W = ada_w.shape
    B = c.shape[0]
    tn = 1024
    return pl.pallas_call(
        _mod_kernel,
        grid=(L, W // tn),
        in_specs=[pl.BlockSpec((B, D), lambda l, j: (0, 0)),
                  pl.BlockSpec((1, D, tn), lambda l, j: (l, 0, j)),
                  pl.BlockSpec((1, 1, tn), lambda l, j: (l, 0, j))],
        out_specs=pl.BlockSpec((1, B, tn), lambda l, j: (l, 0, j)),
        out_shape=jax.ShapeDtypeStruct((L, B, W), F32),
        compiler_params=_cparams(("parallel", "parallel")),
        name="adaln_mod",
    )(c, ada_w, ada_b.reshape(L, 1, W))


def _ffn_kernel(h_ref, mod_ref, g_ref, wi_hbm, wo_hbm, o_ref, u_s, *, k0, nf, tf, layer):
    D = h_ref.shape[1]
    _norm_mod_store(u_s, h_ref, g_ref[...], mod_ref[0, k0:k0 + 1, :], mod_ref[0, k0 + 1:k0 + 2, :])
    o_ref[...] = jnp.zeros_like(o_ref)

    def ff_tile(wi1_ref, wi2_ref, wo_ref):
        u = u_s[...]
        a1 = _dot(u, wi1_ref[0].astype(BF16))
        a2 = _dot(u, wi2_ref[0].astype(BF16))
        act = (a1 * _sigmoid(a1) * a2).astype(BF16)
        o_ref[...] += _dot(act, wo_ref[0].astype(BF16))

    pltpu.emit_pipeline(
        ff_tile,
        grid=(nf,),
        in_specs=[pl.BlockSpec((1, D, tf), lambda f: (layer, 0, f)),
                  pl.BlockSpec((1, D, tf), lambda f: (layer, 0, f + nf)),
                  pl.BlockSpec((1, tf, D), lambda f: (layer, f, 0))],
    )(wi_hbm, wi_hbm, wo_hbm)
    o_ref[...] = h_ref[...] + (0.5 * mod_ref[0, k0 + 2:k0 + 3, :]) * o_ref[...]


def _ffn(h, mod, gain, wi, wo, layer, k0, seq):
    N, D = h.shape
    dff = wo.shape[1]
    tm, tf = 1024, 256
    nf = dff // tf
    tpb = seq // tm
    return pl.pallas_call(
        functools.partial(_ffn_kernel, k0=k0, nf=nf, tf=tf, layer=layer),
        grid=(N // tm,),
        in_specs=[pl.BlockSpec((tm, D), lambda i: (i, 0)),
                  pl.BlockSpec((1, 9, D), lambda i: (i // tpb, 0, 0)),
                  pl.BlockSpec((1, D), lambda i: (0, 0)),
                  pl.BlockSpec(memory_space=pl.ANY),
                  pl.BlockSpec(memory_space=pl.ANY)],
        out_specs=pl.BlockSpec((tm, D), lambda i: (i, 0)),
        out_shape=jax.ShapeDtypeStruct((N, D), F32),
        scratch_shapes=[pltpu.VMEM((tm, D), BF16)],
        compiler_params=_cparams(("parallel",)),
        name="ffn",
    )(h, mod, gain, wi, wo)


def _win_kernel(h_ref, mod_ref, g_ref, w_ref, wg_ref, z_ref, zg_ref, u_s):
    j = pl.program_id(1)

    @pl.when(j == 0)
    def _():
        _norm_mod_store(u_s, h_ref, g_ref[...], mod_ref[0, 3:4, :], mod_ref[0, 4:5, :])
        zg_ref[...] = _dot(u_s[...], wg_ref[0])

    z_ref[...] = _dot(u_s[...], w_ref[0])


def _win(h, mod, gain, w_main, w_gate, layer, seq):
    N, D = h.shape
    tm, tn = 1024, 1024
    tpb = seq // tm
    GW = w_gate.shape[2]
    return pl.pallas_call(
        _win_kernel,
        grid=(N // tm, Z_WIDTH // tn),
        in_specs=[pl.BlockSpec((tm, D), lambda i, j: (i, 0)),
                  pl.BlockSpec((1, 9, D), lambda i, j: (i // tpb, 0, 0)),
                  pl.BlockSpec((1, D), lambda i, j: (0, 0)),
                  pl.BlockSpec((1, D, tn), lambda i, j: (layer, 0, j)),
                  pl.BlockSpec((1, D, GW), lambda i, j: (layer, 0, 0))],
        out_specs=[pl.BlockSpec((tm, tn), lambda i, j: (i, j)),
                   pl.BlockSpec((tm, GW), lambda i, j: (i, 0))],
        out_shape=[jax.ShapeDtypeStruct((N, Z_WIDTH), F32),
                   jax.ShapeDtypeStruct((N, GW), F32)],
        scratch_shapes=[pltpu.VMEM((tm, D), BF16)],
        compiler_params=_cparams(("parallel", "arbitrary")),
        name="in_proj",
    )(h, mod, gain, w_main, w_gate)


def _hgrn_kernel(q_ref, f_ref, i_ref, g_ref, lb_ref, on_ref, o_ref, st_ref, *, tb):
    C, H, DK = HG_CHUNK, HG_HEADS, HG_D
    nc = tb // C
    W = H * DK

    @pl.when(pl.program_id(1) == 0)
    def _():
        st_ref[...] = jnp.zeros_like(st_ref)

    q = q_ref[...]
    qs = q * _sigmoid(q)
    x = f_ref[...]
    iv = i_ref[...]
    lb = lb_ref[...]
    sp = jnp.log1p(jnp.exp(-jnp.abs(x)))
    lsig = jnp.minimum(x, 0.0) - sp
    a = jnp.log(jnp.maximum(lb, 1e-38))
    bterm = jnp.log1p(-lb) + lsig
    lae = jnp.maximum(a, bterm) + jnp.log1p(jnp.exp(-jnp.abs(a - bterm)))
    log_f = jnp.where(lb > 0.0, lae, bterm)
    k = (1.0 - lb) * jnp.exp(jnp.minimum(-x, 0.0) - sp)

    r = lax.broadcasted_iota(jnp.int32, (tb, tb), 0)
    cidx = lax.broadcasted_iota(jnp.int32, (tb, tb), 1)
    lmat = jnp.where(((r // C) == (cidx // C)) & (cidx <= r), 1.0, 0.0).astype(BF16)
    hi, mid, lo = _split3(log_f)
    b = _dot(lmat, hi) + _dot(lmat, mid) + _dot(lmat, lo)

    qe = (qs * jnp.exp(b)).astype(BF16)
    b3 = b.reshape(nc, C, W)
    bend = b3[:, C - 1:C, :]
    kdec = (k.reshape(nc, C, W) * jnp.exp(bend - b3)).reshape(tb, W).astype(BF16)
    ebend = jnp.exp(bend)
    ib = iv.astype(BF16)
    o_heads = []
    for h in range(H):
        ls = slice(h * DK, (h + 1) * DK)
        uts = [_dot_tn(ib[c * C:(c + 1) * C, ls], kdec[c * C:(c + 1) * C, ls]) for c in range(nc)]
        st = st_ref[h]
        sts = []
        for c in range(nc):
            sts.append(st.astype(BF16))
            st = st * ebend[c][:, ls] + uts[c]
        st_ref[h] = st
        o_heads.append(jnp.concatenate(
            [_dot_nt(qe[c * C:(c + 1) * C, ls], sts[c]) for c in range(nc)], axis=0))
    o = jnp.concatenate(o_heads, axis=1)

    b2 = b * LOG2E
    tmod = lax.broadcasted_iota(jnp.int32, (tb, W), 0) % C
    rr = lax.broadcasted_iota(jnp.int32, (W, W), 0) // DK
    cc = lax.broadcasted_iota(jnp.int32, (W, W), 1) // DK
    ones_bd = jnp.where(rr == cc, 1.0, 0.0).astype(BF16)
    SUB = 8

    def shifted_terms(qx, bx, kx, bsrc, isrc, shift, rowmod):
        if shift == 0:
            p = qx * kx * jnp.exp2(bx - bsrc)
            isd = isrc
        else:
            kd = pltpu.roll(kx, shift, 0)
            bd = pltpu.roll(bsrc, shift, 0)
            isd = pltpu.roll(isrc, shift, 0)
            p = jnp.where(rowmod >= shift, qx * kd * jnp.exp2(bx - bd), 0.0)
        return _dot(p.astype(BF16), ones_bd) * isd

    for d in range(SUB):
        if d == 0:
            o = o + _dot((qs * k).astype(BF16), ones_bd) * iv
        else:
            o = o + shifted_terms(qs, b2, k, b2, iv, d, tmod)

    def group(x, i):
        return x.reshape(nc, C // SUB, SUB, W)[:, i].reshape(nc * SUB, W)

    rmod = lax.broadcasted_iota(jnp.int32, (nc * SUB, W), 0) % SUB
    q_hi, b_hi = group(qs, 1), group(b2, 1)
    k_lo, b_lo, i_lo = group(k, 0), group(b2, 0), group(iv, 0)
    o_hi = group(o, 1)
    for d in range(SUB, C):
        o_hi = o_hi + shifted_terms(q_hi, b_hi, k_lo, b_lo, i_lo, d - SUB, rmod)
    o = jnp.stack([group(o, 0).reshape(nc, SUB, W), o_hi.reshape(nc, SUB, W)], axis=1).reshape(tb, W)

    g = g_ref[...]
    gs = g * _sigmoid(g)
    onw = on_ref[...]
    outs = []
    for h in range(H):
        ls = slice(h * DK, (h + 1) * DK)
        oh = o[:, ls]
        outs.append(oh * lax.rsqrt(jnp.mean(oh * oh, axis=-1, keepdims=True) + EPS) * onw * gs[:, ls])
    o_ref[...] = jnp.concatenate(outs, axis=1).astype(o_ref.dtype)


def _hgrn(z, lb, onorm, batch, seq):
    N = z.shape[0]
    tb = 256
    nb = seq // tb
    W = HG_HEADS * HG_D

    def zspec(off):
        return pl.BlockSpec((tb, W), lambda b, j: (b * nb + j, off // W))

    return pl.pallas_call(
        functools.partial(_hgrn_kernel, tb=tb),
        grid=(batch, nb),
        in_specs=[zspec(OFF_HQ), zspec(OFF_HF), zspec(OFF_HI), zspec(OFF_HG),
                  pl.BlockSpec((1, W), lambda b, j: (0, 0)),
                  pl.BlockSpec((1, HG_D), lambda b, j: (0, 0))],
        out_specs=pl.BlockSpec((tb, W), lambda b, j: (b * nb + j, 0)),
        out_shape=jax.ShapeDtypeStruct((N, W), BF16),
        scratch_shapes=[pltpu.VMEM((HG_HEADS, HG_D, HG_D), F32)],
        compiler_params=_cparams(("parallel", "arbitrary")),
        name="hgrn2",
    )(z, z, z, z, lb, onorm)


def _ssm_kernel(u_ref, bm_ref, c_ref, kt_ref, p1r_ref, p1i_ref, p2r_ref, p2i_ref,
                alr_ref, ali_ref, d_ref, o_ref, xr_ref, xi_ref, xpr_ref, xpi_ref, *, tt):
    LC = SSM_CHUNK
    nc = tt // LC
    W = SSM_GROUPS * SSM_N

    @pl.when(pl.program_id(1) == 0)
    def _():
        xr_ref[...] = jnp.zeros_like(xr_ref)
        xi_ref[...] = jnp.zeros_like(xi_ref)

    u = u_ref[...]
    ub = u.astype(BF16)
    nb, cw, sw = bm_ref.shape[0], bm_ref.shape[1], bm_ref.shape[2] // 2
    bu = [_dot(ub[:, s * cw:(s + 1) * cw], bm_ref[s]) for s in range(nb)]
    bur = jnp.concatenate([x[:, :sw] for x in bu], axis=1).reshape(nc, LC, W)
    bui = jnp.concatenate([x[:, sw:] for x in bu], axis=1).reshape(nc, LC, W)
    p1r, p1i = p1r_ref[...], p1i_ref[...]
    member = (lax.broadcasted_iota(jnp.int32, (nc, tt), 1) // LC
              == lax.broadcasted_iota(jnp.int32, (nc, tt), 0))
    member = jnp.where(member, 1.0, 0.0).astype(BF16)
    vr = _dot(member, (bur * p1r - bui * p1i).reshape(tt, W).astype(BF16))
    vi = _dot(member, (bur * p1i + bui * p1r).reshape(tt, W).astype(BF16))

    alr, ali = alr_ref[...], ali_ref[...]
    xr, xi = xr_ref[...], xi_ref[...]
    for c in range(nc):
        xpr_ref[c:c + 1, :] = xr
        xpi_ref[c:c + 1, :] = xi
        nr = alr * xr - ali * xi + vr[c:c + 1, :]
        ni = alr * xi + ali * xr + vi[c:c + 1, :]
        xr, xi = nr, ni
    xr_ref[...] = xr
    xi_ref[...] = xi

    xpr = xpr_ref[...][:, None, :]
    xpi = xpi_ref[...][:, None, :]
    p2r, p2i = p2r_ref[...], p2i_ref[...]
    zr = (p2r * xpr - p2i * xpi).reshape(tt, W).astype(BF16)
    zi = (p2r * xpi + p2i * xpr).reshape(tt, W).astype(BF16)
    y = jnp.concatenate(
        [_dot(jnp.concatenate([zr[:, s * sw:(s + 1) * sw], zi[:, s * sw:(s + 1) * sw]], axis=1), c_ref[s])
         for s in range(nb)], axis=1)

    tmod = lax.broadcasted_iota(jnp.int32, u.shape, 0) % LC
    nkb, kw = kt_ref.shape[1], kt_ref.shape[2]
    taps = [jnp.zeros((tt, kw), F32)] * nkb
    for tau in range(LC):
        if tau == 0:
            us = ub
        else:
            us = jnp.where(tmod >= tau, pltpu.roll(u, tau, 0), 0.0).astype(BF16)
        taps = [taps[s] + _dot(us[:, s * kw:(s + 1) * kw], kt_ref[tau, s]) for s in range(nkb)]
    y = y + jnp.concatenate(taps, axis=1) + d_ref[...] * u
    g = 0.5 * y * (1.0 + jnp.tanh(0.7978845608028654 * (y + 0.044715 * (y * y * y))))
    o_ref[...] = g.astype(o_ref.dtype)


def _ssm_params(a_re, a_im, log_dt, b_re, b_im, c_re, c_im):
    G, P, N, LC = SSM_GROUPS, SSM_P, SSM_N, SSM_CHUNK
    hp = lax.Precision.HIGHEST
    a_re = jnp.minimum(a_re.astype(F32), -1e-4)
    a_im = a_im.astype(F32)
    dt = jnp.exp(log_dt.astype(F32))[:, None]
    mag = jnp.exp(dt * a_re)
    ab_re, ab_im = mag * jnp.cos(dt * a_im), mag * jnp.sin(dt * a_im)
    den = a_re * a_re + a_im * a_im
    nr = ab_re - 1.0
    z_re = (nr * a_re + ab_im * a_im) / den
    z_im = (ab_im * a_re - nr * a_im) / den
    b_re, b_im = b_re.astype(F32), b_im.astype(F32)
    bb_re = z_re[..., None] * b_re - z_im[..., None] * b_im
    bb_im = z_re[..., None] * b_im + z_im[..., None] * b_re
    kk = jnp.arange(LC + 1, dtype=F32)[:, None, None]
    pm = jnp.exp(kk * dt * a_re)
    pw_re, pw_im = pm * jnp.cos(kk * dt * a_im), pm * jnp.sin(kk * dt * a_im)
    def bdiag(src):
        n, a, b = src.shape
        x = jnp.broadcast_to(src.reshape(n * a, 1, b), (n * a, n, b)).reshape(n * a, n * b)
        rg = jnp.arange(n * a)[:, None] // a
        cg = jnp.arange(n * b)[None, :] // b
        return jnp.where(rg == cg, x, 0.0)

    def bundles(src, per):
        return jax.vmap(bdiag)(src.reshape(G // per, per, *src.shape[1:]))

    bmat = jnp.concatenate([bundles(bb_re.transpose(0, 2, 1), SSM_BUNDLE),
                            bundles(bb_im.transpose(0, 2, 1), SSM_BUNDLE)], axis=2)
    c_re, c_im = c_re.astype(F32), c_im.astype(F32)
    cmat = jnp.concatenate([bundles(c_re.transpose(0, 2, 1), SSM_BUNDLE),
                            -bundles(c_im.transpose(0, 2, 1), SSM_BUNDLE)], axis=1)
    t_re = pw_re[:LC, :, :, None] * bb_re[None] - pw_im[:LC, :, :, None] * bb_im[None]
    t_im = pw_re[:LC, :, :, None] * bb_im[None] + pw_im[:LC, :, :, None] * bb_re[None]
    kt = (jnp.einsum('gqn,tgnp->tgpq', c_re, t_re, precision=hp)
          - jnp.einsum('gqn,tgnp->tgpq', c_im, t_im, precision=hp))
    ktm = jax.vmap(lambda k: bundles(k, SSM_TAP_BUNDLE))(kt)
    flat = lambda x: x.reshape(x.shape[0], G * N)
    p1r, p1i = flat(pw_re[LC - 1::-1][:LC]), flat(pw_im[LC - 1::-1][:LC])
    p2r, p2i = flat(pw_re[1:LC + 1]), flat(pw_im[1:LC + 1])
    alr, ali = flat(pw_re[LC:LC + 1]), flat(pw_im[LC:LC + 1])
    return (bmat.astype(BF16), cmat.astype(BF16), ktm.astype(BF16), p1r, p1i, p2r, p2i, alr, ali)


def _ssm(z, params, d_skip, batch, seq):
    N = z.shape[0]
    tt = 256
    nb = seq // tt
    CW = SSM_GROUPS * SSM_P
    W = SSM_GROUPS * SSM_N
    LC = SSM_CHUNK
    bmat, cmat, ktm, p1r, p1i, p2r, p2i, alr, ali = params
    full = lambda a: pl.BlockSpec(a.shape, lambda b, j, _n=a.ndim: (0,) * _n)
    return pl.pallas_call(
        functools.partial(_ssm_kernel, tt=tt),
        grid=(batch, nb),
        in_specs=[pl.BlockSpec((tt, CW), lambda b, j: (b * nb + j, OFF_SU // CW)),
                  full(bmat), full(cmat), full(ktm), full(p1r), full(p1i),
                  full(p2r), full(p2i), full(alr), full(ali),
                  pl.BlockSpec((1, CW), lambda b, j: (0, 0))],
        out_specs=pl.BlockSpec((tt, CW), lambda b, j: (b * nb + j, 0)),
        out_shape=jax.ShapeDtypeStruct((N, CW), BF16),
        scratch_shapes=[pltpu.VMEM((1, W), F32), pltpu.VMEM((1, W), F32),
                        pltpu.VMEM((tt // LC, W), F32), pltpu.VMEM((tt // LC, W), F32)],
        compiler_params=_cparams(("parallel", "arbitrary")),
        name="s5_ssm",
    )(z, bmat, cmat, ktm, p1r, p1i, p2r, p2i, alr, ali, d_skip)


def _t5_bucket(dist):
    n = jnp.maximum(dist, 0)
    max_exact = REL_BUCKETS // 2
    nf = jnp.maximum(n, 1).astype(F32)
    large = max_exact + (jnp.log(nf / max_exact) / math.log(REL_MAX_DIST / max_exact)
                         * (REL_BUCKETS - max_exact)).astype(jnp.int32)
    large = jnp.minimum(large, REL_BUCKETS - 1)
    return jnp.where(n < max_exact, n, large)


def _bias_kernel(tab_ref, bk_ref, o_ref):
    h = pl.program_id(0)
    bk = bk_ref[...]
    base = tab_ref[REL_BUCKETS - 1, h]
    acc = jnp.full(bk.shape, NEG, F32)
    for k in range(REL_BUCKETS):
        acc = jnp.where(bk == k, (tab_ref[k, h] - base) * LOG2E, acc)
    o_ref[0] = acc


def _bias_tables(rel_table, seq):
    j = jnp.arange(QT)[:, None]
    i = jnp.arange(QT)[None, :]
    d0 = i - j
    b0 = jnp.where(d0 >= 0, _t5_bucket(d0), REL_BUCKETS)
    b1 = _t5_bucket(QT + i - j)
    n = jnp.arange(QT)[:, None]
    t = jnp.arange(seq)[None, :]
    n_cmp = (seq - CMP_LEN) // CMP_STRIDE + 1
    dc = t - (n * CMP_STRIDE + CMP_LEN - 1)
    bc = jnp.where((dc >= 0) & (n < n_cmp), _t5_bucket(dc), REL_BUCKETS)
    nh = rel_table.shape[1]
    tab = rel_table.astype(F32)

    def expand(bk):
        rows, lanes = bk.shape
        return pl.pallas_call(
            _bias_kernel,
            grid=(nh,),
            in_specs=[pl.BlockSpec(memory_space=pltpu.SMEM),
                      pl.BlockSpec((rows, lanes), lambda h: (0, 0))],
            out_specs=pl.BlockSpec((1, rows, lanes), lambda h: (h, 0, 0)),
            out_shape=jax.ShapeDtypeStruct((nh, rows, lanes), F32),
            compiler_params=_cparams(("arbitrary",)),
            name="nsa_bias_tables",
        )(tab, bk.astype(jnp.int32))

    near = expand(jnp.concatenate([b0, b1], axis=0)).reshape(nh, 2, QT, QT)
    return near, expand(bc)


def _head_rms(x, gain):
    return x * lax.rsqrt(jnp.mean(x * x, axis=-1, keepdims=True) + EPS) * gain


def _nsa_prep_kernel(kc_ref, vc_ref, ks_ref, vs_ref, kw_ref, vw_ref, kg_ref, pek_ref, pev_ref,
                     phik_ref, phiv_ref, oks_ref, ovs_ref, okw_ref, ovw_ref, okc_ref, ovc_ref,
                     x_ref, *, seq):
    DH = NSA_DH
    kg = kg_ref[...]
    srow = lax.broadcasted_iota(jnp.int32, (seq, DH), 0)
    lane = lax.broadcasted_iota(jnp.int32, (seq, DH), 1)
    onehot = jnp.where(lane == srow // SEL_LEN, 1.0, 0.0).astype(BF16)
    nkt = seq // QT
    npad = WIN // QT

    def value_tiles(v):
        return jnp.concatenate([v, jnp.ones_like(v)], axis=1).T[:NSA_VR].astype(BF16)

    for g in range(NSA_G):
        ls = slice(g * DH, (g + 1) * DH)
        ksn = _head_rms(ks_ref[:, ls], kg).astype(BF16)
        oks_ref[0, g] = jnp.concatenate([ksn, onehot], axis=1)
        okw_ref[0, g, 0:WIN, :] = jnp.zeros((WIN, DH), BF16)
        okw_ref[0, g, WIN:WIN + seq, :] = _head_rms(kw_ref[:, ls], kg).astype(BF16)
        vst = value_tiles(vs_ref[:, ls])
        vwt = value_tiles(vw_ref[:, ls])
        ovw_ref[0, g, 0:npad] = jnp.zeros((npad, NSA_VR, QT), BF16)
        for kt in range(nkt):
            ovs_ref[0, g, kt] = vst[:, kt * QT:(kt + 1) * QT]
            ovw_ref[0, g, npad + kt] = vwt[:, kt * QT:(kt + 1) * QT]
        for src_ref, pe_ref, phi_ref, is_k in ((kc_ref, pek_ref, phik_ref, True),
                                               (vc_ref, pev_ref, phiv_ref, False)):
            x_ref[...] = src_ref[:, ls]
            lo = jnp.zeros((QT, DH), F32)
            hi = jnp.zeros((QT, DH), F32)
            for r in range(CMP_STRIDE):
                xr = x_ref[pl.ds(r, QT, stride=CMP_STRIDE), :]
                lo = lo + _dot((xr + pe_ref[r:r + 1, :]).astype(BF16), phi_ref[r])
                r2 = r + CMP_STRIDE
                hi = hi + _dot((xr + pe_ref[r2:r2 + 1, :]).astype(BF16), phi_ref[r2])
            acc = lo + pltpu.roll(hi, QT - 1, 0)
            if is_k:
                okc_ref[0, g] = _head_rms(acc, kg).astype(BF16)
            else:
                ovc_ref[0, g] = value_tiles(acc)


def _nsa_prep(z, k_gain, pe_k, pe_v, phi_k, phi_v, batch, seq):
    G, DH = NSA_G, NSA_DH
    KW = G * DH

    def zspec(off):
        return pl.BlockSpec((seq, KW), lambda b: (b, off // KW))

    full = lambda a: pl.BlockSpec(a.shape, lambda b, _n=a.ndim: (0,) * _n)

    def ospec(*dims):
        return pl.BlockSpec((1, G) + dims, lambda b, _n=len(dims): (b, 0) + (0,) * _n)

    def oshape(*dims):
        return jax.ShapeDtypeStruct((batch, G) + dims, BF16)

    outs = [(seq, 2 * DH), (seq // QT, NSA_VR, QT), (seq + WIN, DH), ((seq + WIN) // QT, NSA_VR, QT),
            (QT, DH), (NSA_VR, QT)]
    return pl.pallas_call(
        functools.partial(_nsa_prep_kernel, seq=seq),
        grid=(batch,),
        in_specs=[zspec(OFF_KC), zspec(OFF_VC), zspec(OFF_KS), zspec(OFF_VS), zspec(OFF_KW),
                  zspec(OFF_VW), full(k_gain), full(pe_k), full(pe_v), full(phi_k), full(phi_v)],
        out_specs=[ospec(*d) for d in outs],
        out_shape=[oshape(*d) for d in outs],
        scratch_shapes=[pltpu.VMEM((seq, DH), F32)],
        compiler_params=_cparams(("parallel",)),
        name="nsa_kv_prep",
    )(z, z, z, z, z, z, k_gain, pe_k, pe_v, phi_k, phi_v)


def _nsa_kernel(q_ref, gt_ref, ks_ref, vs_ref, kw_ref, vw_ref, kc_ref, vc_ref, cb_ref, nb_ref,
                qg_ref, ov_ref, o_ref):
    DH, HPG, TQ = NSA_DH, NSA_HPG, NSA_TQ
    NR = TQ // QT
    RB = HPG * QT
    R = NR * RB
    NSEL = 32
    nwt = WIN // QT
    qi = pl.program_id(1)
    t0 = qi * TQ

    def cols(r):
        return slice(r * RB, (r + 1) * RB)

    def to_cols(per_head):
        return jnp.concatenate([per_head[h][:, r * QT:(r + 1) * QT] for r in range(NR) for h in range(HPG)],
                               axis=1)

    def krows(x, c):
        return x[c * QT:(c + 1) * QT]

    def vslab(ref, tile0, n):
        return jnp.concatenate([ref[0, 0, tile0 + c] for c in range(n)], axis=1)

    def ktile(ref, start, size):
        return ref[0, 0, pl.ds(pl.multiple_of(start, QT), size), :]

    qt = q_ref[...].T
    gain = jnp.concatenate([qg_ref[...]] * NR, axis=1) * (DH ** -0.5 * LOG2E)
    qh = []
    for h in range(HPG):
        x = qt[h * DH:(h + 1) * DH]
        qh.append(x * lax.rsqrt(jnp.mean(x * x, axis=0, keepdims=True) + EPS) * gain)
    qb = to_cols(qh).astype(BF16)
    nb0 = jnp.concatenate([nb_ref[h, 0] for h in range(HPG)], axis=1)
    nb1 = jnp.concatenate([nb_ref[h, 1] for h in range(HPG)], axis=1)

    jj = lax.broadcasted_iota(jnp.int32, (QT, RB), 0)
    ii = lax.broadcasted_iota(jnp.int32, (QT, RB), 1) % QT
    acc_w = []
    for r in range(NR):
        s = _dot(ktile(kw_ref, t0 + r * QT, WIN + QT), qb[:, cols(r)])
        parts = [krows(s, c) for c in range(nwt + 1)]
        parts[nwt] = parts[nwt] + nb0
        parts[nwt - 1] = parts[nwt - 1] + nb1
        parts[0] = jnp.where(jj > ii, parts[0], NEG)
        for c in range(nwt - r):
            parts[c] = jnp.where(qi > 0, parts[c], NEG)
        s = jnp.concatenate(parts, axis=0)
        p = jnp.exp2(s - jnp.max(s, axis=0, keepdims=True)).astype(BF16)
        acc_w.append(_dot(vslab(vw_ref, qi * NR + r, nwt + 1), p))
    acc_w = jnp.concatenate(acc_w, axis=1)

    sc = _dot(kc_ref[0, 0], qb) + to_cols([cb_ref[h] for h in range(HPG)])
    mc = jnp.max(sc, axis=0, keepdims=True)
    ec = jnp.where(sc > 0.5 * NEG, jnp.exp2(sc - mc), 0.0)
    lc = jnp.sum(ec, axis=0, keepdims=True)
    pc = ec / jnp.where(lc > 0.0, lc, 1.0)
    o_cmp = _dot(vc_ref[0, 0], pc.astype(BF16))[:DH]

    psum = jnp.concatenate(
        [sum(pc[:, r * RB + h * QT:r * RB + (h + 1) * QT] for h in range(HPG)) for r in range(NR)], axis=1)
    ovm = ov_ref[...]
    hi, mid, lo = _split3(psum)
    imp = (_dot(ovm, hi) + _dot(ovm, mid) + _dot(ovm, lo))[0:NSEL]
    jrow = lax.broadcasted_iota(jnp.int32, (NSEL, TQ), 0)
    blk = (t0 + lax.broadcasted_iota(jnp.int32, (NSEL, TQ), 1)) // SEL_LEN
    forced = (jrow == 0) | (jrow == blk) | (jrow == blk - 1)
    imp = jnp.where(forced, FORCE_SCORE, imp)
    imp = jnp.where(jrow <= blk, imp, NEG)
    cnt = jnp.zeros((NSEL, TQ), F32)
    for jp in range(NSEL):
        rowv = imp[jp:jp + 1, :]
        beats = (rowv > imp) | ((rowv == imp) & (jrow > jp))
        cnt = cnt + jnp.where(beats, 1.0, 0.0)
    selb = jnp.where(cnt < float(SEL_TOPN), 0.0, NEG)
    qaug = jnp.concatenate([qb, to_cols([selb] * HPG).astype(BF16),
                            jnp.zeros((2 * DH - DH - NSEL, R), BF16)], axis=0)

    def step(scores, vt, state):
        m_new = [jnp.maximum(state[r][0], jnp.max(scores[r], axis=0, keepdims=True)) for r in range(NR)]
        out = []
        for r in range(NR):
            p = jnp.exp2(scores[r] - m_new[r]).astype(BF16)
            acc = state[r][1] * jnp.exp2(state[r][0] - m_new[r]) + _dot(vt[r], p)
            out.append((m_new[r], acc))
        return out

    scores, vts = [], []
    for r in range(NR):
        s = _dot(ktile(ks_ref, t0, (r + 1) * QT), qaug[:, cols(r)])
        parts = [krows(s, c) for c in range(r + 1)]
        parts[r] = parts[r] + nb0
        if r >= 1:
            parts[r - 1] = parts[r - 1] + nb1
        scores.append(jnp.concatenate(parts, axis=0))
        vts.append(vslab(vs_ref, qi * NR, r + 1))
    state = step(scores, vts, [(jnp.full((1, RB), NEG, F32), jnp.zeros((NSA_VR, RB), F32))] * NR)

    def past_tile(kt, flat):
        kk = ktile(ks_ref, kt * TQ, TQ)
        vt = vslab(vs_ref, kt * NR, NR)
        scores = [_dot(kk, qaug[:, cols(r)]) for r in range(NR)]
        corner = scores[0][TQ - QT:] + jnp.where(kt == qi - 1, nb1, 0.0)
        scores[0] = jnp.concatenate([scores[0][:TQ - QT], corner], axis=0)
        new = step(scores, [vt] * NR, [(flat[2 * r], flat[2 * r + 1]) for r in range(NR)])
        return tuple(x for pair in new for x in pair)

    flat = lax.fori_loop(0, qi, past_tile, tuple(x for pair in state for x in pair))
    acc_s = jnp.concatenate([flat[2 * r + 1] for r in range(NR)], axis=1)

    gt = _sigmoid(gt_ref[...]).T

    def gate_row(br):
        return to_cols([gt[br * HPG + h:br * HPG + h + 1, :] for h in range(HPG)])

    def inv_l(acc):
        l = acc[DH:DH + 1, :]
        return 1.0 / jnp.where(l > 0.0, l, 1.0)

    o = (gate_row(0) * o_cmp + (gate_row(1) * inv_l(acc_s)) * acc_s[:DH]
         + (gate_row(2) * inv_l(acc_w)) * acc_w[:DH])
    o_hd = jnp.concatenate(
        [jnp.concatenate([o[:, r * RB + h * QT:r * RB + (h + 1) * QT] for r in range(NR)], axis=1)
         for h in range(HPG)], axis=0)
    o_ref[...] = o_hd.T.astype(o_ref.dtype)


def _nsa(z, zg, prep, bias, q_gain, ovm, batch, seq):
    N = z.shape[0]
    G, HPG, DH, TQ = NSA_G, NSA_HPG, NSA_DH, NSA_TQ
    nq = seq // TQ
    oks, ovs, okw, ovw, okc, ovc = prep
    nbias, cbias = bias
    QW = HPG * DH
    qg = jnp.broadcast_to(q_gain.reshape(DH, 1), (DH, QT))

    def kvspec(a):
        return pl.BlockSpec((1, 1) + a.shape[2:], lambda p, i, _n=a.ndim - 2: (p // G, p % G) + (0,) * _n)

    return pl.pallas_call(
        _nsa_kernel,
        grid=(batch * G, nq),
        in_specs=[pl.BlockSpec((TQ, QW), lambda p, i: ((p // G) * nq + i, OFF_NQ // QW + p % G)),
                  pl.BlockSpec((TQ, 128), lambda p, i: ((p // G) * nq + i, p % G)),
                  kvspec(oks), kvspec(ovs), kvspec(okw), kvspec(ovw), kvspec(okc), kvspec(ovc),
                  pl.BlockSpec((HPG, QT, TQ), lambda p, i: (p % G, 0, i)),
                  pl.BlockSpec((HPG, 2, QT, QT), lambda p, i: (p % G, 0, 0, 0)),
                  pl.BlockSpec((DH, QT), lambda p, i: (0, 0)),
                  pl.BlockSpec((QT, QT), lambda p, i: (0, 0))],
        out_specs=pl.BlockSpec((TQ, QW), lambda p, i: ((p // G) * nq + i, p % G)),
        out_shape=jax.ShapeDtypeStruct((N, G * QW), BF16),
        compiler_params=_cparams(("parallel", "arbitrary")),
        name="nsa_attention",
    )(z, zg, oks, ovs, okw, ovw, okc, ovc, cbias, nbias, qg, ovm)


def _merge_kernel(h_ref, mod_ref, za_ref, zb_ref, zc_ref, oa_ref, ys_ref, oc_ref,
                  wa_ref, wb_ref, wc_ref, wo_ref, o_ref):
    D = h_ref.shape[1]
    ya = _dot(oa_ref[...], wa_ref[...])
    zz = _dot(ys_ref[...], wb_ref[...])
    yb = zz[:, :D] * _sigmoid(zz[:, D:])
    yc = _dot(oc_ref[...], wc_ref[...])
    merged = _sigmoid(za_ref[...]) * ya + _sigmoid(zb_ref[...]) * yb + _sigmoid(zc_ref[...]) * yc
    o_ref[...] = h_ref[...] + mod_ref[0, 5:6, :] * _dot(merged.astype(BF16), wo_ref[...])


def _merge(h, mod, z, oa, ys, oc, wa, wb, wc, wo, seq):
    N, D = h.shape
    tm = 256
    tpb = seq // tm
    row = lambda w: pl.BlockSpec((tm, w), lambda i: (i, 0))
    res = lambda a: pl.BlockSpec(a.shape, lambda i: (0, 0), pipeline_mode=pl.Buffered(1))
    return pl.pallas_call(
        _merge_kernel,
        grid=(N // tm,),
        in_specs=[row(D),
                  pl.BlockSpec((1, 9, D), lambda i: (i // tpb, 0, 0)),
                  pl.BlockSpec((tm, D), lambda i: (i, OFF_ZA // D)),
                  pl.BlockSpec((tm, D), lambda i: (i, OFF_ZB // D)),
                  pl.BlockSpec((tm, D), lambda i: (i, OFF_ZC // D)),
                  row(oa.shape[1]), row(ys.shape[1]), row(oc.shape[1]),
                  res(wa), res(wb), res(wc), res(wo)],
        out_specs=row(D),
        out_shape=jax.ShapeDtypeStruct((N, D), F32),
        compiler_params=_cparams(("parallel",)),
        name="mix_merge",
    )(h, mod, z, z, z, oa, ys, oc, wa, wb, wc, wo)


def _permute_w_in(w):
    gate0 = 4 * 512 + 512 + 1024 + 6 * 256
    ngate = 3 * NSA_G * NSA_HPG
    L, D = w.shape[:2]
    wb = w.astype(BF16)
    main = jnp.concatenate([wb[:, :, gate0 + ngate:], wb[:, :, :gate0]], axis=2)
    gw = wb[:, :, gate0:gate0 + ngate].reshape(L, D, 3, NSA_G, NSA_HPG)
    gw = gw.transpose(0, 1, 3, 2, 4).reshape(L, D, NSA_G, 3 * NSA_HPG)
    gw = jnp.pad(gw, ((0, 0), (0, 0), (0, 0), (0, 128 - 3 * NSA_HPG))).reshape(L, D, NSA_G * 128)
    return main, gw


def _overlap_matrix():
    j = jnp.arange(QT)[:, None]
    n = jnp.arange(QT)[None, :]
    st = n * CMP_STRIDE
    ov = (st < j * SEL_LEN + SEL_LEN) & (st + CMP_LEN > j * SEL_LEN)
    return jnp.where(ov, 1.0, 0.0).astype(BF16)


def kernel(x, c, ada_w, ada_b, norm_g, ffn1_wi, ffn1_wo, ffn2_wi, ffn2_wo, w_in, hg_lb_logits,
           hg_onorm, hg_proj, ssm_a_re, ssm_a_im, ssm_log_dt, ssm_b_re, ssm_b_im, ssm_c_re,
           ssm_c_im, ssm_d, ssm_glu_w, nsa_q_gain, nsa_k_gain, nsa_pe_k, nsa_pe_v, nsa_phi_k,
           nsa_phi_v, nsa_proj, rel_table, w_out):
    B, S, D = x.shape
    L = ada_w.shape[0]
    N = B * S
    assert S % 512 == 0 and S // SEL_LEN == 32 and S // QT == 16
    lb_cum = jnp.cumsum(jax.nn.softmax(hg_lb_logits.astype(F32), axis=0), axis=0)
    lower_bounds = lb_cum - lb_cum[0:1]
    mods = _mods(c, ada_w, ada_b).reshape(L, B, 9, D)
    bias = _bias_tables(rel_table, S)
    ovm = _overlap_matrix()
    w_main, w_gate = _permute_w_in(w_in)
    h = x.reshape(N, D)
    for l in range(L):
        mod = mods[l]
        h = _ffn(h, mod, norm_g[l, 0:1], ffn1_wi, ffn1_wo, l, 0, S)
        z, zg = _win(h, mod, norm_g[l, 1:2], w_main, w_gate, l, S)
        oa = _hgrn(z, lower_bounds[l:l + 1], hg_onorm[l:l + 1], B, S)
        sp = _ssm_params(ssm_a_re[l], ssm_a_im[l], ssm_log_dt[l], ssm_b_re[l], ssm_b_im[l],
                         ssm_c_re[l], ssm_c_im[l])
        ys = _ssm(z, sp, ssm_d[l:l + 1], B, S)
        prep = _nsa_prep(z, nsa_k_gain[l:l + 1], nsa_pe_k[l], nsa_pe_v[l],
                         nsa_phi_k[l].astype(BF16), nsa_phi_v[l].astype(BF16), B, S)
        oc = _nsa(z, zg, prep, bias, nsa_q_gain[l:l + 1], ovm, B, S)
        h = _merge(h, mod, z, oa, ys, oc, hg_proj[l].astype(BF16), ssm_glu_w[l].astype(BF16),
                   nsa_proj[l].astype(BF16), w_out[l].astype(BF16), S)
        h = _ffn(h, mod, norm_g[l, 2:3], ffn2_wi, ffn2_wo, l, 6, S)
    return h.reshape(B, S, D)
```
